```python
import jax, jax.numpy as jnp
from jax import lax
import numpy as np

D_MODEL = 2048
BATCH = 8
SEQ = 2048
DEPTH = 2
DEC_BATCH = 128
DEC_SEQ = 4
PAST_LEN = 8192
PAGE_SIZE = 128

D_MIX = D_MODEL
HEAD_DIM = 64
D_ATT = D_MIX // 2
N_HEADS = D_ATT // HEAD_DIM
N_KV_HEADS = 4
GROUP = N_HEADS // N_KV_HEADS
D_KV = N_KV_HEADS * HEAD_DIM
WINDOW = 128
ROPE_THETA = 10000.0
D_POOL = D_MIX // 4
POOL_WINDOWS = (2, 4, 8, 16)
N_POOL_GROUPS = len(POOL_WINDOWS)
POOL_GROUP_DIM = D_POOL // N_POOL_GROUPS
POOL_HIST = max(POOL_WINDOWS) - 1
D_SGU = D_MIX - D_ATT - D_POOL
CHUNK = 128
N_SGU_HEADS = 4
SGU_HEAD_DIM = D_SGU // N_SGU_HEADS
SPLIT_SIZES = (D_ATT, D_KV, D_KV, D_ATT, D_POOL, D_POOL, D_SGU, D_SGU, D_SGU)
D_IN = sum(SPLIT_SIZES)
EPS = 1e-6

kernel_name = "hymba_swa_pool_sgu_step"


def _rmsnorm(x, g):
    xf = x.astype(jnp.float32)
    xf = xf * lax.rsqrt(jnp.mean(xf * xf, axis=-1, keepdims=True) + EPS)
    return (xf * g.astype(jnp.float32)).astype(x.dtype)


def _layernorm(x, g, b):
    xf = x.astype(jnp.float32)
    mu = jnp.mean(xf, axis=-1, keepdims=True)
    var = jnp.mean(jnp.square(xf - mu), axis=-1, keepdims=True)
    y = (xf - mu) * lax.rsqrt(var + EPS) * g.astype(jnp.float32) + b.astype(jnp.float32)
    return y.astype(x.dtype)


def _rope(x, pos):
    half = HEAD_DIM // 2
    inv = ROPE_THETA ** (-jnp.arange(0, HEAD_DIM, 2, dtype=jnp.float32) / HEAD_DIM)
    ang = pos.astype(jnp.float32)[:, None] * inv[None, :]
    cos = jnp.cos(ang)[:, None, :]
    sin = jnp.sin(ang)[:, None, :]
    xf = x.astype(jnp.float32)
    x1, x2 = xf[..., :half], xf[..., half:]
    return jnp.concatenate([x1 * cos - x2 * sin, x2 * cos + x1 * sin], axis=-1).astype(x.dtype)


def _project(x, pre_g, w_in, pos):
    h = _rmsnorm(x, pre_g)
    z = h @ w_in
    offs = np.cumsum(SPLIT_SIZES)[:-1].tolist()
    q, k, v, g_att, xb, g_pool, u, vs, g_sgu = jnp.split(z, offs, axis=-1)
    B, S = x.shape[:2]
    q = _rope(q.reshape(B, S, N_HEADS, HEAD_DIM), pos)
    k = _rope(k.reshape(B, S, N_KV_HEADS, HEAD_DIM), pos)
    v = v.reshape(B, S, N_KV_HEADS, HEAD_DIM)
    return q, k, v, g_att, xb, g_pool, u, vs, g_sgu


def _sink_attend(q, k, v, mask, sinks):
    s = jnp.einsum('...qkgd,...skd->...kgqs', q, k,
                   preferred_element_type=jnp.float32) * (HEAD_DIM ** -0.5)
    s = jnp.where(mask, s, -jnp.inf)
    sink = sinks.astype(jnp.float32).reshape(N_KV_HEADS, GROUP, 1, 1)
    m = jnp.maximum(jnp.max(s, axis=-1, keepdims=True), sink)
    p = jnp.exp(s - m)
    p = p / (jnp.sum(p, axis=-1, keepdims=True) + jnp.exp(sink - m))
    return jnp.einsum('...kgqs,...skd->...qkgd', p.astype(v.dtype), v)


def _window_attn_prompt(q, k, v, sinks):
    B, S = q.shape[:2]
    nb = S // WINDOW
    qb = q.reshape(B, nb, WINDOW, N_KV_HEADS, GROUP, HEAD_DIM)
    kb = k.reshape(B, nb, WINDOW, N_KV_HEADS, HEAD_DIM)
    vb = v.reshape(B, nb, WINDOW, N_KV_HEADS, HEAD_DIM)
    pad = ((0, 0), (1, 0), (0, 0), (0, 0), (0, 0))
    kk = jnp.concatenate([jnp.pad(kb[:, :-1], pad), kb], axis=2)
    vv = jnp.concatenate([jnp.pad(vb[:, :-1], pad), vb], axis=2)
    i = jnp.arange(WINDOW)[:, None]
    j = jnp.arange(2 * WINDOW)[None, :]
    diff = i + WINDOW - j
    band = (diff >= 0) & (diff < WINDOW)
    mask = band[None] & ((jnp.arange(nb)[:, None, None] > 0) | (j[None] >= WINDOW))
    out = _sink_attend(qb, kk, vv, mask[:, None, None], sinks)
    return out.reshape(B, S, D_ATT)


def _window_attn_sample(q, k_new, v_new, k_hist, v_hist, sinks):
    Bd, T = q.shape[:2]
    Lb = k_hist.shape[1]
    k_all = jnp.concatenate([k_hist, k_new], axis=1)
    v_all = jnp.concatenate([v_hist, v_new], axis=1)
    qpos = PAST_LEN + jnp.arange(T)
    kpos = PAST_LEN - Lb + jnp.arange(Lb + T)
    d = qpos[:, None] - kpos[None, :]
    mask = (d >= 0) & (d < WINDOW)
    out = _sink_attend(q.reshape(Bd, T, N_KV_HEADS, GROUP, HEAD_DIM), k_all, v_all, mask, sinks)
    return out.reshape(Bd, T, D_ATT), k_all[:, T:], v_all[:, T:]


def _pool_mix(xb_ext, start, pool_w, pool_scale):
    B = xb_ext.shape[0]
    n = xb_ext.shape[1] - POOL_HIST
    xf = xb_ext.astype(jnp.float32)
    cs = jnp.pad(jnp.cumsum(xf, axis=1), ((0, 0), (1, 0), (0, 0)))
    hi = cs[:, POOL_HIST + 1:]
    pos = start + jnp.arange(n)
    outs = []
    for g, w in enumerate(POOL_WINDOWS):
        sl = slice(g * POOL_GROUP_DIM, (g + 1) * POOL_GROUP_DIM)
        lo = cs[:, POOL_HIST + 1 - w:POOL_HIST + 1 - w + n, sl]
        cnt = jnp.minimum(w, pos + 1).astype(jnp.float32)[:, None]
        outs.append((hi[..., sl] - lo) / cnt - xf[:, POOL_HIST:, sl])
    pooled = jnp.stack(outs, axis=2).astype(pool_w.dtype)
    mixed = jnp.einsum('bngc,gcd->bngd', pooled, pool_w)
    return mixed.reshape(B, n, D_POOL) * pool_scale


def _sgu(u, v, ln_g, ln_b, w_s, b_s, L):
    B, S = u.shape[:2]
    vn = _layernorm(v, ln_g, ln_b)
    vc = vn.reshape(B, S // L, L, N_SGU_HEADS, SGU_HEAD_DIM)
    tri = jnp.tril(jnp.ones((L, L), dtype=bool))
    wm = jnp.where(tri, w_s[:, :L, :L], 0)
    mixed = jnp.einsum('hts,bnshd->bnthd', wm, vc) + b_s[:, :L].T[:, :, None]
    out = u.reshape(B, S // L, L, N_SGU_HEADS, SGU_HEAD_DIM) * mixed
    return out.reshape(B, S, D_SGU), vn


def _finish(x, o_att, g_att, o_pool, g_pool, o_sgu, g_sgu, w_out, post_g):
    y = jnp.concatenate([o_att * jax.nn.silu(g_att),
                         o_pool * jax.nn.silu(g_pool),
                         o_sgu * jax.nn.silu(g_sgu)], axis=-1) @ w_out
    return x + _rmsnorm(y, post_g)


def setup_inputs(seed: int = 0) -> dict:
    key = jax.random.key(seed)
    ks = jax.random.split(key, 16)
    nrm = jax.random.normal
    win = min(WINDOW, PAST_LEN)
    return {
        "x_prompt": nrm(ks[0], (BATCH, SEQ, D_MODEL), jnp.float32),
        "x_sample": nrm(ks[1], (DEC_BATCH, DEC_SEQ, D_MODEL), jnp.float32),
        "cache_k": nrm(ks[2], (DEPTH, DEC_BATCH, win, N_KV_HEADS, HEAD_DIM), jnp.float32),
        "cache_v": nrm(ks[3], (DEPTH, DEC_BATCH, win, N_KV_HEADS, HEAD_DIM), jnp.float32),
        "state_pool": nrm(ks[4], (DEPTH, DEC_BATCH, POOL_HIST, D_POOL), jnp.float32),
        "w_in": nrm(ks[5], (DEPTH, D_MODEL, D_IN), jnp.float32) * D_MODEL ** -0.5,
        "w_out": nrm(ks[6], (DEPTH, D_MIX, D_MODEL), jnp.float32) * D_MIX ** -0.5,
        "norm_pre": 1.0 + 0.1 * nrm(ks[7], (DEPTH, D_MODEL), jnp.float32),
        "norm_post": 1.0 + 0.1 * nrm(ks[8], (DEPTH, D_MODEL), jnp.float32),
        "attn_sinks": 0.5 * nrm(ks[9], (DEPTH, N_HEADS), jnp.float32),
        "pool_w": nrm(ks[10], (DEPTH, N_POOL_GROUPS, POOL_GROUP_DIM, POOL_GROUP_DIM), jnp.float32) * POOL_GROUP_DIM ** -0.5,
        "pool_scale": 1.0 + 0.1 * nrm(ks[11], (DEPTH, D_POOL), jnp.float32),
        "sgu_ln_g": 1.0 + 0.1 * nrm(ks[12], (DEPTH, D_SGU), jnp.float32),
        "sgu_ln_b": 0.02 * nrm(ks[13], (DEPTH, D_SGU), jnp.float32),
        "sgu_w": nrm(ks[14], (DEPTH, N_SGU_HEADS, CHUNK, CHUNK), jnp.float32) * CHUNK ** -0.5,
        "sgu_b": 1.0 + 0.1 * nrm(ks[15], (DEPTH, N_SGU_HEADS, CHUNK), jnp.float32),
    }


def reference(x_prompt, x_sample, cache_k, cache_v, state_pool, w_in, w_out, norm_pre,
              norm_post, attn_sinks, pool_w, pool_scale, sgu_ln_g, sgu_ln_b, sgu_w, sgu_b):
    pos_p = jnp.arange(SEQ)
    pos_s = PAST_LEN + jnp.arange(DEC_SEQ)
    xp, xs = x_prompt, x_sample
    nk_p, nv_p, npool_p = [], [], []
    nk_s, nv_s, npool_s, nchunk_s = [], [], [], []
    for l in range(DEPTH):
        q, k, v, g_att, xb, g_pool, u, vs, g_sgu = _project(xp, norm_pre[l], w_in[l], pos_p)
        o_att = _window_attn_prompt(q, k, v, attn_sinks[l])
        xb_ext = jnp.pad(xb, ((0, 0), (POOL_HIST, 0), (0, 0)))
        o_pool = _pool_mix(xb_ext, 0, pool_w[l], pool_scale[l])
        o_sgu, _ = _sgu(u, vs, sgu_ln_g[l], sgu_ln_b[l], sgu_w[l], sgu_b[l], CHUNK)
        xp_next = _finish(xp, o_att, g_att, o_pool, g_pool, o_sgu, g_sgu, w_out[l], norm_post[l])
        nk_p.append(k[:, -WINDOW:])
        nv_p.append(v[:, -WINDOW:])
        npool_p.append(xb[:, -POOL_HIST:])
        xp = xp_next

        q, k, v, g_att, xb, g_pool, u, vs, g_sgu = _project(xs, norm_pre[l], w_in[l], pos_s)
        o_att, k_buf, v_buf = _window_attn_sample(q, k, v, cache_k[l], cache_v[l], attn_sinks[l])
        xb_ext = jnp.concatenate([state_pool[l].astype(xb.dtype), xb], axis=1)
        o_pool = _pool_mix(xb_ext, PAST_LEN, pool_w[l], pool_scale[l])
        o_sgu, vn = _sgu(u, vs, sgu_ln_g[l], sgu_ln_b[l], sgu_w[l], sgu_b[l], DEC_SEQ)
        xs = _finish(xs, o_att, g_att, o_pool, g_pool, o_sgu, g_sgu, w_out[l], norm_post[l])
        nk_s.append(k_buf)
        nv_s.append(v_buf)
        npool_s.append(xb_ext[:, -POOL_HIST:])
        nchunk_s.append(vn)

    return (xp, xs,
            jnp.stack(nk_p), jnp.stack(nv_p), jnp.stack(npool_p),
            jnp.stack(nk_s), jnp.stack(nv_s), jnp.stack(npool_s), jnp.stack(nchunk_s))
```

```python
import functools

import jax
import jax.numpy as jnp
import numpy as np
from jax import lax
from jax.experimental import pallas as pl
from jax.experimental.pallas import tpu as pltpu

D_MODEL = 2048
SEQ = 2048
BATCH = 8
DEC_BATCH = 128
DEC_SEQ = 4
PAST_LEN = 8192
HEAD_DIM = 64
HALF_DIM = HEAD_DIM // 2
N_HEADS = 16
N_KV = 4
GROUP = N_HEADS // N_KV
D_ATT = N_HEADS * HEAD_DIM
D_KV = N_KV * HEAD_DIM
WINDOW = 128
ROPE_THETA = 10000.0
D_POOL = 512
POOL_WINDOWS = (2, 4, 8, 16)
POOL_HIST = 15
D_SGU = 512
CHUNK = 128
N_SGU_HEADS = 4
D_IN = 5120
EPS = 1e-6
ATT_SCALE = HEAD_DIM ** -0.5

C_Q = 0
C_GA = C_Q + D_ATT
C_K = C_GA + D_ATT
C_V = C_K + D_KV
C_XB = C_V + D_KV
C_GB = C_XB + D_POOL
C_U = C_GB + D_POOL
C_VS = C_U + D_SGU
C_GC = C_VS + D_SGU
assert C_GC + D_SGU == D_IN

LANES = 128
HIST_PAD = 16
TM = 256
N_CHUNKS = TM // CHUNK
SAMPLE_ROWS = DEC_BATCH * DEC_SEQ
BT = 16
NEW_PAD = 16
S1_TN = 1024
VMEM_LIMIT_PROMPT = 56 * 1024 * 1024
VMEM_LIMIT_SAMPLE = 48 * 1024 * 1024

F32 = jnp.float32
BF16 = jnp.bfloat16
NEG_INF = float("-inf")


def _rms(x, g):
    return x * lax.rsqrt(jnp.mean(x * x, axis=-1, keepdims=True) + EPS) * g


def _silu(x):
    return x * (1.0 / (1.0 + jnp.exp(-x)))


def _layernorm(x, g, b):
    mu = jnp.mean(x, axis=-1, keepdims=True)
    xc = x - mu
    var = jnp.mean(xc * xc, axis=-1, keepdims=True)
    return xc * lax.rsqrt(var + EPS) * g + b


def _rope(x, cos, sin):
    rows = x.shape[0]
    lane = lax.broadcasted_iota(jnp.int32, (rows, LANES), 1)
    first_half = (lane & HALF_DIM) == 0
    outs = []
    for c in range(x.shape[1] // LANES):
        xc = x[:, c * LANES:(c + 1) * LANES]
        partner = jnp.where(first_half,
                            pltpu.roll(xc, LANES - HALF_DIM, 1),
                            pltpu.roll(xc, HALF_DIM, 1))
        outs.append(xc * cos + partner * sin)
    return jnp.concatenate(outs, axis=1)


def _dot(a, b):
    return jnp.dot(a, b, preferred_element_type=F32)


def _dot_nt(a, b):
    return lax.dot_general(a, b, (((1,), (1,)), ((), ())), preferred_element_type=F32)


def _prompt_body(sinks_ref, x_ref, cos_ref, sin_ref, w_in_ref, w_out_ref, npre_ref, npost_ref,
                 pw_ref, ps_ref, lng_ref, lnb_ref, sw_ref, sb_ref,
                 y_ref, nk_ref, nv_ref, np_ref,
                 hbuf, qbuf, gbuf, kbuf, vbuf, xbext, mixbuf):
    j = pl.program_id(1)

    @pl.when(j == 0)
    def _start_of_sequence():
        kbuf[:, 0:WINDOW, :] = jnp.zeros((N_KV, WINDOW, LANES), BF16)
        vbuf[:, 0:WINDOW, :] = jnp.zeros((N_KV, WINDOW, D_KV), BF16)
        xbext[0:HIST_PAD, :] = jnp.zeros((HIST_PAD, D_POOL), F32)

    @pl.when(j > 0)
    def _carry_from_previous_step():
        for g in range(N_KV):
            kbuf[g, 0:WINDOW, :] = kbuf[g, TM:TM + WINDOW, :]
            vbuf[g, 0:WINDOW, :] = vbuf[g, TM:TM + WINDOW, :]
        xbext[0:HIST_PAD, :] = xbext[TM:TM + HIST_PAD, :]

    hbuf[...] = _rms(x_ref[0], npre_ref[...]).astype(BF16)

    def proj(lo, width):
        return _dot(hbuf[...], w_in_ref[:, lo:lo + width])

    cos = cos_ref[...]
    sin = sin_ref[...]

    qbuf[...] = _rope(proj(C_Q, D_ATT), cos * ATT_SCALE, sin * ATT_SCALE).astype(BF16)
    gbuf[...] = _silu(proj(C_GA, D_ATT))
    k = _rope(proj(C_K, D_KV), cos, sin)
    v = proj(C_V, D_KV)
    nk_ref[0] = k[TM - WINDOW:TM]
    nv_ref[0] = v[TM - WINDOW:TM]
    lane_k = lax.broadcasted_iota(jnp.int32, (TM, LANES), 1)
    lane_v = lax.broadcasted_iota(jnp.int32, (TM, D_KV), 1)
    for g in range(N_KV):
        pair, odd = divmod(g, 2)
        kp = k[:, pair * LANES:(pair + 1) * LANES]
        keep = (lane_k >= HEAD_DIM) if odd else (lane_k < HEAD_DIM)
        kbuf[g, WINDOW:WINDOW + TM, :] = jnp.where(keep, kp, 0.0).astype(BF16)
        keep_v = (lane_v >= g * HEAD_DIM) & (lane_v < (g + 1) * HEAD_DIM)
        vbuf[g, WINDOW:WINDOW + TM, :] = jnp.where(keep_v, v, 0.0).astype(BF16)

    qi = lax.broadcasted_iota(jnp.int32, (CHUNK, 2 * WINDOW), 0)
    kj = lax.broadcasted_iota(jnp.int32, (CHUNK, 2 * WINDOW), 1)
    dist = qi + WINDOW - kj
    band_bias = jnp.where((dist >= 0) & (dist < WINDOW), 0.0, NEG_INF)
    no_prev = jnp.where(j == 0, WINDOW, 0)
    first_bias = jnp.where(kj < no_prev, NEG_INF, band_bias)
    for c in range(N_CHUNKS):
        bias = first_bias if c == 0 else band_bias
        r0 = c * CHUNK
        o = None
        for pair in range(2):
            qs = jnp.concatenate(
                [qbuf[r0:r0 + CHUNK, r * D_KV + pair * LANES:r * D_KV + (pair + 1) * LANES]
                 for r in range(GROUP)], axis=0)
            for odd in range(2):
                g = 2 * pair + odd
                s = _dot_nt(qs, kbuf[g, r0:r0 + 2 * WINDOW, :])
                blocks = []
                for r in range(GROUP):
                    sb = s[r * CHUNK:(r + 1) * CHUNK] + bias
                    sink = sinks_ref[GROUP * g + r]
                    m = jnp.maximum(jnp.max(sb, axis=1, keepdims=True), sink)
                    pe = jnp.exp(sb - m)
                    den = jnp.sum(pe, axis=1, keepdims=True) + jnp.exp(sink - m)
                    blocks.append((pe / den).astype(BF16))
                pg = jnp.concatenate(blocks, axis=0)
                og = _dot(pg, vbuf[g, r0:r0 + 2 * WINDOW, :])
                o = og if o is None else o + og
        for r in range(GROUP):
            cols = slice(r * D_KV, (r + 1) * D_KV)
            mixbuf[r0:r0 + CHUNK, cols] = (
                o[r * CHUNK:(r + 1) * CHUNK] * gbuf[r0:r0 + CHUNK, cols]).astype(BF16)

    xbext[HIST_PAD:HIST_PAD + TM, :] = proj(C_XB, D_POOL)
    np_ref[0] = xbext[TM:TM + HIST_PAD, :]
    gate_b = _silu(proj(C_GB, D_POOL))
    pos1 = j * TM + lax.broadcasted_iota(jnp.int32, (TM, LANES), 0) + 1
    for gi, w in enumerate(POOL_WINDOWS):
        cols = slice(gi * LANES, (gi + 1) * LANES)
        cur = xbext[HIST_PAD:HIST_PAD + TM, cols]
        acc = cur
        for i in range(1, w):
            acc = acc + xbext[HIST_PAD - i:HIST_PAD - i + TM, cols]
        cnt = jnp.minimum(w, pos1).astype(F32)
        pooled = acc / cnt - cur
        mixed = _dot(pooled.astype(BF16), pw_ref[gi]) * ps_ref[:, cols]
        mixbuf[:, D_ATT + gi * LANES:D_ATT + (gi + 1) * LANES] = (mixed * gate_b[:, cols]).astype(BF16)

    vn = _layernorm(proj(C_VS, D_SGU), lng_ref[...], lnb_ref[...]).astype(BF16)
    u = proj(C_U, D_SGU)
    gate_c = _silu(proj(C_GC, D_SGU))
    ti = lax.broadcasted_iota(jnp.int32, (CHUNK, CHUNK), 0)
    si = lax.broadcasted_iota(jnp.int32, (CHUNK, CHUNK), 1)
    for hh in range(N_SGU_HEADS):
        cols = slice(hh * LANES, (hh + 1) * LANES)
        wm = jnp.where(ti >= si, sw_ref[hh], 0.0).astype(BF16)
        for c in range(N_CHUNKS):
            rows = slice(c * CHUNK, (c + 1) * CHUNK)
            mixed = _dot(wm, vn[rows, cols]) + sb_ref[hh]
            mixbuf[rows, D_ATT + D_POOL + hh * LANES:D_ATT + D_POOL + (hh + 1) * LANES] = (
                u[rows, cols] * mixed * gate_c[rows, cols]).astype(BF16)

    y = _dot(mixbuf[...], w_out_ref[...])
    y_ref[0] = x_ref[0] + _rms(y, npost_ref[...])


def _const_spec(shape):
    nd = len(shape)
    return pl.BlockSpec(shape, lambda b, j: (0,) * nd, pipeline_mode=pl.Buffered(1))


def _prompt_layer(x, cos, sin, w_in_p, w_out_p, npre, npost, sinks, pw, ps, lng, lnb, sw, sb_full):
    grid = (BATCH, SEQ // TM)
    in_specs = [
        pl.BlockSpec(memory_space=pltpu.SMEM),
        pl.BlockSpec((1, TM, D_MODEL), lambda b, j: (b, j, 0)),
        pl.BlockSpec((TM, LANES), lambda b, j: (j, 0)),
        pl.BlockSpec((TM, LANES), lambda b, j: (j, 0)),
        _const_spec((D_MODEL, D_IN)),
        _const_spec((D_MODEL, D_MODEL)),
        _const_spec((1, D_MODEL)),
        _const_spec((1, D_MODEL)),
        _const_spec((len(POOL_WINDOWS), LANES, LANES)),
        _const_spec((1, D_POOL)),
        _const_spec((1, D_SGU)),
        _const_spec((1, D_SGU)),
        _const_spec((N_SGU_HEADS, CHUNK, CHUNK)),
        _const_spec((N_SGU_HEADS, CHUNK, LANES)),
    ]
    out_specs = [
        pl.BlockSpec((1, TM, D_MODEL), lambda b, j: (b, j, 0)),
        pl.BlockSpec((1, WINDOW, D_KV), lambda b, j: (b, 0, 0)),
        pl.BlockSpec((1, WINDOW, D_KV), lambda b, j: (b, 0, 0)),
        pl.BlockSpec((1, HIST_PAD, D_POOL), lambda b, j: (b, 0, 0)),
    ]
    out_shape = [
        jax.ShapeDtypeStruct((BATCH, SEQ, D_MODEL), F32),
        jax.ShapeDtypeStruct((BATCH, WINDOW, D_KV), F32),
        jax.ShapeDtypeStruct((BATCH, WINDOW, D_KV), F32),
        jax.ShapeDtypeStruct((BATCH, HIST_PAD, D_POOL), F32),
    ]
    scratch = [
        pltpu.VMEM((TM, D_MODEL), BF16),
        pltpu.VMEM((TM, D_ATT), BF16),
        pltpu.VMEM((TM, D_ATT), F32),
        pltpu.VMEM((N_KV, WINDOW + TM, LANES), BF16),
        pltpu.VMEM((N_KV, WINDOW + TM, D_KV), BF16),
        pltpu.VMEM((HIST_PAD + TM, D_POOL), F32),
        pltpu.VMEM((TM, D_MODEL), BF16),
    ]
    return pl.pallas_call(
        _prompt_body,
        grid=grid,
        in_specs=in_specs,
        out_specs=out_specs,
        out_shape=out_shape,
        scratch_shapes=scratch,
        compiler_params=pltpu.CompilerParams(
            dimension_semantics=("arbitrary", "arbitrary"),
            vmem_limit_bytes=VMEM_LIMIT_PROMPT),
        name="prompt_layer",
    )(sinks, x, cos, sin, w_in_p, w_out_p, npre, npost, pw, ps, lng, lnb, sw, sb_full)


def _sample_proj_body(x_ref, npre_ref, w_ref, z_ref, hbuf):
    @pl.when(pl.program_id(0) == 0)
    def _norm_once():
        hbuf[...] = _rms(x_ref[...], npre_ref[...]).astype(BF16)

    z_ref[...] = _dot(hbuf[...], w_ref[...])


def _sample_proj(xs, npre, w_in_p):
    return pl.pallas_call(
        _sample_proj_body,
        grid=(D_IN // S1_TN,),
        in_specs=[
            pl.BlockSpec((SAMPLE_ROWS, D_MODEL), lambda n: (0, 0)),
            pl.BlockSpec((1, D_MODEL), lambda n: (0, 0)),
            pl.BlockSpec((D_MODEL, S1_TN), lambda n: (0, n)),
        ],
        out_specs=pl.BlockSpec((SAMPLE_ROWS, S1_TN), lambda n: (0, n)),
        out_shape=jax.ShapeDtypeStruct((SAMPLE_ROWS, D_IN), F32),
        scratch_shapes=[pltpu.VMEM((SAMPLE_ROWS, D_MODEL), BF16)],
        compiler_params=pltpu.CompilerParams(
            dimension_semantics=("arbitrary",), vmem_limit_bytes=VMEM_LIMIT_SAMPLE),
        name="sample_proj",
    )(xs, npre, w_in_p)


QROWS = DEC_SEQ * GROUP
SROWS = N_KV * QROWS
R_XB, R_GB, R_U, R_VS, R_GC = 0, D_POOL, 2 * D_POOL, 2 * D_POOL + D_SGU, 2 * D_POOL + 2 * D_SGU
D_REST = 2 * D_POOL + 3 * D_SGU


def _sample_mix_body(qz_ref, kz_ref, vz_ref, cq_ref, sq_ref, ckn_ref, skn_ref, sink_ref,
                     ck_ref, cv_ref, z_ref, st_ref, pw_ref, ps_ref, lng_ref, lnb_ref,
                     wexp_ref, bexp_ref,
                     oatt_ref, mixr_ref, nk_ref, nv_ref, npool_ref, vn_ref,
                     qbig, knew_f, vnew_f, knew_b, vnew_b):
    q = _rope(qz_ref[...], cq_ref[...], sq_ref[...])
    lane_q = lax.broadcasted_iota(jnp.int32, (BT * QROWS, D_KV), 1)
    for g in range(N_KV):
        keep = (lane_q >= g * HEAD_DIM) & (lane_q < (g + 1) * HEAD_DIM)
        qbig[:, g * QROWS:(g + 1) * QROWS, :] = (
            jnp.where(keep, q, 0.0).astype(BF16).reshape(BT, QROWS, D_KV))
    kn = _rope(kz_ref[...], ckn_ref[...], skn_ref[...])
    knew_f[...] = kn.reshape(BT, NEW_PAD, D_KV)
    knew_b[...] = kn.astype(BF16).reshape(BT, NEW_PAD, D_KV)
    vnew = vz_ref[...]
    vnew_f[...] = vnew.reshape(BT, NEW_PAD, D_KV)
    vnew_b[...] = vnew.astype(BF16).reshape(BT, NEW_PAD, D_KV)

    row_t_h = (lax.broadcasted_iota(jnp.int32, (SROWS, WINDOW), 0) >> 2) & (DEC_SEQ - 1)
    col_h = lax.broadcasted_iota(jnp.int32, (SROWS, WINDOW), 1)
    bias_h = jnp.where(col_h > row_t_h, 0.0, NEG_INF)
    row_t_n = (lax.broadcasted_iota(jnp.int32, (SROWS, NEW_PAD), 0) >> 2) & (DEC_SEQ - 1)
    col_n = lax.broadcasted_iota(jnp.int32, (SROWS, NEW_PAD), 1)
    bias_n = jnp.where(col_n <= row_t_n, 0.0, NEG_INF)
    sink = sink_ref[:, 0:1]
    lane_o = lax.broadcasted_iota(jnp.int32, (QROWS, D_KV), 1)
    row8 = lax.broadcasted_iota(jnp.int32, (8, D_KV), 0)
    shift = WINDOW - DEC_SEQ

    def roll_in(hist, new8):
        rolled = pltpu.roll(hist, shift, 0)
        tail = jnp.where(row8 >= 8 - DEC_SEQ, pltpu.roll(new8, 8 - DEC_SEQ, 0), rolled[WINDOW - 8:])
        return jnp.concatenate([rolled[:WINDOW - 8], tail], axis=0)

    def per_batch(b, carry):
        kh = ck_ref[b]
        vh = cv_ref[b]
        qb = qbig[b]
        s_h = _dot_nt(qb, kh.astype(BF16)) + bias_h
        s_n = _dot_nt(qb, knew_b[b]) + bias_n
        m = jnp.maximum(jnp.maximum(jnp.max(s_h, axis=1, keepdims=True),
                                    jnp.max(s_n, axis=1, keepdims=True)), sink)
        p_h = jnp.exp(s_h - m)
        p_n = jnp.exp(s_n - m)
        den = (jnp.sum(p_h, axis=1, keepdims=True) + jnp.sum(p_n, axis=1, keepdims=True)
               + jnp.exp(sink - m))
        inv = 1.0 / den
        o = (_dot((p_h * inv).astype(BF16), vh.astype(BF16))
             + _dot((p_n * inv).astype(BF16), vnew_b[b]))
        osel = o[0:QROWS]
        for g in range(1, N_KV):
            osel = jnp.where(lane_o >= g * HEAD_DIM, o[g * QROWS:(g + 1) * QROWS], osel)
        oatt_ref[pl.ds(pl.multiple_of(b * QROWS, QROWS), QROWS), :] = osel
        nk_ref[b] = roll_in(kh, knew_f[b, 0:8, :])
        nv_ref[b] = roll_in(vh, vnew_f[b, 0:8, :])
        return carry

    lax.fori_loop(0, BT, per_batch, 0)

    def ext(i):
        if i < POOL_HIST:
            return st_ref[i]
        return z_ref[i - POOL_HIST, :, R_XB:R_XB + D_POOL]

    for s in range(POOL_HIST):
        npool_ref[s] = ext(s + DEC_SEQ)
    for gi, w in enumerate(POOL_WINDOWS):
        cols = slice(gi * LANES, (gi + 1) * LANES)
        pooled = []
        for t in range(DEC_SEQ):
            acc = ext(POOL_HIST + t)[:, cols]
            cur = acc
            for i in range(1, w):
                acc = acc + ext(POOL_HIST + t - i)[:, cols]
            cnt = float(min(w, PAST_LEN + t + 1))
            pooled.append(acc / cnt - cur)
        pooled = jnp.concatenate(pooled, axis=0).astype(BF16)
        mixed = _dot(pooled, pw_ref[gi]) * ps_ref[:, cols]
        for t in range(DEC_SEQ):
            gate = _silu(z_ref[t, :, R_GB + gi * LANES:R_GB + (gi + 1) * LANES])
            mixr_ref[t, :, gi * LANES:(gi + 1) * LANES] = (
                mixed[t * BT:(t + 1) * BT] * gate).astype(BF16)

    vns = []
    for t in range(DEC_SEQ):
        vn_t = _layernorm(z_ref[t, :, R_VS:R_VS + D_SGU], lng_ref[...], lnb_ref[...])
        vn_ref[t] = vn_t
        vns.append(vn_t)
    for t in range(DEC_SEQ):
        mixed = bexp_ref[t:t + 1, :]
        for s in range(t + 1):
            mixed = mixed + wexp_ref[DEC_SEQ * t + s:DEC_SEQ * t + s + 1, :] * vns[s]
        out = z_ref[t, :, R_U:R_U + D_SGU] * mixed * _silu(z_ref[t, :, R_GC:R_GC + D_SGU])
        mixr_ref[t, :, D_POOL:D_POOL + D_SGU] = out.astype(BF16)


def _sample_mix(qz, kz, vz, cq, sq, ckn, skn, sink_rows, cache_k, cache_v, z3, st_t,
                pw, ps, lng, lnb, wexp, bexp):
    nb = DEC_BATCH // BT
    const2 = lambda shape: pl.BlockSpec(shape, lambda i: (0,) * len(shape))
    in_specs = [
        pl.BlockSpec((BT * QROWS, D_KV), lambda i: (i, 0)),
        pl.BlockSpec((BT * NEW_PAD, D_KV), lambda i: (i, 0)),
        pl.BlockSpec((BT * NEW_PAD, D_KV), lambda i: (i, 0)),
        const2((BT * QROWS, LANES)), const2((BT * QROWS, LANES)),
        const2((BT * NEW_PAD, LANES)), const2((BT * NEW_PAD, LANES)),
        const2((SROWS, LANES)),
        pl.BlockSpec((BT, WINDOW, D_KV), lambda i: (i, 0, 0)),
        pl.BlockSpec((BT, WINDOW, D_KV), lambda i: (i, 0, 0)),
        pl.BlockSpec((DEC_SEQ, BT, D_REST), lambda i: (0, i, 1)),
        pl.BlockSpec((POOL_HIST, BT, D_POOL), lambda i: (0, i, 0)),
        const2((len(POOL_WINDOWS), LANES, LANES)),
        const2((1, D_POOL)), const2((1, D_SGU)), const2((1, D_SGU)),
        const2((DEC_SEQ * DEC_SEQ, D_SGU)), const2((DEC_SEQ, D_SGU)),
    ]
    out_specs = [
        pl.BlockSpec((BT * QROWS, D_KV), lambda i: (i, 0)),
        pl.BlockSpec((DEC_SEQ, BT, D_POOL + D_SGU), lambda i: (0, i, 0)),
        pl.BlockSpec((BT, WINDOW, D_KV), lambda i: (i, 0, 0)),
        pl.BlockSpec((BT, WINDOW, D_KV), lambda i: (i, 0, 0)),
        pl.BlockSpec((POOL_HIST, BT, D_POOL), lambda i: (0, i, 0)),
        pl.BlockSpec((DEC_SEQ, BT, D_SGU), lambda i: (0, i, 0)),
    ]
    out_shape = [
        jax.ShapeDtypeStruct((DEC_BATCH * QROWS, D_KV), F32),
        jax.ShapeDtypeStruct((DEC_SEQ, DEC_BATCH, D_POOL + D_SGU), BF16),
        jax.ShapeDtypeStruct((DEC_BATCH, WINDOW, D_KV), F32),
        jax.ShapeDtypeStruct((DEC_BATCH, WINDOW, D_KV), F32),
        jax.ShapeDtypeStruct((POOL_HIST, DEC_BATCH, D_POOL), F32),
        jax.ShapeDtypeStruct((DEC_SEQ, DEC_BATCH, D_SGU), F32),
    ]
    scratch = [
        pltpu.VMEM((BT, SROWS, D_KV), BF16),
        pltpu.VMEM((BT, NEW_PAD, D_KV), F32),
        pltpu.VMEM((BT, NEW_PAD, D_KV), F32),
        pltpu.VMEM((BT, NEW_PAD, D_KV), BF16),
        pltpu.VMEM((BT, NEW_PAD, D_KV), BF16),
    ]
    return pl.pallas_call(
        _sample_mix_body,
        grid=(nb,),
        in_specs=in_specs,
        out_specs=out_specs,
        out_shape=out_shape,
        scratch_shapes=scratch,
        compiler_params=pltpu.CompilerParams(
            dimension_semantics=("arbitrary",), vmem_limit_bytes=VMEM_LIMIT_SAMPLE),
        name="sample_mix",
    )(qz, kz, vz, cq, sq, ckn, skn, sink_rows, cache_k, cache_v, z3, st_t,
      pw, ps, lng, lnb, wexp, bexp)


def _sample_out_body(x_ref, oatt_ref, zg_ref, mixr_ref, w_out_ref, npost_ref, y_ref):
    a = (oatt_ref[...] * _silu(zg_ref[...])).astype(BF16)
    y = _dot(a, w_out_ref[0:D_ATT, :]) + _dot(mixr_ref[...], w_out_ref[D_ATT:D_MODEL, :])
    y_ref[...] = x_ref[...] + _rms(y, npost_ref[...])


def _sample_out(xs, oatt, z, mixr, w_out_p, npost):
    full = lambda shape: pl.BlockSpec(shape, lambda i: (0,) * len(shape))
    return pl.pallas_call(
        _sample_out_body,
        grid=(1,),
        in_specs=[
            full((SAMPLE_ROWS, D_MODEL)),
            full((SAMPLE_ROWS, D_ATT)),
            pl.BlockSpec((SAMPLE_ROWS, D_ATT), lambda i: (0, C_GA // D_ATT)),
            full((SAMPLE_ROWS, D_POOL + D_SGU)),
            full((D_MODEL, D_MODEL)),
            full((1, D_MODEL)),
        ],
        out_specs=full((SAMPLE_ROWS, D_MODEL)),
        out_shape=jax.ShapeDtypeStruct((SAMPLE_ROWS, D_MODEL), F32),
        compiler_params=pltpu.CompilerParams(
            dimension_semantics=("arbitrary",), vmem_limit_bytes=VMEM_LIMIT_SAMPLE),
        name="sample_out",
    )(xs, oatt, z, mixr, w_out_p, npost)


def _heads_r_major(w):
    lead = w.shape[:-1]
    return w.reshape(*lead, N_KV, GROUP, HEAD_DIM).swapaxes(-3, -2).reshape(*lead, D_ATT)


def _prep_w_in(w):
    w = w.astype(BF16)
    q, k, v, ga, rest = (w[:, :D_ATT], w[:, D_ATT:D_ATT + D_KV], w[:, D_ATT + D_KV:D_ATT + 2 * D_KV],
                         w[:, D_ATT + 2 * D_KV:2 * D_ATT + 2 * D_KV], w[:, 2 * D_ATT + 2 * D_KV:])
    return jnp.concatenate([_heads_r_major(q), _heads_r_major(ga), k, v, rest], axis=1)


def _prep_w_out(w):
    w = w.astype(BF16)
    att = w[:D_ATT].reshape(N_KV, GROUP, HEAD_DIM, D_MODEL).swapaxes(0, 1).reshape(D_ATT, D_MODEL)
    return jnp.concatenate([att, w[D_ATT:]], axis=0)


def _rope_tables(pos):
    inv = ROPE_THETA ** (-jnp.arange(0, HEAD_DIM, 2, dtype=F32) / HEAD_DIM)
    ang = pos.astype(F32)[:, None] * inv[None, :]
    c, s = jnp.cos(ang), jnp.sin(ang)
    return jnp.concatenate([c, c, c, c], axis=1), jnp.concatenate([-s, s, -s, s], axis=1)


def kernel(x_prompt, x_sample, cache_k, cache_v, state_pool, w_in, w_out, norm_pre, norm_post,
           attn_sinks, pool_w, pool_scale, sgu_ln_g, sgu_ln_b, sgu_w, sgu_b):
    depth = w_in.shape[0]
    cos_p, sin_p = _rope_tables(jnp.arange(SEQ))
    cos_s, sin_s = _rope_tables(PAST_LEN + jnp.arange(DEC_SEQ))
    cq = jnp.tile(jnp.repeat(cos_s, GROUP, axis=0), (BT, 1)) * ATT_SCALE
    sq = jnp.tile(jnp.repeat(sin_s, GROUP, axis=0), (BT, 1)) * ATT_SCALE
    pad_rows = ((0, NEW_PAD - DEC_SEQ), (0, 0))
    ckn = jnp.tile(jnp.pad(cos_s, pad_rows), (BT, 1))
    skn = jnp.tile(jnp.pad(sin_s, pad_rows), (BT, 1))

    xp = x_prompt
    xs = x_sample.transpose(1, 0, 2).reshape(SAMPLE_ROWS, D_MODEL)
    ck = cache_k.reshape(depth, DEC_BATCH, WINDOW, D_KV)
    cv = cache_v.reshape(depth, DEC_BATCH, WINDOW, D_KV)
    outs = {n: [] for n in ("nk_p", "nv_p", "np_p", "nk_s", "nv_s", "np_s", "vn_s")}
    for l in range(depth):
        w_in_p = _prep_w_in(w_in[l])
        w_out_p = _prep_w_out(w_out[l])
        npre = norm_pre[l][None, :]
        npost = norm_post[l][None, :]
        pw = pool_w[l].astype(BF16)
        ps = pool_scale[l][None, :]
        lng = sgu_ln_g[l][None, :]
        lnb = sgu_ln_b[l][None, :]

        sb_full = jnp.broadcast_to(sgu_b[l][:, :, None], (N_SGU_HEADS, CHUNK, LANES))
        xp, nk, nv, npool = _prompt_layer(xp, cos_p, sin_p, w_in_p, w_out_p, npre, npost,
                                          attn_sinks[l], pw, ps, lng, lnb, sgu_w[l], sb_full)
        outs["nk_p"].append(nk.reshape(BATCH, WINDOW, N_KV, HEAD_DIM))
        outs["nv_p"].append(nv.reshape(BATCH, WINDOW, N_KV, HEAD_DIM))
        outs["np_p"].append(npool[:, HIST_PAD - POOL_HIST:])

        z = _sample_proj(xs, npre, w_in_p)
        qz = (z[:, C_Q:C_Q + D_ATT].reshape(DEC_SEQ, DEC_BATCH, GROUP, D_KV)
              .transpose(1, 0, 2, 3).reshape(DEC_BATCH * QROWS, D_KV))
        kv = z[:, C_K:C_K + 2 * D_KV].reshape(DEC_SEQ, DEC_BATCH, 2 * D_KV).transpose(1, 0, 2)
        kv = jnp.pad(kv, ((0, 0), (0, NEW_PAD - DEC_SEQ), (0, 0)))
        kz = kv[:, :, :D_KV].reshape(DEC_BATCH * NEW_PAD, D_KV)
        vz = kv[:, :, D_KV:].reshape(DEC_BATCH * NEW_PAD, D_KV)
        sink_rows = jnp.broadcast_to(
            attn_sinks[l].reshape(N_KV, 1, GROUP, 1), (N_KV, DEC_SEQ, GROUP, LANES)
        ).reshape(SROWS, LANES)
        st_t = state_pool[l].transpose(1, 0, 2)
        wexp = jnp.repeat(sgu_w[l][:, :DEC_SEQ, :DEC_SEQ].transpose(1, 2, 0),
                          LANES, axis=-1).reshape(DEC_SEQ * DEC_SEQ, D_SGU)
        bexp = jnp.repeat(sgu_b[l][:, :DEC_SEQ].T, LANES, axis=-1)
        oatt, mixr, nk_s, nv_s, npool_t, vn = _sample_mix(
            qz, kz, vz, cq, sq, ckn, skn, sink_rows, ck[l], cv[l],
            z.reshape(DEC_SEQ, DEC_BATCH, D_IN), st_t, pw, ps, lng, lnb, wexp, bexp)
        oatt_t = (oatt.reshape(DEC_BATCH, DEC_SEQ, GROUP, D_KV).transpose(1, 0, 2, 3)
                  .reshape(SAMPLE_ROWS, D_ATT))
        xs = _sample_out(xs, oatt_t, z, mixr.reshape(SAMPLE_ROWS, D_POOL + D_SGU), w_out_p, npost)
        outs["nk_s"].append(nk_s.reshape(DEC_BATCH, WINDOW, N_KV, HEAD_DIM))
        outs["nv_s"].append(nv_s.reshape(DEC_BATCH, WINDOW, N_KV, HEAD_DIM))
        outs["np_s"].append(npool_t.transpose(1, 0, 2))
        outs["vn_s"].append(vn.transpose(1, 0, 2))

    y_sample = xs.reshape(DEC_SEQ, DEC_BATCH, D_MODEL).transpose(1, 0, 2)
    return (xp, y_sample,
            jnp.stack(outs["nk_p"]), jnp.stack(outs["nv_p"]), jnp.stack(outs["np_p"]),
            jnp.stack(outs["nk_s"]), jnp.stack(outs["nv_s"]), jnp.stack(outs["np_s"]),
            jnp.stack(outs["vn_s"]))
```

```python
import functools

import jax
import jax.numpy as jnp
from jax import lax
from jax.experimental import pallas as pl
from jax.experimental.pallas import tpu as pltpu

D_MODEL = 2048
SEQ = 2048
BATCH = 8
DEPTH = 2
DEC_BATCH = 128
DEC_SEQ = 4
PAST_LEN = 8192
HEAD_DIM = 64
HALF_DIM = HEAD_DIM // 2
N_HEADS = 16
N_KV = 4
GROUP = N_HEADS // N_KV
D_ATT = N_HEADS * HEAD_DIM
D_KV = N_KV * HEAD_DIM
WINDOW = 128
ROPE_THETA = 10000.0
D_POOL = 512
POOL_WINDOWS = (2, 4, 8, 16)
POOL_HIST = 15
D_SGU = 512
CHUNK = 128
N_SGU_HEADS = 4
D_IN = 5120
EPS = 1e-6
ATT_SCALE = HEAD_DIM ** -0.5

C_Q = 0
C_K = C_Q + D_ATT
C_V = C_K + D_KV
C_GA = C_V + D_KV
C_XB = C_GA + D_ATT
C_GB = C_XB + D_POOL
C_U = C_GB + D_POOL
C_VS = C_U + D_SGU
C_GC = C_VS + D_SGU
assert C_GC + D_SGU == D_IN

LANES = 128
SUBLANES = 8
HIST_PAD = 16
TM = 256
N_CHUNKS = TM // CHUNK
SAMPLE_ROWS = DEC_BATCH * DEC_SEQ
BT = 16
RT = BT * DEC_SEQ
GB = 4
GR = GB * DEC_SEQ
S1_TN = 1024
VMEM_LIMIT_PROMPT = 56 * 1024 * 1024
VMEM_LIMIT_SAMPLE = 48 * 1024 * 1024

F32 = jnp.float32
BF16 = jnp.bfloat16
NEG_INF = float("-inf")


def _rms(x, g):
    return x * lax.rsqrt(jnp.mean(x * x, axis=-1, keepdims=True) + EPS) * g


def _silu(x):
    return x * (1.0 / (1.0 + jnp.exp(-x)))


def _layernorm(x, g, b):
    mu = jnp.mean(x, axis=-1, keepdims=True)
    xc = x - mu
    var = jnp.mean(xc * xc, axis=-1, keepdims=True)
    return xc * lax.rsqrt(var + EPS) * g + b


def _swap_heads(x):
    return pltpu.roll(x, HEAD_DIM, 1)


def _rope(x, cos, sin):
    rows = x.shape[0]
    lane = lax.broadcasted_iota(jnp.int32, (rows, LANES), 1)
    first_half = (lane & HALF_DIM) == 0
    outs = []
    for c in range(x.shape[1] // LANES):
        xc = x[:, c * LANES:(c + 1) * LANES]
        partner = jnp.where(first_half,
                            pltpu.roll(xc, LANES - HALF_DIM, 1),
                            pltpu.roll(xc, HALF_DIM, 1))
        outs.append(xc * cos + partner * sin)
    return jnp.concatenate(outs, axis=1)


def _dot(a, b):
    return jnp.dot(a, b, preferred_element_type=F32)


def _dot_nt(a, b):
    return lax.dot_general(a, b, (((1,), (1,)), ((), ())), preferred_element_type=F32)


def _prompt_body(layer, n_alias, sinks_ref, x_ref, cos_ref, sin_ref, w_in_ref, w_out_ref,
                 npre_ref, npost_ref, pw_ref, ps_ref, lng_ref, lnb_ref, sw_ref, sb_ref, *rest):
    y_ref, nk_ref, nv_ref, np_ref, hbuf, qbuf, gbuf, kbuf, vbuf, xbext, mixbuf = rest[n_alias:]
    j = pl.program_id(1)

    @pl.when(j == 0)
    def _start_of_sequence():
        kbuf[:, 0:WINDOW, :] = jnp.zeros((2 * N_KV, WINDOW, LANES), BF16)
        vbuf[:, 0:WINDOW, :] = jnp.zeros((N_KV, WINDOW, D_KV), BF16)
        xbext[0:HIST_PAD, :] = jnp.zeros((HIST_PAD, D_POOL), F32)

    @pl.when(j > 0)
    def _carry_from_previous_step():
        for i in range(2 * N_KV):
            kbuf[i, 0:WINDOW, :] = kbuf[i, TM:TM + WINDOW, :]
        for g in range(N_KV):
            vbuf[g, 0:WINDOW, :] = vbuf[g, TM:TM + WINDOW, :]
        xbext[0:HIST_PAD, :] = xbext[TM:TM + HIST_PAD, :]

    hbuf[...] = _rms(x_ref[0], npre_ref[...]).astype(BF16)

    def proj(lo, width):
        return _dot(hbuf[...], w_in_ref[:, lo:lo + width])

    cos = cos_ref[...]
    sin = sin_ref[...]

    qbuf[...] = _rope(proj(C_Q, D_ATT), cos * ATT_SCALE, sin * ATT_SCALE).astype(BF16)
    gbuf[...] = _silu(proj(C_GA, D_ATT))
    k = _rope(proj(C_K, D_KV), cos, sin)
    v = proj(C_V, D_KV)
    nk_ref[0] = k[TM - WINDOW:TM]
    nv_ref[0] = v[TM - WINDOW:TM]
    lane_k = lax.broadcasted_iota(jnp.int32, (TM, LANES), 1)
    low = lane_k < HEAD_DIM
    lane_v = lax.broadcasted_iota(jnp.int32, (TM, D_KV), 1)
    for pair in range(2):
        kp = k[:, pair * LANES:(pair + 1) * LANES]
        ks = _swap_heads(kp)
        rows = slice(WINDOW, WINDOW + TM)
        kbuf[4 * pair + 0, rows, :] = jnp.where(low, kp, 0.0).astype(BF16)
        kbuf[4 * pair + 1, rows, :] = jnp.where(low, 0.0, ks).astype(BF16)
        kbuf[4 * pair + 2, rows, :] = jnp.where(low, ks, 0.0).astype(BF16)
        kbuf[4 * pair + 3, rows, :] = jnp.where(low, 0.0, kp).astype(BF16)
    for g in range(N_KV):
        keep_v = (lane_v >= g * HEAD_DIM) & (lane_v < (g + 1) * HEAD_DIM)
        vbuf[g, WINDOW:WINDOW + TM, :] = jnp.where(keep_v, v, 0.0).astype(BF16)

    qi = lax.broadcasted_iota(jnp.int32, (CHUNK, 2 * WINDOW), 0)
    kj = lax.broadcasted_iota(jnp.int32, (CHUNK, 2 * WINDOW), 1)
    dist = qi + WINDOW - kj
    band_bias = jnp.where((dist >= 0) & (dist < WINDOW), 0.0, NEG_INF)
    no_prev = jnp.where(j == 0, WINDOW, 0)
    first_bias = jnp.where(kj < no_prev, NEG_INF, band_bias)
    low_c = lax.broadcasted_iota(jnp.int32, (CHUNK, LANES), 1) < HEAD_DIM
    for c in range(N_CHUNKS):
        bias = first_bias if c == 0 else band_bias
        r0 = c * CHUNK
        o = None
        for g in range(N_KV):
            qs = jnp.concatenate(
                [qbuf[r0:r0 + CHUNK, (2 * g + i) * LANES:(2 * g + i + 1) * LANES] for i in range(2)],
                axis=0)
            s_half = [_dot_nt(qs, kbuf[2 * g + half, r0:r0 + 2 * WINDOW, :]) for half in range(2)]
            blocks = []
            for r in range(GROUP):
                sb = s_half[r % 2][(r // 2) * CHUNK:(r // 2 + 1) * CHUNK] + bias
                sink = sinks_ref[layer, GROUP * g + r]
                m = jnp.maximum(jnp.max(sb, axis=1, keepdims=True), sink)
                pe = jnp.exp(sb - m)
                den = jnp.sum(pe, axis=1, keepdims=True) + jnp.exp(sink - m)
                blocks.append((pe / den).astype(BF16))
            pg = jnp.concatenate(blocks, axis=0)
            og = _dot(pg, vbuf[g, r0:r0 + 2 * WINDOW, :])
            o = og if o is None else o + og
        for pair in range(2):
            for i in range(2):
                a = o[(2 * i) * CHUNK:(2 * i + 1) * CHUNK, pair * LANES:(pair + 1) * LANES]
                b = o[(2 * i + 1) * CHUNK:(2 * i + 2) * CHUNK, pair * LANES:(pair + 1) * LANES]
                for odd, blk in ((0, jnp.where(low_c, a, _swap_heads(b))),
                                 (1, jnp.where(low_c, _swap_heads(a), b))):
                    cols = slice((2 * (2 * pair + odd) + i) * LANES, (2 * (2 * pair + odd) + i + 1) * LANES)
                    mixbuf[r0:r0 + CHUNK, cols] = (blk * gbuf[r0:r0 + CHUNK, cols]).astype(BF16)

    xbext[HIST_PAD:HIST_PAD + TM, :] = proj(C_XB, D_POOL)
    np_ref[0] = xbext[HIST_PAD + TM - POOL_HIST:HIST_PAD + TM, :]
    gate_b = _silu(proj(C_GB, D_POOL))
    pos1 = j * TM + lax.broadcasted_iota(jnp.int32, (TM, LANES), 0) + 1
    for gi, w in enumerate(POOL_WINDOWS):
        cols = slice(gi * LANES, (gi + 1) * LANES)
        cur = xbext[HIST_PAD:HIST_PAD + TM, cols]
        acc = cur
        for i in range(1, w):
            acc = acc + xbext[HIST_PAD - i:HIST_PAD - i + TM, cols]
        cnt = jnp.minimum(w, pos1).astype(F32)
        pooled = acc / cnt - cur
        mixed = _dot(pooled.astype(BF16), pw_ref[gi]) * ps_ref[:, cols]
        mixbuf[:, D_ATT + gi * LANES:D_ATT + (gi + 1) * LANES] = (mixed * gate_b[:, cols]).astype(BF16)

    vn = _layernorm(proj(C_VS, D_SGU), lng_ref[...], lnb_ref[...]).astype(BF16)
    u = proj(C_U, D_SGU)
    gate_c = _silu(proj(C_GC, D_SGU))
    ti = lax.broadcasted_iota(jnp.int32, (CHUNK, CHUNK), 0)
    si = lax.broadcasted_iota(jnp.int32, (CHUNK, CHUNK), 1)
    for hh in range(N_SGU_HEADS):
        cols = slice(hh * LANES, (hh + 1) * LANES)
        wm = jnp.where(ti >= si, sw_ref[hh], 0.0).astype(BF16)
        for c in range(N_CHUNKS):
            rows = slice(c * CHUNK, (c + 1) * CHUNK)
            mixed = _dot(wm, vn[rows, cols]) + sb_ref[hh]
            mixbuf[rows, D_ATT + D_POOL + hh * LANES:D_ATT + D_POOL + (hh + 1) * LANES] = (
                u[rows, cols] * mixed * gate_c[rows, cols]).astype(BF16)

    y = _dot(mixbuf[...], w_out_ref[...])
    y_ref[0] = x_ref[0] + _rms(y, npost_ref[...])


def _layer_spec(shape, layer, grid_rank):
    zeros = (0,) * len(shape)
    if grid_rank == 2:
        index_map = lambda b, j: (layer,) + zeros
    else:
        index_map = lambda i: (layer,) + zeros
    return pl.BlockSpec((None,) + tuple(shape), index_map, pipeline_mode=pl.Buffered(1))


def _prompt_layer(layer, x, cos, sin, params, prev_outs):
    (sinks, w_in_b, w_out_b, npre, npost, pw_b, ps, lng, lnb, sw, sb_full) = params
    grid = (BATCH, SEQ // TM)
    n_alias = len(prev_outs)
    in_specs = [
        pl.BlockSpec(memory_space=pltpu.SMEM),
        pl.BlockSpec((1, TM, D_MODEL), lambda b, j: (b, j, 0)),
        pl.BlockSpec((TM, LANES), lambda b, j: (j, 0)),
        pl.BlockSpec((TM, LANES), lambda b, j: (j, 0)),
        _layer_spec((D_MODEL, D_IN), layer, 2),
        _layer_spec((D_MODEL, D_MODEL), layer, 2),
        _layer_spec((1, D_MODEL), layer, 2),
        _layer_spec((1, D_MODEL), layer, 2),
        _layer_spec((len(POOL_WINDOWS), LANES, LANES), layer, 2),
        _layer_spec((1, D_POOL), layer, 2),
        _layer_spec((1, D_SGU), layer, 2),
        _layer_spec((1, D_SGU), layer, 2),
        _layer_spec((N_SGU_HEADS, CHUNK, CHUNK), layer, 2),
        _layer_spec((N_SGU_HEADS, CHUNK, LANES), layer, 2),
    ] + [pl.BlockSpec(memory_space=pl.ANY)] * n_alias
    out_specs = [
        pl.BlockSpec((1, TM, D_MODEL), lambda b, j: (b, j, 0)),
        pl.BlockSpec((None, 1, WINDOW, D_KV), lambda b, j: (layer, b, 0, 0)),
        pl.BlockSpec((None, 1, WINDOW, D_KV), lambda b, j: (layer, b, 0, 0)),
        pl.BlockSpec((None, 1, POOL_HIST, D_POOL), lambda b, j: (layer, b, 0, 0)),
    ]
    out_shape = [
        jax.ShapeDtypeStruct((BATCH, SEQ, D_MODEL), F32),
        jax.ShapeDtypeStruct((DEPTH, BATCH, WINDOW, D_KV), F32),
        jax.ShapeDtypeStruct((DEPTH, BATCH, WINDOW, D_KV), F32),
        jax.ShapeDtypeStruct((DEPTH, BATCH, POOL_HIST, D_POOL), F32),
    ]
    scratch = [
        pltpu.VMEM((TM, D_MODEL), BF16),
        pltpu.VMEM((TM, D_ATT), BF16),
        pltpu.VMEM((TM, D_ATT), F32),
        pltpu.VMEM((2 * N_KV, WINDOW + TM, LANES), BF16),
        pltpu.VMEM((N_KV, WINDOW + TM, D_KV), BF16),
        pltpu.VMEM((HIST_PAD + TM, D_POOL), F32),
        pltpu.VMEM((TM, D_MODEL), BF16),
    ]
    n_in = len(in_specs) - n_alias
    return pl.pallas_call(
        functools.partial(_prompt_body, layer, n_alias),
        grid=grid,
        in_specs=in_specs,
        out_specs=out_specs,
        out_shape=out_shape,
        scratch_shapes=scratch,
        input_output_aliases={n_in + i: 1 + i for i in range(n_alias)},
        compiler_params=pltpu.CompilerParams(
            dimension_semantics=("arbitrary", "arbitrary"),
            vmem_limit_bytes=VMEM_LIMIT_PROMPT),
        name="prompt_layer",
    )(sinks, x, cos, sin, w_in_b, w_out_b, npre, npost, pw_b, ps, lng, lnb, sw, sb_full, *prev_outs)


def _sample_proj_body(x_ref, npre_ref, w_ref, z_ref, hbuf):
    @pl.when(pl.program_id(0) == 0)
    def _norm_once():
        hbuf[...] = _rms(x_ref[...], npre_ref[...]).astype(BF16)

    z_ref[...] = _dot(hbuf[...], w_ref[...])


def _sample_proj(layer, xs, npre, w_in_b):
    return pl.pallas_call(
        _sample_proj_body,
        grid=(D_IN // S1_TN,),
        in_specs=[
            pl.BlockSpec((SAMPLE_ROWS, D_MODEL), lambda n: (0, 0)),
            pl.BlockSpec((None, 1, D_MODEL), lambda n: (layer, 0, 0)),
            pl.BlockSpec((None, D_MODEL, S1_TN), lambda n: (layer, 0, n)),
        ],
        out_specs=pl.BlockSpec((SAMPLE_ROWS, S1_TN), lambda n: (0, n)),
        out_shape=jax.ShapeDtypeStruct((SAMPLE_ROWS, D_IN), F32),
        scratch_shapes=[pltpu.VMEM((SAMPLE_ROWS, D_MODEL), BF16)],
        compiler_params=pltpu.CompilerParams(
            dimension_semantics=("arbitrary",), vmem_limit_bytes=VMEM_LIMIT_SAMPLE),
        name="sample_proj",
    )(xs, npre, w_in_b)


SCORE_ROWS = N_HEADS * GR
N_REST_BLOCKS = (D_IN - C_XB) // LANES


def _sample_mix_body(n_alias, z_ref, cos_ref, sin_ref, sink_ref, ck_ref, cv_ref, st_ref,
                     pw_ref, ps_ref, lng_ref, lnb_ref, wexp_ref, bexp_ref, *rest):
    mix_ref, nk_ref, nv_ref, npool_ref, vn_ref, qbig, knew_f, vnew_f, zc, sc, npc, mo, vo = rest[n_alias:]
    cos = cos_ref[...]
    sin = sin_ref[...]

    q = _rope(z_ref[:, C_Q:C_Q + D_ATT], cos * ATT_SCALE, sin * ATT_SCALE)
    knew_f[...] = _rope(z_ref[:, C_K:C_K + D_KV], cos, sin)
    vnew_f[...] = z_ref[:, C_V:C_V + D_KV]
    low = lax.broadcasted_iota(jnp.int32, (RT, LANES), 1) < HEAD_DIM
    zero_blk = jnp.zeros((RT, LANES), F32)
    for h in range(N_HEADS):
        g = h // GROUP
        src = q[:, (h // 2) * LANES:(h // 2 + 1) * LANES]
        if h % 2 != g % 2:
            src = _swap_heads(src)
        piece = jnp.where(low, src, 0.0) if g % 2 == 0 else jnp.where(low, 0.0, src)
        full = [zero_blk, zero_blk]
        full[g // 2] = piece
        qbig[h] = jnp.concatenate(full, axis=1).astype(BF16)

    row_h = lax.broadcasted_iota(jnp.int32, (SCORE_ROWS, GB * WINDOW), 0)
    col_h = lax.broadcasted_iota(jnp.int32, (SCORE_ROWS, GB * WINDOW), 1)
    same_h = ((row_h >> 2) & (GB - 1)) == (col_h >> 7)
    bias_h = jnp.where(same_h, jnp.where((col_h & (WINDOW - 1)) > (row_h & (DEC_SEQ - 1)), 0.0, NEG_INF),
                       NEG_INF)
    row_n = lax.broadcasted_iota(jnp.int32, (SCORE_ROWS, GR), 0)
    col_n = lax.broadcasted_iota(jnp.int32, (SCORE_ROWS, GR), 1)
    same_n = ((row_n >> 2) & (GB - 1)) == (col_n >> 2)
    bias_n = jnp.where(same_n, jnp.where((col_n & (DEC_SEQ - 1)) <= (row_n & (DEC_SEQ - 1)), 0.0, NEG_INF),
                       NEG_INF)
    sink = sink_ref[:, 0:1]
    low_g = lax.broadcasted_iota(jnp.int32, (GR, LANES), 1) < HEAD_DIM
    row8 = lax.broadcasted_iota(jnp.int32, (SUBLANES, D_KV), 0)

    def roll_in(hist, new_tile):
        rolled = pltpu.roll(hist, WINDOW - DEC_SEQ, 0)
        tail = jnp.where(row8 >= SUBLANES - DEC_SEQ, new_tile, rolled[WINDOW - SUBLANES:])
        return jnp.concatenate([rolled[:WINDOW - SUBLANES], tail], axis=0)

    for grp in range(BT // GB):
        r0 = grp * GR
        lhs = qbig[:, r0:r0 + GR, :].reshape(SCORE_ROWS, D_KV)
        kh = ck_ref[grp * GB:(grp + 1) * GB]
        vh = cv_ref[grp * GB:(grp + 1) * GB]
        kn = knew_f[r0:r0 + GR, :]
        vn_new = vnew_f[r0:r0 + GR, :]
        s_h = _dot_nt(lhs, kh.reshape(GB * WINDOW, D_KV).astype(BF16)) + bias_h
        s_n = _dot_nt(lhs, kn.astype(BF16)) + bias_n
        m = jnp.maximum(jnp.maximum(jnp.max(s_h, axis=1, keepdims=True),
                                    jnp.max(s_n, axis=1, keepdims=True)), sink)
        p_h = jnp.exp(s_h - m)
        p_n = jnp.exp(s_n - m)
        den = (jnp.sum(p_h, axis=1, keepdims=True) + jnp.sum(p_n, axis=1, keepdims=True)
               + jnp.exp(sink - m))
        inv = 1.0 / den
        o = (_dot((p_h * inv).astype(BF16), vh.reshape(GB * WINDOW, D_KV).astype(BF16))
             + _dot((p_n * inv).astype(BF16), vn_new.astype(BF16)))
        for c in range(N_HEADS // 2):
            g = c // 2
            a = o[(2 * c) * GR:(2 * c + 1) * GR, (g // 2) * LANES:(g // 2 + 1) * LANES]
            b = o[(2 * c + 1) * GR:(2 * c + 2) * GR, (g // 2) * LANES:(g // 2 + 1) * LANES]
            blk = jnp.where(low_g, a, _swap_heads(b)) if g % 2 == 0 else jnp.where(low_g, _swap_heads(a), b)
            gate = _silu(z_ref[r0:r0 + GR, C_GA + c * LANES:C_GA + (c + 1) * LANES])
            mix_ref[r0:r0 + GR, c * LANES:(c + 1) * LANES] = blk * gate
        for bb in range(GB):
            k_tile = kn[(bb // 2) * SUBLANES:(bb // 2 + 1) * SUBLANES]
            v_tile = vn_new[(bb // 2) * SUBLANES:(bb // 2 + 1) * SUBLANES]
            if bb % 2 == 0:
                k_tile = pltpu.roll(k_tile, SUBLANES - DEC_SEQ, 0)
                v_tile = pltpu.roll(v_tile, SUBLANES - DEC_SEQ, 0)
            nk_ref[grp * GB + bb] = roll_in(kh[bb], k_tile)
            nv_ref[grp * GB + bb] = roll_in(vh[bb], v_tile)

    for c in range(N_REST_BLOCKS):
        zc[c] = z_ref[:, C_XB + c * LANES:C_XB + (c + 1) * LANES]
    for c in range(D_POOL // LANES):
        sc[c] = st_ref[:, c * LANES:(c + 1) * LANES]

    def tok(t, lo, width):
        blk0 = (lo - C_XB) // LANES
        parts = [zc[blk0 + c, pl.ds(t, BT, stride=DEC_SEQ), :] for c in range(width // LANES)]
        return parts[0] if len(parts) == 1 else jnp.concatenate(parts, axis=1)

    def ext(i, gi):
        if i < POOL_HIST:
            return sc[gi, pl.ds(i, BT, stride=POOL_HIST), :]
        return tok(i - POOL_HIST, C_XB + gi * LANES, LANES)

    for gi, w in enumerate(POOL_WINDOWS):
        for s in range(POOL_HIST):
            npc[gi, pl.ds(s, BT, stride=POOL_HIST), :] = ext(s + DEC_SEQ, gi)
        pooled = []
        for t in range(DEC_SEQ):
            cur = ext(POOL_HIST + t, gi)
            acc = cur
            for i in range(1, w):
                acc = acc + ext(POOL_HIST + t - i, gi)
            cnt = float(min(w, PAST_LEN + t + 1))
            pooled.append(acc / cnt - cur)
        pooled = jnp.concatenate(pooled, axis=0).astype(BF16)
        mixed = _dot(pooled, pw_ref[gi]) * ps_ref[:, gi * LANES:(gi + 1) * LANES]
        for t in range(DEC_SEQ):
            gate = _silu(tok(t, C_GB + gi * LANES, LANES))
            mo[gi, pl.ds(t, BT, stride=DEC_SEQ), :] = mixed[t * BT:(t + 1) * BT] * gate

    vns = []
    for t in range(DEC_SEQ):
        vn_t = _layernorm(tok(t, C_VS, D_SGU), lng_ref[...], lnb_ref[...])
        for c in range(D_SGU // LANES):
            vo[c, pl.ds(t, BT, stride=DEC_SEQ), :] = vn_t[:, c * LANES:(c + 1) * LANES]
        vns.append(vn_t)
    for t in range(DEC_SEQ):
        mixed = bexp_ref[t:t + 1, :]
        for s in range(t + 1):
            mixed = mixed + wexp_ref[DEC_SEQ * t + s:DEC_SEQ * t + s + 1, :] * vns[s]
        out = tok(t, C_U, D_SGU) * mixed * _silu(tok(t, C_GC, D_SGU))
        for c in range(D_SGU // LANES):
            mo[D_POOL // LANES + c, pl.ds(t, BT, stride=DEC_SEQ), :] = out[:, c * LANES:(c + 1) * LANES]

    for c in range(D_POOL // LANES):
        npool_ref[:, c * LANES:(c + 1) * LANES] = npc[c]
    for c in range((D_POOL + D_SGU) // LANES):
        mix_ref[:, D_ATT + c * LANES:D_ATT + (c + 1) * LANES] = mo[c]
    for c in range(D_SGU // LANES):
        vn_ref[:, c * LANES:(c + 1) * LANES] = vo[c]


def _sample_mix(layer, z, cos_t, sin_t, params, ck, cv, state2, prev_outs):
    (sink_rows, pw_b, ps, lng, lnb, wexp, bexp) = params
    n_alias = len(prev_outs)
    const = lambda shape: pl.BlockSpec(shape, lambda i: (0,) * len(shape))
    in_specs = [
        pl.BlockSpec((RT, D_IN), lambda i: (i, 0)),
        const((RT, LANES)), const((RT, LANES)),
        _layer_spec((SCORE_ROWS, LANES), layer, 1),
        pl.BlockSpec((None, BT, WINDOW, D_KV), lambda i: (layer, i, 0, 0)),
        pl.BlockSpec((None, BT, WINDOW, D_KV), lambda i: (layer, i, 0, 0)),
        pl.BlockSpec((None, BT * POOL_HIST, D_POOL), lambda i: (layer, i, 0)),
        _layer_spec((len(POOL_WINDOWS), LANES, LANES), layer, 1),
        _layer_spec((1, D_POOL), layer, 1),
        _layer_spec((1, D_SGU), layer, 1),
        _layer_spec((1, D_SGU), layer, 1),
        _layer_spec((DEC_SEQ * DEC_SEQ, D_SGU), layer, 1),
        _layer_spec((DEC_SEQ, D_SGU), layer, 1),
    ] + [pl.BlockSpec(memory_space=pl.ANY)] * n_alias
    out_specs = [
        pl.BlockSpec((RT, D_MODEL), lambda i: (i, 0)),
        pl.BlockSpec((None, BT, WINDOW, D_KV), lambda i: (layer, i, 0, 0)),
        pl.BlockSpec((None, BT, WINDOW, D_KV), lambda i: (layer, i, 0, 0)),
        pl.BlockSpec((None, BT * POOL_HIST, D_POOL), lambda i: (layer, i, 0)),
        pl.BlockSpec((None, RT, D_SGU), lambda i: (layer, i, 0)),
    ]
    out_shape = [
        jax.ShapeDtypeStruct((SAMPLE_ROWS, D_MODEL), F32),
        jax.ShapeDtypeStruct((DEPTH, DEC_BATCH, WINDOW, D_KV), F32),
        jax.ShapeDtypeStruct((DEPTH, DEC_BATCH, WINDOW, D_KV), F32),
        jax.ShapeDtypeStruct((DEPTH, DEC_BATCH * POOL_HIST, D_POOL), F32),
        jax.ShapeDtypeStruct((DEPTH, SAMPLE_ROWS, D_SGU), F32),
    ]
    scratch = [
        pltpu.VMEM((N_HEADS, RT, D_KV), BF16),
        pltpu.VMEM((RT, D_KV), F32),
        pltpu.VMEM((RT, D_KV), F32),
        pltpu.VMEM((N_REST_BLOCKS, RT, LANES), F32),
        pltpu.VMEM((D_POOL // LANES, BT * POOL_HIST, LANES), F32),
        pltpu.VMEM((D_POOL // LANES, BT * POOL_HIST, LANES), F32),
        pltpu.VMEM(((D_POOL + D_SGU) // LANES, RT, LANES), F32),
        pltpu.VMEM((D_SGU // LANES, RT, LANES), F32),
    ]
    n_in = len(in_specs) - n_alias
    return pl.pallas_call(
        functools.partial(_sample_mix_body, n_alias),
        grid=(DEC_BATCH // BT,),
        in_specs=in_specs,
        out_specs=out_specs,
        out_shape=out_shape,
        scratch_shapes=scratch,
        input_output_aliases={n_in + i: 1 + i for i in range(n_alias)},
        compiler_params=pltpu.CompilerParams(
            dimension_semantics=("arbitrary",), vmem_limit_bytes=VMEM_LIMIT_SAMPLE),
        name="sample_mix",
    )(z, cos_t, sin_t, sink_rows, ck, cv, state2, pw_b, ps, lng, lnb, wexp, bexp, *prev_outs)


def _sample_out_body(x_ref, mix_ref, w_out_ref, npost_ref, y_ref):
    y = _dot(mix_ref[...].astype(BF16), w_out_ref[...])
    y_ref[...] = x_ref[...] + _rms(y, npost_ref[...])


def _sample_out(layer, xs, mix, w_out_b, npost):
    full = lambda shape: pl.BlockSpec(shape, lambda i: (0,) * len(shape))
    return pl.pallas_call(
        _sample_out_body,
        grid=(1,),
        in_specs=[
            full((SAMPLE_ROWS, D_MODEL)),
            full((SAMPLE_ROWS, D_MODEL)),
            pl.BlockSpec((None, D_MODEL, D_MODEL), lambda i: (layer, 0, 0)),
            pl.BlockSpec((None, 1, D_MODEL), lambda i: (layer, 0, 0)),
        ],
        out_specs=full((SAMPLE_ROWS, D_MODEL)),
        out_shape=jax.ShapeDtypeStruct((SAMPLE_ROWS, D_MODEL), F32),
        compiler_params=pltpu.CompilerParams(
            dimension_semantics=("arbitrary",), vmem_limit_bytes=VMEM_LIMIT_SAMPLE),
        name="sample_out",
    )(xs, mix, w_out_b, npost)


def _rope_tables(pos):
    inv = ROPE_THETA ** (-jnp.arange(0, HEAD_DIM, 2, dtype=F32) / HEAD_DIM)
    ang = pos.astype(F32)[:, None] * inv[None, :]
    c, s = jnp.cos(ang), jnp.sin(ang)
    return jnp.concatenate([c, c, c, c], axis=1), jnp.concatenate([-s, s, -s, s], axis=1)


def kernel(x_prompt, x_sample, cache_k, cache_v, state_pool, w_in, w_out, norm_pre, norm_post,
           attn_sinks, pool_w, pool_scale, sgu_ln_g, sgu_ln_b, sgu_w, sgu_b):
    cos_p, sin_p = _rope_tables(jnp.arange(SEQ))
    cos_s, sin_s = _rope_tables(PAST_LEN + jnp.arange(DEC_SEQ))
    cos_t = jnp.tile(cos_s, (BT, 1))
    sin_t = jnp.tile(sin_s, (BT, 1))

    w_in_b = w_in.astype(BF16)
    w_out_b = w_out.astype(BF16)
    pw_b = pool_w.astype(BF16)
    npre = norm_pre[:, None, :]
    npost = norm_post[:, None, :]
    ps = pool_scale[:, None, :]
    lng = sgu_ln_g[:, None, :]
    lnb = sgu_ln_b[:, None, :]
    sb_full = jnp.broadcast_to(sgu_b[:, :, :, None], (DEPTH, N_SGU_HEADS, CHUNK, LANES))
    wexp = jnp.repeat(sgu_w[:, :, :DEC_SEQ, :DEC_SEQ].transpose(0, 2, 3, 1), LANES, axis=-1
                      ).reshape(DEPTH, DEC_SEQ * DEC_SEQ, D_SGU)
    bexp = jnp.repeat(sgu_b[:, :, :DEC_SEQ].transpose(0, 2, 1), LANES, axis=-1)
    sink_rows = jnp.broadcast_to(attn_sinks[:, :, None, None], (DEPTH, N_HEADS, GR, LANES)
                                 ).reshape(DEPTH, SCORE_ROWS, LANES)

    xp = x_prompt
    xs = x_sample.reshape(SAMPLE_ROWS, D_MODEL)
    ck = cache_k.reshape(DEPTH, DEC_BATCH, WINDOW, D_KV)
    cv = cache_v.reshape(DEPTH, DEC_BATCH, WINDOW, D_KV)
    state2 = state_pool.reshape(DEPTH, DEC_BATCH * POOL_HIST, D_POOL)
    prompt_params = (attn_sinks, w_in_b, w_out_b, npre, npost, pw_b, ps, lng, lnb, sgu_w, sb_full)
    sample_params = (sink_rows, pw_b, ps, lng, lnb, wexp, bexp)
    p_outs, s_outs = [], []
    for layer in range(DEPTH):
        xp, *p_outs = _prompt_layer(layer, xp, cos_p, sin_p, prompt_params, p_outs)
        z = _sample_proj(layer, xs, npre, w_in_b)
        mix, *s_outs = _sample_mix(layer, z, cos_t, sin_t, sample_params, ck, cv, state2, s_outs)
        xs = _sample_out(layer, xs, mix, w_out_b, npost)

    nk_p, nv_p, np_p = p_outs
    nk_s, nv_s, np_s, vn_s = s_outs
    kv5 = lambda a, nb: a.reshape(DEPTH, nb, WINDOW, N_KV, HEAD_DIM)
    return (xp, xs.reshape(DEC_BATCH, DEC_SEQ, D_MODEL),
            kv5(nk_p, BATCH), kv5(nv_p, BATCH), np_p,
            kv5(nk_s, DEC_BATCH), kv5(nv_s, DEC_BATCH),
            np_s.reshape(DEPTH, DEC_BATCH, POOL_HIST, D_POOL),
            vn_s.reshape(DEPTH, DEC_BATCH, DEC_SEQ, D_SGU))
```

```python
import functools

import jax
import jax.numpy as jnp
from jax import lax
from jax.experimental import pallas as pl
from jax.experimental.pallas import tpu as pltpu

D_MODEL = 2048
SEQ = 2048
BATCH = 8
DEPTH = 2
DEC_BATCH = 128
DEC_SEQ = 4
PAST_LEN = 8192
HEAD_DIM = 64
HALF_DIM = HEAD_DIM // 2
N_HEADS = 16
N_KV = 4
GROUP = N_HEADS // N_KV
D_ATT = N_HEADS * HEAD_DIM
D_KV = N_KV * HEAD_DIM
WINDOW = 128
ROPE_THETA = 10000.0
D_POOL = 512
POOL_WINDOWS = (2, 4, 8, 16)
POOL_HIST = 15
D_SGU = 512
CHUNK = 128
N_SGU_HEADS = 4
D_IN = 5120
EPS = 1e-6
ATT_SCALE = HEAD_DIM ** -0.5

C_Q = 0
C_K = C_Q + D_ATT
C_V = C_K + D_KV
C_GA = C_V + D_KV
C_XB = C_GA + D_ATT
C_GB = C_XB + D_POOL
C_U = C_GB + D_POOL
C_VS = C_U + D_SGU
C_GC = C_VS + D_SGU
assert C_GC + D_SGU == D_IN

LANES = 128
SUBLANES = 8
HIST_PAD = 16
TM = 256
N_CHUNKS = TM // CHUNK
SAMPLE_ROWS = DEC_BATCH * DEC_SEQ
BT = 16
RT = BT * DEC_SEQ
GB = 4
GR = GB * DEC_SEQ
REST_PIECE = 256
N_REST_PIECES = (D_IN - C_XB) // REST_PIECE
S1_TN = 1024
VMEM_LIMIT_PROMPT = 56 * 1024 * 1024
VMEM_LIMIT_SAMPLE = 48 * 1024 * 1024

F32 = jnp.float32
BF16 = jnp.bfloat16
NEG_INF = float("-inf")


def _rms(x, g):
    return x * lax.rsqrt(jnp.mean(x * x, axis=-1, keepdims=True) + EPS) * g


def _silu(x):
    return x * (1.0 / (1.0 + jnp.exp(-x)))


def _layernorm(x, g, b):
    mu = jnp.mean(x, axis=-1, keepdims=True)
    xc = x - mu
    var = jnp.mean(xc * xc, axis=-1, keepdims=True)
    return xc * lax.rsqrt(var + EPS) * g + b


def _swap_heads(x):
    return pltpu.roll(x, HEAD_DIM, 1)


def _rope(x, cos, sin):
    rows = x.shape[0]
    lane = lax.broadcasted_iota(jnp.int32, (rows, LANES), 1)
    first_half = (lane & HALF_DIM) == 0
    outs = []
    for c in range(x.shape[1] // LANES):
        xc = x[:, c * LANES:(c + 1) * LANES]
        partner = jnp.where(first_half,
                            pltpu.roll(xc, LANES - HALF_DIM, 1),
                            pltpu.roll(xc, HALF_DIM, 1))
        outs.append(xc * cos + partner * sin)
    return jnp.concatenate(outs, axis=1)


def _dot(a, b):
    return jnp.dot(a, b, preferred_element_type=F32)


def _dot_nt(a, b):
    return lax.dot_general(a, b, (((1,), (1,)), ((), ())), preferred_element_type=F32)


def _prompt_body(layer, n_alias, sinks_ref, x_ref, cos_ref, sin_ref, w_in_ref, w_out_ref,
                 npre_ref, npost_ref, pw_ref, ps_ref, lng_ref, lnb_ref, sw_ref, sb_ref, *rest):
    y_ref, nk_ref, nv_ref, np_ref, hbuf, qbuf, gbuf, kbuf, vbuf, xbext, mixbuf, zrest = rest[n_alias:]
    j = pl.program_id(1)

    @pl.when(j == 0)
    def _start_of_sequence():
        kbuf[:, 0:WINDOW, :] = jnp.zeros((2 * N_KV, WINDOW, LANES), BF16)
        vbuf[:, 0:WINDOW, :] = jnp.zeros((N_KV, WINDOW, D_KV), BF16)
        xbext[0:HIST_PAD, :] = jnp.zeros((HIST_PAD, D_POOL), F32)

    @pl.when(j > 0)
    def _carry_from_previous_step():
        for i in range(2 * N_KV):
            kbuf[i, 0:WINDOW, :] = kbuf[i, TM:TM + WINDOW, :]
        for g in range(N_KV):
            vbuf[g, 0:WINDOW, :] = vbuf[g, TM:TM + WINDOW, :]
        xbext[0:HIST_PAD, :] = xbext[TM:TM + HIST_PAD, :]

    h = _rms(x_ref[0], npre_ref[...]).astype(BF16)

    def proj(lo, width):
        return _dot(h, w_in_ref[:, lo:lo + width])

    cos = cos_ref[...]
    sin = sin_ref[...]

    k = _rope(proj(C_K, D_KV), cos, sin)
    v = proj(C_V, D_KV)
    qbuf[...] = _rope(proj(C_Q, D_ATT), cos * ATT_SCALE, sin * ATT_SCALE).astype(BF16)
    gbuf[...] = _silu(proj(C_GA, D_ATT))
    nk_ref[0] = k[TM - WINDOW:TM]
    nv_ref[0] = v[TM - WINDOW:TM]
    lane_k = lax.broadcasted_iota(jnp.int32, (TM, LANES), 1)
    low = lane_k < HEAD_DIM
    lane_v = lax.broadcasted_iota(jnp.int32, (TM, D_KV), 1)
    for pair in range(2):
        kp = k[:, pair * LANES:(pair + 1) * LANES]
        ks = _swap_heads(kp)
        rows = slice(WINDOW, WINDOW + TM)
        kbuf[4 * pair + 0, rows, :] = jnp.where(low, kp, 0.0).astype(BF16)
        kbuf[4 * pair + 1, rows, :] = jnp.where(low, 0.0, ks).astype(BF16)
        kbuf[4 * pair + 2, rows, :] = jnp.where(low, ks, 0.0).astype(BF16)
        kbuf[4 * pair + 3, rows, :] = jnp.where(low, 0.0, kp).astype(BF16)
    for g in range(N_KV):
        keep_v = (lane_v >= g * HEAD_DIM) & (lane_v < (g + 1) * HEAD_DIM)
        vbuf[g, WINDOW:WINDOW + TM, :] = jnp.where(keep_v, v, 0.0).astype(BF16)

    qi = lax.broadcasted_iota(jnp.int32, (CHUNK, 2 * WINDOW), 0)
    kj = lax.broadcasted_iota(jnp.int32, (CHUNK, 2 * WINDOW), 1)
    dist = qi + WINDOW - kj
    band_bias = jnp.where((dist >= 0) & (dist < WINDOW), 0.0, NEG_INF)
    no_prev = jnp.where(j == 0, WINDOW, 0)
    first_bias = jnp.where(kj < no_prev, NEG_INF, band_bias)
    low_c = lax.broadcasted_iota(jnp.int32, (CHUNK, LANES), 1) < HEAD_DIM
    rest_piece = 0

    def project_rest_piece(piece):
        cols = slice(piece * REST_PIECE, (piece + 1) * REST_PIECE)
        zrest[:, cols] = proj(C_XB + piece * REST_PIECE, REST_PIECE)

    def scores(c, g):
        r0 = c * CHUNK
        qs = jnp.concatenate(
            [qbuf[r0:r0 + CHUNK, (2 * g + i) * LANES:(2 * g + i + 1) * LANES] for i in range(2)],
            axis=0)
        return [_dot_nt(qs, kbuf[2 * g + half, r0:r0 + 2 * WINDOW, :]) for half in range(2)]

    order = [(c, g) for c in range(N_CHUNKS) for g in range(N_KV)]
    while rest_piece < N_REST_PIECES - len(order):
        project_rest_piece(rest_piece)
        rest_piece += 1
    s_next = scores(*order[0])
    for step, (c, g) in enumerate(order):
        bias = first_bias if c == 0 else band_bias
        r0 = c * CHUNK
        s_half = s_next
        if step + 1 < len(order):
            s_next = scores(*order[step + 1])
        blocks = []
        for r in range(GROUP):
            sb = s_half[r % 2][(r // 2) * CHUNK:(r // 2 + 1) * CHUNK] + bias
            sink = sinks_ref[layer, GROUP * g + r]
            m = jnp.maximum(jnp.max(sb, axis=1, keepdims=True), sink)
            pe = jnp.exp(sb - m)
            den = jnp.sum(pe, axis=1, keepdims=True) + jnp.exp(sink - m)
            blocks.append((pe * (1.0 / den)).astype(BF16))
        pg = jnp.concatenate(blocks, axis=0)
        og = _dot(pg, vbuf[g, r0:r0 + 2 * WINDOW, :])
        o = og if g == 0 else o + og
        if rest_piece < N_REST_PIECES:
            project_rest_piece(rest_piece)
            rest_piece += 1
        if g < N_KV - 1:
            continue
        for pair in range(2):
            for i in range(2):
                a = o[(2 * i) * CHUNK:(2 * i + 1) * CHUNK, pair * LANES:(pair + 1) * LANES]
                b = o[(2 * i + 1) * CHUNK:(2 * i + 2) * CHUNK, pair * LANES:(pair + 1) * LANES]
                for odd, blk in ((0, jnp.where(low_c, a, _swap_heads(b))),
                                 (1, jnp.where(low_c, _swap_heads(a), b))):
                    cols = slice((2 * (2 * pair + odd) + i) * LANES, (2 * (2 * pair + odd) + i + 1) * LANES)
                    mixbuf[r0:r0 + CHUNK, cols] = (blk * gbuf[r0:r0 + CHUNK, cols]).astype(BF16)

    while rest_piece < N_REST_PIECES:
        project_rest_piece(rest_piece)
        rest_piece += 1

    def rest(lo, width):
        return zrest[:, lo - C_XB:lo - C_XB + width]

    xbext[HIST_PAD:HIST_PAD + TM, :] = rest(C_XB, D_POOL)
    np_ref[0] = xbext[HIST_PAD + TM - POOL_HIST:HIST_PAD + TM, :]
    gate_b = _silu(rest(C_GB, D_POOL))
    pos1 = j * TM + lax.broadcasted_iota(jnp.int32, (TM, LANES), 0) + 1
    for gi, w in enumerate(POOL_WINDOWS):
        cols = slice(gi * LANES, (gi + 1) * LANES)
        cur = xbext[HIST_PAD:HIST_PAD + TM, cols]
        acc = cur
        for i in range(1, w):
            acc = acc + xbext[HIST_PAD - i:HIST_PAD - i + TM, cols]
        cnt = jnp.minimum(w, pos1).astype(F32)
        pooled = acc / cnt - cur
        mixed = _dot(pooled.astype(BF16), pw_ref[gi]) * ps_ref[:, cols]
        mixbuf[:, D_ATT + gi * LANES:D_ATT + (gi + 1) * LANES] = (mixed * gate_b[:, cols]).astype(BF16)

    vn = _layernorm(rest(C_VS, D_SGU), lng_ref[...], lnb_ref[...]).astype(BF16)
    u = rest(C_U, D_SGU)
    gate_c = _silu(rest(C_GC, D_SGU))
    ti = lax.broadcasted_iota(jnp.int32, (CHUNK, CHUNK), 0)
    si = lax.broadcasted_iota(jnp.int32, (CHUNK, CHUNK), 1)
    for hh in range(N_SGU_HEADS):
        cols = slice(hh * LANES, (hh + 1) * LANES)
        wm = jnp.where(ti >= si, sw_ref[hh], 0.0).astype(BF16)
        for c in range(N_CHUNKS):
            rows = slice(c * CHUNK, (c + 1) * CHUNK)
            mixed = _dot(wm, vn[rows, cols]) + sb_ref[hh]
            mixbuf[rows, D_ATT + D_POOL + hh * LANES:D_ATT + D_POOL + (hh + 1) * LANES] = (
                u[rows, cols] * mixed * gate_c[rows, cols]).astype(BF16)

    y = _dot(mixbuf[...], w_out_ref[...])
    y_ref[0] = x_ref[0] + _rms(y, npost_ref[...])


def _layer_spec(shape, layer, grid_rank):
    zeros = (0,) * len(shape)
    if grid_rank == 2:
        index_map = lambda b, j: (layer,) + zeros
    else:
        index_map = lambda i: (layer,) + zeros
    return pl.BlockSpec((None,) + tuple(shape), index_map, pipeline_mode=pl.Buffered(1))


def _prompt_layer(layer, x, cos, sin, params, prev_outs):
    (sinks, w_in_b, w_out_b, npre, npost, pw_b, ps, lng, lnb, sw, sb_full) = params
    grid = (BATCH, SEQ // TM)
    n_alias = len(prev_outs)
    in_specs = [
        pl.BlockSpec(memory_space=pltpu.SMEM),
        pl.BlockSpec((1, TM, D_MODEL), lambda b, j: (b, j, 0)),
        pl.BlockSpec((TM, LANES), lambda b, j: (j, 0)),
        pl.BlockSpec((TM, LANES), lambda b, j: (j, 0)),
        _layer_spec((D_MODEL, D_IN), layer, 2),
        _layer_spec((D_MODEL, D_MODEL), layer, 2),
        _layer_spec((1, D_MODEL), layer, 2),
        _layer_spec((1, D_MODEL), layer, 2),
        _layer_spec((len(POOL_WINDOWS), LANES, LANES), layer, 2),
        _layer_spec((1, D_POOL), layer, 2),
        _layer_spec((1, D_SGU), layer, 2),
        _layer_spec((1, D_SGU), layer, 2),
        _layer_spec((N_SGU_HEADS, CHUNK, CHUNK), layer, 2),
        _layer_spec((N_SGU_HEADS, CHUNK, LANES), layer, 2),
    ] + [pl.BlockSpec(memory_space=pl.ANY)] * n_alias
    out_specs = [
        pl.BlockSpec((1, TM, D_MODEL), lambda b, j: (b, j, 0)),
        pl.BlockSpec((None, 1, WINDOW, D_KV), lambda b, j: (layer, b, 0, 0)),
        pl.BlockSpec((None, 1, WINDOW, D_KV), lambda b, j: (layer, b, 0, 0)),
        pl.BlockSpec((None, 1, POOL_HIST, D_POOL), lambda b, j: (layer, b, 0, 0)),
    ]
    out_shape = [
        jax.ShapeDtypeStruct((BATCH, SEQ, D_MODEL), F32),
        jax.ShapeDtypeStruct((DEPTH, BATCH, WINDOW, D_KV), F32),
        jax.ShapeDtypeStruct((DEPTH, BATCH, WINDOW, D_KV), F32),
        jax.ShapeDtypeStruct((DEPTH, BATCH, POOL_HIST, D_POOL), F32),
    ]
    scratch = [
        pltpu.VMEM((TM, D_MODEL), BF16),
        pltpu.VMEM((TM, D_ATT), BF16),
        pltpu.VMEM((TM, D_ATT), F32),
        pltpu.VMEM((2 * N_KV, WINDOW + TM, LANES), BF16),
        pltpu.VMEM((N_KV, WINDOW + TM, D_KV), BF16),
        pltpu.VMEM((HIST_PAD + TM, D_POOL), F32),
        pltpu.VMEM((TM, D_MODEL), BF16),
        pltpu.VMEM((TM, D_IN - C_XB), F32),
    ]
    n_in = len(in_specs) - n_alias
    return pl.pallas_call(
        functools.partial(_prompt_body, layer, n_alias),
        grid=grid,
        in_specs=in_specs,
        out_specs=out_specs,
        out_shape=out_shape,
        scratch_shapes=scratch,
        input_output_aliases={n_in + i: 1 + i for i in range(n_alias)},
        compiler_params=pltpu.CompilerParams(
            dimension_semantics=("arbitrary", "arbitrary"),
            vmem_limit_bytes=VMEM_LIMIT_PROMPT),
        name="prompt_layer",
    )(sinks, x, cos, sin, w_in_b, w_out_b, npre, npost, pw_b, ps, lng, lnb, sw, sb_full, *prev_outs)


def _sample_proj_body(x_ref, npre_ref, w_ref, z_ref, hbuf):
    @pl.when(pl.program_id(0) == 0)
    def _norm_once():
        hbuf[...] = _rms(x_ref[...], npre_ref[...]).astype(BF16)

    z_ref[...] = _dot(hbuf[...], w_ref[...])


def _sample_proj(layer, xs, npre, w_in_b):
    return pl.pallas_call(
        _sample_proj_body,
        grid=(D_IN // S1_TN,),
        in_specs=[
            pl.BlockSpec((SAMPLE_ROWS, D_MODEL), lambda n: (0, 0)),
            pl.BlockSpec((None, 1, D_MODEL), lambda n: (layer, 0, 0)),
            pl.BlockSpec((None, D_MODEL, S1_TN), lambda n: (layer, 0, n)),
        ],
        out_specs=pl.BlockSpec((SAMPLE_ROWS, S1_TN), lambda n: (0, n)),
        out_shape=jax.ShapeDtypeStruct((SAMPLE_ROWS, D_IN), F32),
        scratch_shapes=[pltpu.VMEM((SAMPLE_ROWS, D_MODEL), BF16)],
        compiler_params=pltpu.CompilerParams(
            dimension_semantics=("arbitrary",), vmem_limit_bytes=VMEM_LIMIT_SAMPLE),
        name="sample_proj",
    )(xs, npre, w_in_b)


SCORE_ROWS = N_HEADS * GR
N_REST_BLOCKS = (D_IN - C_XB) // LANES


def _sample_mix_body(n_alias, z_ref, cos_ref, sin_ref, sink_ref, ck_ref, cv_ref, st_ref,
                     pw_ref, ps_ref, lng_ref, lnb_ref, wexp_ref, bexp_ref, *rest):
    mix_ref, nk_ref, nv_ref, npool_ref, vn_ref, qbig, knew_f, vnew_f, zc, sc, npc, mo, vo = rest[n_alias:]
    cos = cos_ref[...]
    sin = sin_ref[...]

    q = _rope(z_ref[:, C_Q:C_Q + D_ATT], cos * ATT_SCALE, sin * ATT_SCALE)
    knew_f[...] = _rope(z_ref[:, C_K:C_K + D_KV], cos, sin)
    vnew_f[...] = z_ref[:, C_V:C_V + D_KV]
    low = lax.broadcasted_iota(jnp.int32, (RT, LANES), 1) < HEAD_DIM
    zero_blk = jnp.zeros((RT, LANES), F32)
    for h in range(N_HEADS):
        g = h // GROUP
        src = q[:, (h // 2) * LANES:(h // 2 + 1) * LANES]
        if h % 2 != g % 2:
            src = _swap_heads(src)
        piece = jnp.where(low, src, 0.0) if g % 2 == 0 else jnp.where(low, 0.0, src)
        full = [zero_blk, zero_blk]
        full[g // 2] = piece
        qbig[h] = jnp.concatenate(full, axis=1).astype(BF16)

    row_h = lax.broadcasted_iota(jnp.int32, (SCORE_ROWS, GB * WINDOW), 0)
    col_h = lax.broadcasted_iota(jnp.int32, (SCORE_ROWS, GB * WINDOW), 1)
    same_h = ((row_h >> 2) & (GB - 1)) == (col_h >> 7)
    bias_h = jnp.where(same_h, jnp.where((col_h & (WINDOW - 1)) > (row_h & (DEC_SEQ - 1)), 0.0, NEG_INF),
                       NEG_INF)
    row_n = lax.broadcasted_iota(jnp.int32, (SCORE_ROWS, GR), 0)
    col_n = lax.broadcasted_iota(jnp.int32, (SCORE_ROWS, GR), 1)
    same_n = ((row_n >> 2) & (GB - 1)) == (col_n >> 2)
    bias_n = jnp.where(same_n, jnp.where((col_n & (DEC_SEQ - 1)) <= (row_n & (DEC_SEQ - 1)), 0.0, NEG_INF),
                       NEG_INF)
    sink = sink_ref[:, 0:1]
    low_g = lax.broadcasted_iota(jnp.int32, (GR, LANES), 1) < HEAD_DIM
    row8 = lax.broadcasted_iota(jnp.int32, (SUBLANES, D_KV), 0)

    def roll_in(hist, new_tile):
        rolled = pltpu.roll(hist, WINDOW - DEC_SEQ, 0)
        tail = jnp.where(row8 >= SUBLANES - DEC_SEQ, new_tile, rolled[WINDOW - SUBLANES:])
        return jnp.concatenate([rolled[:WINDOW - SUBLANES], tail], axis=0)

    for grp in range(BT // GB):
        r0 = grp * GR
        lhs = qbig[:, r0:r0 + GR, :].reshape(SCORE_ROWS, D_KV)
        kh = ck_ref[grp * GB:(grp + 1) * GB]
        vh = cv_ref[grp * GB:(grp + 1) * GB]
        kn = knew_f[r0:r0 + GR, :]
        vn_new = vnew_f[r0:r0 + GR, :]
        s_h = _dot_nt(lhs, kh.reshape(GB * WINDOW, D_KV).astype(BF16)) + bias_h
        s_n = _dot_nt(lhs, kn.astype(BF16)) + bias_n
        m = jnp.maximum(jnp.maximum(jnp.max(s_h, axis=1, keepdims=True),
                                    jnp.max(s_n, axis=1, keepdims=True)), sink)
        p_h = jnp.exp(s_h - m)
        p_n = jnp.exp(s_n - m)
        den = (jnp.sum(p_h, axis=1, keepdims=True) + jnp.sum(p_n, axis=1, keepdims=True)
               + jnp.exp(sink - m))
        inv = 1.0 / den
        o = (_dot((p_h * inv).astype(BF16), vh.reshape(GB * WINDOW, D_KV).astype(BF16))
             + _dot((p_n * inv).astype(BF16), vn_new.astype(BF16)))
        for c in range(N_HEADS // 2):
            g = c // 2
            a = o[(2 * c) * GR:(2 * c + 1) * GR, (g // 2) * LANES:(g // 2 + 1) * LANES]
            b = o[(2 * c + 1) * GR:(2 * c + 2) * GR, (g // 2) * LANES:(g // 2 + 1) * LANES]
            blk = jnp.where(low_g, a, _swap_heads(b)) if g % 2 == 0 else jnp.where(low_g, _swap_heads(a), b)
            gate = _silu(z_ref[r0:r0 + GR, C_GA + c * LANES:C_GA + (c + 1) * LANES])
            mix_ref[r0:r0 + GR, c * LANES:(c + 1) * LANES] = blk * gate
        for bb in range(GB):
            k_tile = kn[(bb // 2) * SUBLANES:(bb // 2 + 1) * SUBLANES]
            v_tile = vn_new[(bb // 2) * SUBLANES:(bb // 2 + 1) * SUBLANES]
            if bb % 2 == 0:
                k_tile = pltpu.roll(k_tile, SUBLANES - DEC_SEQ, 0)
                v_tile = pltpu.roll(v_tile, SUBLANES - DEC_SEQ, 0)
            nk_ref[grp * GB + bb] = roll_in(kh[bb], k_tile)
            nv_ref[grp * GB + bb] = roll_in(vh[bb], v_tile)

    for c in range(N_REST_BLOCKS):
        zc[c] = z_ref[:, C_XB + c * LANES:C_XB + (c + 1) * LANES]
    for c in range(D_POOL // LANES):
        sc[c] = st_ref[:, c * LANES:(c + 1) * LANES]

    def tok(t, lo, width):
        blk0 = (lo - C_XB) // LANES
        parts = [zc[blk0 + c, pl.ds(t, BT, stride=DEC_SEQ), :] for c in range(width // LANES)]
        return parts[0] if len(parts) == 1 else jnp.concatenate(parts, axis=1)

    def ext(i, gi):
        if i < POOL_HIST:
            return sc[gi, pl.ds(i, BT, stride=POOL_HIST), :]
        return tok(i - POOL_HIST, C_XB + gi * LANES, LANES)

    for gi, w in enumerate(POOL_WINDOWS):
        for s in range(POOL_HIST):
            npc[gi, pl.ds(s, BT, stride=POOL_HIST), :] = ext(s + DEC_SEQ, gi)
        pooled = []
        for t in range(DEC_SEQ):
            cur = ext(POOL_HIST + t, gi)
            acc = cur
            for i in range(1, w):
                acc = acc + ext(POOL_HIST + t - i, gi)
            cnt = float(min(w, PAST_LEN + t + 1))
            pooled.append(acc / cnt - cur)
        pooled = jnp.concatenate(pooled, axis=0).astype(BF16)
        mixed = _dot(pooled, pw_ref[gi]) * ps_ref[:, gi * LANES:(gi + 1) * LANES]
        for t in range(DEC_SEQ):
            gate = _silu(tok(t, C_GB + gi * LANES, LANES))
            mo[gi, pl.ds(t, BT, stride=DEC_SEQ), :] = mixed[t * BT:(t + 1) * BT] * gate

    vns = []
    for t in range(DEC_SEQ):
        vn_t = _layernorm(tok(t, C_VS, D_SGU), lng_ref[...], lnb_ref[...])
        for c in range(D_SGU // LANES):
            vo[c, pl.ds(t, BT, stride=DEC_SEQ), :] = vn_t[:, c * LANES:(c + 1) * LANES]
        vns.append(vn_t)
    for t in range(DEC_SEQ):
        mixed = bexp_ref[t:t + 1, :]
        for s in range(t + 1):
            mixed = mixed + wexp_ref[DEC_SEQ * t + s:DEC_SEQ * t + s + 1, :] * vns[s]
        out = tok(t, C_U, D_SGU) * mixed * _silu(tok(t, C_GC, D_SGU))
        for c in range(D_SGU // LANES):
            mo[D_POOL // LANES + c, pl.ds(t, BT, stride=DEC_SEQ), :] = out[:, c * LANES:(c + 1) * LANES]

    for c in range(D_POOL // LANES):
        npool_ref[:, c * LANES:(c + 1) * LANES] = npc[c]
    for c in range((D_POOL + D_SGU) // LANES):
        mix_ref[:, D_ATT + c * LANES:D_ATT + (c + 1) * LANES] = mo[c]
    for c in range(D_SGU // LANES):
        vn_ref[:, c * LANES:(c + 1) * LANES] = vo[c]


def _sample_mix(layer, z, cos_t, sin_t, params, ck, cv, state2, prev_outs):
    (sink_rows, pw_b, ps, lng, lnb, wexp, bexp) = params
    n_alias = len(prev_outs)
    const = lambda shape: pl.BlockSpec(shape, lambda i: (0,) * len(shape))
    in_specs = [
        pl.BlockSpec((RT, D_IN), lambda i: (i, 0)),
        const((RT, LANES)), const((RT, LANES)),
        _layer_spec((SCORE_ROWS, LANES), layer, 1),
        pl.BlockSpec((None, BT, WINDOW, D_KV), lambda i: (layer, i, 0, 0)),
        pl.BlockSpec((None, BT, WINDOW, D_KV), lambda i: (layer, i, 0, 0)),
        pl.BlockSpec((None, BT * POOL_HIST, D_POOL), lambda i: (layer, i, 0)),
        _layer_spec((len(POOL_WINDOWS), LANES, LANES), layer, 1),
        _layer_spec((1, D_POOL), layer, 1),
        _layer_spec((1, D_SGU), layer, 1),
        _layer_spec((1, D_SGU), layer, 1),
        _layer_spec((DEC_SEQ * DEC_SEQ, D_SGU), layer, 1),
        _layer_spec((DEC_SEQ, D_SGU), layer, 1),
    ] + [pl.BlockSpec(memory_space=pl.ANY)] * n_alias
    out_specs = [
        pl.BlockSpec((RT, D_MODEL), lambda i: (i, 0)),
        pl.BlockSpec((None, BT, WINDOW, D_KV), lambda i: (layer, i, 0, 0)),
        pl.BlockSpec((None, BT, WINDOW, D_KV), lambda i: (layer, i, 0, 0)),
        pl.BlockSpec((None, BT * POOL_HIST, D_POOL), lambda i: (layer, i, 0)),
        pl.BlockSpec((None, RT, D_SGU), lambda i: (layer, i, 0)),
    ]
    out_shape = [
        jax.ShapeDtypeStruct((SAMPLE_ROWS, D_MODEL), F32),
        jax.ShapeDtypeStruct((DEPTH, DEC_BATCH, WINDOW, D_KV), F32),
        jax.ShapeDtypeStruct((DEPTH, DEC_BATCH, WINDOW, D_KV), F32),
        jax.ShapeDtypeStruct((DEPTH, DEC_BATCH * POOL_HIST, D_POOL), F32),
        jax.ShapeDtypeStruct((DEPTH, SAMPLE_ROWS, D_SGU), F32),
    ]
    scratch = [
        pltpu.VMEM((N_HEADS, RT, D_KV), BF16),
        pltpu.VMEM((RT, D_KV), F32),
        pltpu.VMEM((RT, D_KV), F32),
        pltpu.VMEM((N_REST_BLOCKS, RT, LANES), F32),
        pltpu.VMEM((D_POOL // LANES, BT * POOL_HIST, LANES), F32),
        pltpu.VMEM((D_POOL // LANES, BT * POOL_HIST, LANES), F32),
        pltpu.VMEM(((D_POOL + D_SGU) // LANES, RT, LANES), F32),
        pltpu.VMEM((D_SGU // LANES, RT, LANES), F32),
    ]
    n_in = len(in_specs) - n_alias
    return pl.pallas_call(
        functools.partial(_sample_mix_body, n_alias),
        grid=(DEC_BATCH // BT,),
        in_specs=in_specs,
        out_specs=out_specs,
        out_shape=out_shape,
        scratch_shapes=scratch,
        input_output_aliases={n_in + i: 1 + i for i in range(n_alias)},
        compiler_params=pltpu.CompilerParams(
            dimension_semantics=("arbitrary",), vmem_limit_bytes=VMEM_LIMIT_SAMPLE),
        name="sample_mix",
    )(z, cos_t, sin_t, sink_rows, ck, cv, state2, pw_b, ps, lng, lnb, wexp, bexp, *prev_outs)


def _sample_out_body(x_ref, mix_ref, w_out_ref, npost_ref, y_ref):
    y = _dot(mix_ref[...].astype(BF16), w_out_ref[...])
    y_ref[...] = x_ref[...] + _rms(y, npost_ref[...])


def _sample_out(layer, xs, mix, w_out_b, npost):
    full = lambda shape: pl.BlockSpec(shape, lambda i: (0,) * len(shape))
    return pl.pallas_call(
        _sample_out_body,
        grid=(1,),
        in_specs=[
            full((SAMPLE_ROWS, D_MODEL)),
            full((SAMPLE_ROWS, D_MODEL)),
            pl.BlockSpec((None, D_MODEL, D_MODEL), lambda i: (layer, 0, 0)),
            pl.BlockSpec((None, 1, D_MODEL), lambda i: (layer, 0, 0)),
        ],
        out_specs=full((SAMPLE_ROWS, D_MODEL)),
        out_shape=jax.ShapeDtypeStruct((SAMPLE_ROWS, D_MODEL), F32),
        compiler_params=pltpu.CompilerParams(
            dimension_semantics=("arbitrary",), vmem_limit_bytes=VMEM_LIMIT_SAMPLE),
        name="sample_out",
    )(xs, mix, w_out_b, npost)


def _rope_tables(pos):
    inv = ROPE_THETA ** (-jnp.arange(0, HEAD_DIM, 2, dtype=F32) / HEAD_DIM)
    ang = pos.astype(F32)[:, None] * inv[None, :]
    c, s = jnp.cos(ang), jnp.sin(ang)
    return jnp.concatenate([c, c, c, c], axis=1), jnp.concatenate([-s, s, -s, s], axis=1)


def kernel(x_prompt, x_sample, cache_k, cache_v, state_pool, w_in, w_out, norm_pre, norm_post,
           attn_sinks, pool_w, pool_scale, sgu_ln_g, sgu_ln_b, sgu_w, sgu_b):
    cos_p, sin_p = _rope_tables(jnp.arange(SEQ))
    cos_s, sin_s = _rope_tables(PAST_LEN + jnp.arange(DEC_SEQ))
    cos_t = jnp.tile(cos_s, (BT, 1))
    sin_t = jnp.tile(sin_s, (BT, 1))

    w_in_b = w_in.astype(BF16)
    w_out_b = w_out.astype(BF16)
    pw_b = pool_w.astype(BF16)
    npre = norm_pre[:, None, :]
    npost = norm_post[:, None, :]
    ps = pool_scale[:, None, :]
    lng = sgu_ln_g[:, None, :]
    lnb = sgu_ln_b[:, None, :]
    sb_full = jnp.broadcast_to(sgu_b[:, :, :, None], (DEPTH, N_SGU_HEADS, CHUNK, LANES))
    wexp = jnp.repeat(sgu_w[:, :, :DEC_SEQ, :DEC_SEQ].transpose(0, 2, 3, 1), LANES, axis=-1
                      ).reshape(DEPTH, DEC_SEQ * DEC_SEQ, D_SGU)
    bexp = jnp.repeat(sgu_b[:, :, :DEC_SEQ].transpose(0, 2, 1), LANES, axis=-1)
    sink_rows = jnp.broadcast_to(attn_sinks[:, :, None, None], (DEPTH, N_HEADS, GR, LANES)
                                 ).reshape(DEPTH, SCORE_ROWS, LANES)

    xp = x_prompt
    xs = x_sample.reshape(SAMPLE_ROWS, D_MODEL)
    ck = cache_k.reshape(DEPTH, DEC_BATCH, WINDOW, D_KV)
    cv = cache_v.reshape(DEPTH, DEC_BATCH, WINDOW, D_KV)
    state2 = state_pool.reshape(DEPTH, DEC_BATCH * POOL_HIST, D_POOL)
    prompt_params = (attn_sinks, w_in_b, w_out_b, npre, npost, pw_b, ps, lng, lnb, sgu_w, sb_full)
    sample_params = (sink_rows, pw_b, ps, lng, lnb, wexp, bexp)
    p_outs, s_outs = [], []
    for layer in range(DEPTH):
        xp, *p_outs = _prompt_layer(layer, xp, cos_p, sin_p, prompt_params, p_outs)
        z = _sample_proj(layer, xs, npre, w_in_b)
        mix, *s_outs = _sample_mix(layer, z, cos_t, sin_t, sample_params, ck, cv, state2, s_outs)
        xs = _sample_out(layer, xs, mix, w_out_b, npost)

    nk_p, nv_p, np_p = p_outs
    nk_s, nv_s, np_s, vn_s = s_outs
    kv5 = lambda a, nb: a.reshape(DEPTH, nb, WINDOW, N_KV, HEAD_DIM)
    return (xp, xs.reshape(DEC_BATCH, DEC_SEQ, D_MODEL),
            kv5(nk_p, BATCH), kv5(nv_p, BATCH), np_p,
            kv5(nk_s, DEC_BATCH), kv5(nv_s, DEC_BATCH),
            np_s.reshape(DEPTH, DEC_BATCH, POOL_HIST, D_POOL),
            vn_s.reshape(DEPTH, DEC_BATCH, DEC_SEQ, D_SGU))
```

```python
import functools

import jax
import jax.numpy as jnp
from jax import lax
from jax.experimental import pallas as pl
from jax.experimental.pallas import tpu as pltpu

D_MODEL = 2048
SEQ = 2048
BATCH = 8
DEPTH = 2
DEC_BATCH = 128
DEC_SEQ = 4
PAST_LEN = 8192
HEAD_DIM = 64
HALF_DIM = HEAD_DIM // 2
N_HEADS = 16
N_KV = 4
GROUP = N_HEADS // N_KV
D_ATT = N_HEADS * HEAD_DIM
D_KV = N_KV * HEAD_DIM
WINDOW = 128
ROPE_THETA = 10000.0
D_POOL = 512
POOL_WINDOWS = (2, 4, 8, 16)
POOL_HIST = 15
D_SGU = 512
CHUNK = 128
N_SGU_HEADS = 4
D_IN = 5120
EPS = 1e-6
ATT_SCALE = HEAD_DIM ** -0.5

C_Q = 0
C_K = C_Q + D_ATT
C_V = C_K + D_KV
C_GA = C_V + D_KV
C_XB = C_GA + D_ATT
C_GB = C_XB + D_POOL
C_U = C_GB + D_POOL
C_VS = C_U + D_SGU
C_GC = C_VS + D_SGU
assert C_GC + D_SGU == D_IN

LANES = 128
SUBLANES = 8
HIST_PAD = 16
TM = 256
N_CHUNKS = TM // CHUNK
SAMPLE_ROWS = DEC_BATCH * DEC_SEQ
BT = 16
RT = BT * DEC_SEQ
GB = 4
GR = GB * DEC_SEQ
ATT_PIECE = 512
REST_PIECE = 256
N_REST_PIECES = (D_IN - C_XB) // REST_PIECE
Z_U = D_POOL
Z_VS = Z_U + D_SGU
Z_WIDTH = Z_VS + D_SGU
S1_TN = 1024
VMEM_LIMIT_PROMPT = 56 * 1024 * 1024
VMEM_LIMIT_SAMPLE = 48 * 1024 * 1024

F32 = jnp.float32
BF16 = jnp.bfloat16
NEG_INF = float("-inf")


def _rms(x, g):
    return x * lax.rsqrt(jnp.mean(x * x, axis=-1, keepdims=True) + EPS) * g


def _silu(x):
    return x * (1.0 / (1.0 + jnp.exp(-x)))


def _layernorm(x, g, b):
    mu = jnp.mean(x, axis=-1, keepdims=True)
    xc = x - mu
    var = jnp.mean(xc * xc, axis=-1, keepdims=True)
    return xc * lax.rsqrt(var + EPS) * g + b


def _swap_heads(x):
    return pltpu.roll(x, HEAD_DIM, 1)


def _rope(x, cos, sin):
    rows = x.shape[0]
    lane = lax.broadcasted_iota(jnp.int32, (rows, LANES), 1)
    first_half = (lane & HALF_DIM) == 0
    outs = []
    for c in range(x.shape[1] // LANES):
        xc = x[:, c * LANES:(c + 1) * LANES]
        partner = jnp.where(first_half,
                            pltpu.roll(xc, LANES - HALF_DIM, 1),
                            pltpu.roll(xc, HALF_DIM, 1))
        outs.append(xc * cos + partner * sin)
    return jnp.concatenate(outs, axis=1)


def _dot(a, b):
    return jnp.dot(a, b, preferred_element_type=F32)


def _dot_nt(a, b):
    return lax.dot_general(a, b, (((1,), (1,)), ((), ())), preferred_element_type=F32)


def _prompt_body(layer, n_alias, sinks_ref, x_ref, cos_ref, sin_ref, w_in_ref, w_out_ref,
                 npre_ref, npost_ref, pw_ref, ps_ref, lng_ref, lnb_ref, sw_ref, sb_ref, *rest):
    y_ref, nk_ref, nv_ref, np_ref, qbuf, kbuf, vbuf, xbext, mixbuf, zrest = rest[n_alias:]
    j = pl.program_id(1)

    @pl.when(j == 0)
    def _start_of_sequence():
        kbuf[:, 0:WINDOW, :] = jnp.zeros((2 * N_KV, WINDOW, LANES), BF16)
        vbuf[:, 0:WINDOW, :] = jnp.zeros((N_KV, WINDOW, D_KV), BF16)
        xbext[0:HIST_PAD, :] = jnp.zeros((HIST_PAD, D_POOL), F32)

    @pl.when(j > 0)
    def _carry_from_previous_step():
        for i in range(2 * N_KV):
            kbuf[i, 0:WINDOW, :] = kbuf[i, TM:TM + WINDOW, :]
        for g in range(N_KV):
            vbuf[g, 0:WINDOW, :] = vbuf[g, TM:TM + WINDOW, :]
        xbext[0:HIST_PAD, :] = xbext[TM:TM + HIST_PAD, :]

    h = _rms(x_ref[0], npre_ref[...]).astype(BF16)

    def proj(lo, width):
        return _dot(h, w_in_ref[:, lo:lo + width])

    cos = cos_ref[...]
    sin = sin_ref[...]

    k = _rope(proj(C_K, D_KV), cos, sin)
    v = proj(C_V, D_KV)
    cos_q = cos * ATT_SCALE
    sin_q = sin * ATT_SCALE
    for part in range(D_ATT // ATT_PIECE):
        cols = slice(part * ATT_PIECE, (part + 1) * ATT_PIECE)
        qbuf[:, cols] = _rope(proj(C_Q + part * ATT_PIECE, ATT_PIECE), cos_q, sin_q).astype(BF16)
    for part in range(D_ATT // ATT_PIECE):
        cols = slice(part * ATT_PIECE, (part + 1) * ATT_PIECE)
        mixbuf[:, cols] = _silu(proj(C_GA + part * ATT_PIECE, ATT_PIECE)).astype(BF16)
    nk_ref[0] = k[TM - WINDOW:TM]
    nv_ref[0] = v[TM - WINDOW:TM]
    lane_k = lax.broadcasted_iota(jnp.int32, (TM, LANES), 1)
    low = lane_k < HEAD_DIM
    lane_v = lax.broadcasted_iota(jnp.int32, (TM, D_KV), 1)
    for pair in range(2):
        kp = k[:, pair * LANES:(pair + 1) * LANES]
        ks = _swap_heads(kp)
        rows = slice(WINDOW, WINDOW + TM)
        kbuf[4 * pair + 0, rows, :] = jnp.where(low, kp, 0.0).astype(BF16)
        kbuf[4 * pair + 1, rows, :] = jnp.where(low, 0.0, ks).astype(BF16)
        kbuf[4 * pair + 2, rows, :] = jnp.where(low, ks, 0.0).astype(BF16)
        kbuf[4 * pair + 3, rows, :] = jnp.where(low, 0.0, kp).astype(BF16)
    for g in range(N_KV):
        keep_v = (lane_v >= g * HEAD_DIM) & (lane_v < (g + 1) * HEAD_DIM)
        vbuf[g, WINDOW:WINDOW + TM, :] = jnp.where(keep_v, v, 0.0).astype(BF16)

    qi = lax.broadcasted_iota(jnp.int32, (CHUNK, 2 * WINDOW), 0)
    kj = lax.broadcasted_iota(jnp.int32, (CHUNK, 2 * WINDOW), 1)
    dist = qi + WINDOW - kj
    band_bias = jnp.where((dist >= 0) & (dist < WINDOW), 0.0, NEG_INF)
    no_prev = jnp.where(j == 0, WINDOW, 0)
    first_bias = jnp.where(kj < no_prev, NEG_INF, band_bias)
    low_c = lax.broadcasted_iota(jnp.int32, (CHUNK, LANES), 1) < HEAD_DIM
    rest_piece = 0

    def project_rest_piece(piece):
        cols = slice(piece * REST_PIECE, (piece + 1) * REST_PIECE)
        lo = C_XB + piece * REST_PIECE
        val = proj(lo, REST_PIECE)
        if C_GB <= lo < C_U:
            mixbuf[:, D_ATT + lo - C_GB:D_ATT + lo - C_GB + REST_PIECE] = _silu(val).astype(BF16)
        elif lo >= C_GC:
            at = D_ATT + D_POOL + lo - C_GC
            mixbuf[:, at:at + REST_PIECE] = _silu(val).astype(BF16)
        else:
            at = lo - C_XB if lo < C_GB else (Z_U + lo - C_U if lo < C_VS else Z_VS + lo - C_VS)
            zrest[:, at:at + REST_PIECE] = val

    def scores(c, g):
        r0 = c * CHUNK
        qs = jnp.concatenate(
            [qbuf[r0:r0 + CHUNK, (2 * g + i) * LANES:(2 * g + i + 1) * LANES] for i in range(2)],
            axis=0)
        return [_dot_nt(qs, kbuf[2 * g + half, r0:r0 + 2 * WINDOW, :]) for half in range(2)]

    order = [(c, g) for c in range(N_CHUNKS) for g in range(N_KV)]
    while rest_piece < N_REST_PIECES - len(order):
        project_rest_piece(rest_piece)
        rest_piece += 1
    s_next = scores(*order[0])
    for step, (c, g) in enumerate(order):
        bias = first_bias if c == 0 else band_bias
        r0 = c * CHUNK
        s_half = s_next
        if step + 1 < len(order):
            s_next = scores(*order[step + 1])
        blocks = []
        for r in range(GROUP):
            sb = s_half[r % 2][(r // 2) * CHUNK:(r // 2 + 1) * CHUNK] + bias
            sink = sinks_ref[layer, GROUP * g + r]
            m = jnp.maximum(jnp.max(sb, axis=1, keepdims=True), sink)
            pe = jnp.exp(sb - m)
            den = jnp.sum(pe, axis=1, keepdims=True) + jnp.exp(sink - m)
            blocks.append((pe * (1.0 / den)).astype(BF16))
        pg = jnp.concatenate(blocks, axis=0)
        og = _dot(pg, vbuf[g, r0:r0 + 2 * WINDOW, :])
        o = og if g == 0 else o + og
        if rest_piece < N_REST_PIECES:
            project_rest_piece(rest_piece)
            rest_piece += 1
        if g < N_KV - 1:
            continue
        for pair in range(2):
            for i in range(2):
                a = o[(2 * i) * CHUNK:(2 * i + 1) * CHUNK, pair * LANES:(pair + 1) * LANES]
                b = o[(2 * i + 1) * CHUNK:(2 * i + 2) * CHUNK, pair * LANES:(pair + 1) * LANES]
                for odd, blk in ((0, jnp.where(low_c, a, _swap_heads(b))),
                                 (1, jnp.where(low_c, _swap_heads(a), b))):
                    cols = slice((2 * (2 * pair + odd) + i) * LANES, (2 * (2 * pair + odd) + i + 1) * LANES)
                    mixbuf[r0:r0 + CHUNK, cols] = (blk * mixbuf[r0:r0 + CHUNK, cols]).astype(BF16)

    while rest_piece < N_REST_PIECES:
        project_rest_piece(rest_piece)
        rest_piece += 1

    xbext[HIST_PAD:HIST_PAD + TM, :] = zrest[:, 0:D_POOL]
    np_ref[0] = xbext[HIST_PAD + TM - POOL_HIST:HIST_PAD + TM, :]
    pos1 = j * TM + lax.broadcasted_iota(jnp.int32, (TM, LANES), 0) + 1
    for gi, w in enumerate(POOL_WINDOWS):
        cols = slice(gi * LANES, (gi + 1) * LANES)
        cur = xbext[HIST_PAD:HIST_PAD + TM, cols]
        acc = cur
        for i in range(1, w):
            acc = acc + xbext[HIST_PAD - i:HIST_PAD - i + TM, cols]
        cnt = jnp.minimum(w, pos1).astype(F32)
        pooled = acc / cnt - cur
        mixed = _dot(pooled.astype(BF16), pw_ref[gi]) * ps_ref[:, cols]
        out_cols = slice(D_ATT + gi * LANES, D_ATT + (gi + 1) * LANES)
        mixbuf[:, out_cols] = (mixed * mixbuf[:, out_cols]).astype(BF16)

    vn = _layernorm(zrest[:, Z_VS:Z_VS + D_SGU], lng_ref[...], lnb_ref[...]).astype(BF16)
    ti = lax.broadcasted_iota(jnp.int32, (CHUNK, CHUNK), 0)
    si = lax.broadcasted_iota(jnp.int32, (CHUNK, CHUNK), 1)
    for hh in range(N_SGU_HEADS):
        cols = slice(hh * LANES, (hh + 1) * LANES)
        wm = jnp.where(ti >= si, sw_ref[hh], 0.0).astype(BF16)
        for c in range(N_CHUNKS):
            rows = slice(c * CHUNK, (c + 1) * CHUNK)
            mixed = _dot(wm, vn[rows, cols]) + sb_ref[hh]
            out_cols = slice(D_ATT + D_POOL + hh * LANES, D_ATT + D_POOL + (hh + 1) * LANES)
            mixbuf[rows, out_cols] = (
                zrest[rows, Z_U + hh * LANES:Z_U + (hh + 1) * LANES] * mixed * mixbuf[rows, out_cols]
            ).astype(BF16)

    y = _dot(mixbuf[...], w_out_ref[...])
    y_ref[0] = x_ref[0] + _rms(y, npost_ref[...])


def _layer_spec(shape, layer, grid_rank):
    zeros = (0,) * len(shape)
    if grid_rank == 2:
        index_map = lambda b, j: (layer,) + zeros
    else:
        index_map = lambda i: (layer,) + zeros
    return pl.BlockSpec((None,) + tuple(shape), index_map, pipeline_mode=pl.Buffered(1))


def _prompt_layer(layer, x, cos, sin, params, prev_outs):
    (sinks, w_in_b, w_out_b, npre, npost, pw_b, ps, lng, lnb, sw, sb_full) = params
    grid = (BATCH, SEQ // TM)
    n_alias = len(prev_outs)
    in_specs = [
        pl.BlockSpec(memory_space=pltpu.SMEM),
        pl.BlockSpec((1, TM, D_MODEL), lambda b, j: (b, j, 0)),
        pl.BlockSpec((TM, LANES), lambda b, j: (j, 0)),
        pl.BlockSpec((TM, LANES), lambda b, j: (j, 0)),
        _layer_spec((D_MODEL, D_IN), layer, 2),
        _layer_spec((D_MODEL, D_MODEL), layer, 2),
        _layer_spec((1, D_MODEL), layer, 2),
        _layer_spec((1, D_MODEL), layer, 2),
        _layer_spec((len(POOL_WINDOWS), LANES, LANES), layer, 2),
        _layer_spec((1, D_POOL), layer, 2),
        _layer_spec((1, D_SGU), layer, 2),
        _layer_spec((1, D_SGU), layer, 2),
        _layer_spec((N_SGU_HEADS, CHUNK, CHUNK), layer, 2),
        _layer_spec((N_SGU_HEADS, CHUNK, LANES), layer, 2),
    ] + [pl.BlockSpec(memory_space=pl.ANY)] * n_alias
    out_specs = [
        pl.BlockSpec((1, TM, D_MODEL), lambda b, j: (b, j, 0)),
        pl.BlockSpec((None, 1, WINDOW, D_KV), lambda b, j: (layer, b, 0, 0)),
        pl.BlockSpec((None, 1, WINDOW, D_KV), lambda b, j: (layer, b, 0, 0)),
        pl.BlockSpec((None, 1, POOL_HIST, D_POOL), lambda b, j: (layer, b, 0, 0)),
    ]
    out_shape = [
        jax.ShapeDtypeStruct((BATCH, SEQ, D_MODEL), F32),
        jax.ShapeDtypeStruct((DEPTH, BATCH, WINDOW, D_KV), F32),
        jax.ShapeDtypeStruct((DEPTH, BATCH, WINDOW, D_KV), F32),
        jax.ShapeDtypeStruct((DEPTH, BATCH, POOL_HIST, D_POOL), F32),
    ]
    scratch = [
        pltpu.VMEM((TM, D_ATT), BF16),
        pltpu.VMEM((2 * N_KV, WINDOW + TM, LANES), BF16),
        pltpu.VMEM((N_KV, WINDOW + TM, D_KV), BF16),
        pltpu.VMEM((HIST_PAD + TM, D_POOL), F32),
        pltpu.VMEM((TM, D_MODEL), BF16),
        pltpu.VMEM((TM, Z_WIDTH), F32),
    ]
    n_in = len(in_specs) - n_alias
    return pl.pallas_call(
        functools.partial(_prompt_body, layer, n_alias),
        grid=grid,
        in_specs=in_specs,
        out_specs=out_specs,
        out_shape=out_shape,
        scratch_shapes=scratch,
        input_output_aliases={n_in + i: 1 + i for i in range(n_alias)},
        compiler_params=pltpu.CompilerParams(
            dimension_semantics=("arbitrary", "arbitrary"),
            vmem_limit_bytes=VMEM_LIMIT_PROMPT),
        name="prompt_layer",
    )(sinks, x, cos, sin, w_in_b, w_out_b, npre, npost, pw_b, ps, lng, lnb, sw, sb_full, *prev_outs)


def _sample_proj_body(x_ref, npre_ref, w_ref, z_ref, hbuf):
    @pl.when(pl.program_id(0) == 0)
    def _norm_once():
        hbuf[...] = _rms(x_ref[...].reshape(SAMPLE_ROWS, D_MODEL), npre_ref[...]).astype(BF16)

    z_ref[...] = _dot(hbuf[...], w_ref[...])


def _sample_proj(layer, xs, npre, w_in_b):
    return pl.pallas_call(
        _sample_proj_body,
        grid=(D_IN // S1_TN,),
        in_specs=[
            pl.BlockSpec(xs.shape, lambda n: (0,) * xs.ndim),
            pl.BlockSpec((None, 1, D_MODEL), lambda n: (layer, 0, 0)),
            pl.BlockSpec((None, D_MODEL, S1_TN), lambda n: (layer, 0, n)),
        ],
        out_specs=pl.BlockSpec((SAMPLE_ROWS, S1_TN), lambda n: (0, n)),
        out_shape=jax.ShapeDtypeStruct((SAMPLE_ROWS, D_IN), F32),
        scratch_shapes=[pltpu.VMEM((SAMPLE_ROWS, D_MODEL), BF16)],
        compiler_params=pltpu.CompilerParams(
            dimension_semantics=("arbitrary",), vmem_limit_bytes=VMEM_LIMIT_SAMPLE),
        name="sample_proj",
    )(xs, npre, w_in_b)


SCORE_ROWS = N_HEADS * GR
N_REST_BLOCKS = (D_IN - C_XB) // LANES


def _sample_mix_body(n_alias, z_ref, cos_ref, sin_ref, sink_ref, ck_ref, cv_ref, st_ref,
                     pw_ref, ps_ref, lng_ref, lnb_ref, wexp_ref, bexp_ref, *rest):
    mix_ref, nk_ref, nv_ref, npool_ref, vn_ref, qbig, knew_f, vnew_f, zc, sc, npc, mo, vo = rest[n_alias:]
    cos = cos_ref[...]
    sin = sin_ref[...]

    q = _rope(z_ref[:, C_Q:C_Q + D_ATT], cos * ATT_SCALE, sin * ATT_SCALE)
    knew_f[...] = _rope(z_ref[:, C_K:C_K + D_KV], cos, sin)
    vnew_f[...] = z_ref[:, C_V:C_V + D_KV]
    low = lax.broadcasted_iota(jnp.int32, (RT, LANES), 1) < HEAD_DIM
    zero_blk = jnp.zeros((RT, LANES), F32)
    for h in range(N_HEADS):
        g = h // GROUP
        src = q[:, (h // 2) * LANES:(h // 2 + 1) * LANES]
        if h % 2 != g % 2:
            src = _swap_heads(src)
        piece = jnp.where(low, src, 0.0) if g % 2 == 0 else jnp.where(low, 0.0, src)
        full = [zero_blk, zero_blk]
        full[g // 2] = piece
        qbig[h] = jnp.concatenate(full, axis=1).astype(BF16)

    row_h = lax.broadcasted_iota(jnp.int32, (SCORE_ROWS, GB * WINDOW), 0)
    col_h = lax.broadcasted_iota(jnp.int32, (SCORE_ROWS, GB * WINDOW), 1)
    same_h = ((row_h >> 2) & (GB - 1)) == (col_h >> 7)
    bias_h = jnp.where(same_h, jnp.where((col_h & (WINDOW - 1)) > (row_h & (DEC_SEQ - 1)), 0.0, NEG_INF),
                       NEG_INF)
    row_n = lax.broadcasted_iota(jnp.int32, (SCORE_ROWS, GR), 0)
    col_n = lax.broadcasted_iota(jnp.int32, (SCORE_ROWS, GR), 1)
    same_n = ((row_n >> 2) & (GB - 1)) == (col_n >> 2)
    bias_n = jnp.where(same_n, jnp.where((col_n & (DEC_SEQ - 1)) <= (row_n & (DEC_SEQ - 1)), 0.0, NEG_INF),
                       NEG_INF)
    sink = sink_ref[:, 0:1]
    low_g = lax.broadcasted_iota(jnp.int32, (GR, LANES), 1) < HEAD_DIM
    row8 = lax.broadcasted_iota(jnp.int32, (SUBLANES, D_KV), 0)

    def roll_in(hist, new_tile):
        rolled = pltpu.roll(hist, WINDOW - DEC_SEQ, 0)
        tail = jnp.where(row8 >= SUBLANES - DEC_SEQ, new_tile, rolled[WINDOW - SUBLANES:])
        return jnp.concatenate([rolled[:WINDOW - SUBLANES], tail], axis=0)

    for grp in range(BT // GB):
        r0 = grp * GR
        lhs = qbig[:, r0:r0 + GR, :].reshape(SCORE_ROWS, D_KV)
        kh = ck_ref[grp * GB:(grp + 1) * GB]
        vh = cv_ref[grp * GB:(grp + 1) * GB]
        kn = knew_f[r0:r0 + GR, :]
        vn_new = vnew_f[r0:r0 + GR, :]
        s_h = _dot_nt(lhs, kh.reshape(GB * WINDOW, D_KV).astype(BF16)) + bias_h
        s_n = _dot_nt(lhs, kn.astype(BF16)) + bias_n
        m = jnp.maximum(jnp.maximum(jnp.max(s_h, axis=1, keepdims=True),
                                    jnp.max(s_n, axis=1, keepdims=True)), sink)
        p_h = jnp.exp(s_h - m)
        p_n = jnp.exp(s_n - m)
        den = (jnp.sum(p_h, axis=1, keepdims=True) + jnp.sum(p_n, axis=1, keepdims=True)
               + jnp.exp(sink - m))
        inv = 1.0 / den
        o = (_dot((p_h * inv).astype(BF16), vh.reshape(GB * WINDOW, D_KV).astype(BF16))
             + _dot((p_n * inv).astype(BF16), vn_new.astype(BF16)))
        for c in range(N_HEADS // 2):
            g = c // 2
            a = o[(2 * c) * GR:(2 * c + 1) * GR, (g // 2) * LANES:(g // 2 + 1) * LANES]
            b = o[(2 * c + 1) * GR:(2 * c + 2) * GR, (g // 2) * LANES:(g // 2 + 1) * LANES]
            blk = jnp.where(low_g, a, _swap_heads(b)) if g % 2 == 0 else jnp.where(low_g, _swap_heads(a), b)
            gate = _silu(z_ref[r0:r0 + GR, C_GA + c * LANES:C_GA + (c + 1) * LANES])
            mix_ref[r0:r0 + GR, c * LANES:(c + 1) * LANES] = blk * gate
        for bb in range(GB):
            k_tile = kn[(bb // 2) * SUBLANES:(bb // 2 + 1) * SUBLANES]
            v_tile = vn_new[(bb // 2) * SUBLANES:(bb // 2 + 1) * SUBLANES]
            if bb % 2 == 0:
                k_tile = pltpu.roll(k_tile, SUBLANES - DEC_SEQ, 0)
                v_tile = pltpu.roll(v_tile, SUBLANES - DEC_SEQ, 0)
            nk_ref[grp * GB + bb] = roll_in(kh[bb], k_tile)
            nv_ref[grp * GB + bb] = roll_in(vh[bb], v_tile)

    for c in range(N_REST_BLOCKS):
        zc[c] = z_ref[:, C_XB + c * LANES:C_XB + (c + 1) * LANES]
    for c in range(D_POOL // LANES):
        sc[c] = st_ref[:, c * LANES:(c + 1) * LANES]

    def tok(t, lo, width):
        blk0 = (lo - C_XB) // LANES
        parts = [zc[blk0 + c, pl.ds(t, BT, stride=DEC_SEQ), :] for c in range(width // LANES)]
        return parts[0] if len(parts) == 1 else jnp.concatenate(parts, axis=1)

    def ext(i, gi):
        if i < POOL_HIST:
            return sc[gi, pl.ds(i, BT, stride=POOL_HIST), :]
        return tok(i - POOL_HIST, C_XB + gi * LANES, LANES)

    for gi, w in enumerate(POOL_WINDOWS):
        for s in range(POOL_HIST):
            npc[gi, pl.ds(s, BT, stride=POOL_HIST), :] = ext(s + DEC_SEQ, gi)
        pooled = []
        for t in range(DEC_SEQ):
            cur = ext(POOL_HIST + t, gi)
            acc = cur
            for i in range(1, w):
                acc = acc + ext(POOL_HIST + t - i, gi)
            cnt = float(min(w, PAST_LEN + t + 1))
            pooled.append(acc / cnt - cur)
        pooled = jnp.concatenate(pooled, axis=0).astype(BF16)
        mixed = _dot(pooled, pw_ref[gi]) * ps_ref[:, gi * LANES:(gi + 1) * LANES]
        for t in range(DEC_SEQ):
            gate = _silu(tok(t, C_GB + gi * LANES, LANES))
            mo[gi, pl.ds(t, BT, stride=DEC_SEQ), :] = mixed[t * BT:(t + 1) * BT] * gate

    vns = []
    for t in range(DEC_SEQ):
        vn_t = _layernorm(tok(t, C_VS, D_SGU), lng_ref[...], lnb_ref[...])
        for c in range(D_SGU // LANES):
            vo[c, pl.ds(t, BT, stride=DEC_SEQ), :] = vn_t[:, c * LANES:(c + 1) * LANES]
        vns.append(vn_t)
    for t in range(DEC_SEQ):
        mixed = bexp_ref[t:t + 1, :]
        for s in range(t + 1):
            mixed = mixed + wexp_ref[DEC_SEQ * t + s:DEC_SEQ * t + s + 1, :] * vns[s]
        out = tok(t, C_U, D_SGU) * mixed * _silu(tok(t, C_GC, D_SGU))
        for c in range(D_SGU // LANES):
            mo[D_POOL // LANES + c, pl.ds(t, BT, stride=DEC_SEQ), :] = out[:, c * LANES:(c + 1) * LANES]

    for c in range(D_POOL // LANES):
        npool_ref[:, c * LANES:(c + 1) * LANES] = npc[c]
    for c in range((D_POOL + D_SGU) // LANES):
        mix_ref[:, D_ATT + c * LANES:D_ATT + (c + 1) * LANES] = mo[c]
    for c in range(D_SGU // LANES):
        vn_ref[:, c * LANES:(c + 1) * LANES] = vo[c]


def _sample_mix(layer, z, cos_t, sin_t, params, ck, cv, state2, prev_outs):
    (sink_rows, pw_b, ps, lng, lnb, wexp, bexp) = params
    n_alias = len(prev_outs)
    const = lambda shape: pl.BlockSpec(shape, lambda i: (0,) * len(shape))
    in_specs = [
        pl.BlockSpec((RT, D_IN), lambda i: (i, 0)),
        const((RT, LANES)), const((RT, LANES)),
        _layer_spec((SCORE_ROWS, LANES), layer, 1),
        pl.BlockSpec((None, BT, WINDOW, D_KV), lambda i: (layer, i, 0, 0)),
        pl.BlockSpec((None, BT, WINDOW, D_KV), lambda i: (layer, i, 0, 0)),
        pl.BlockSpec((None, BT * POOL_HIST, D_POOL), lambda i: (layer, i, 0)),
        _layer_spec((len(POOL_WINDOWS), LANES, LANES), layer, 1),
        _layer_spec((1, D_POOL), layer, 1),
        _layer_spec((1, D_SGU), layer, 1),
        _layer_spec((1, D_SGU), layer, 1),
        _layer_spec((DEC_SEQ * DEC_SEQ, D_SGU), layer, 1),
        _layer_spec((DEC_SEQ, D_SGU), layer, 1),
    ] + [pl.BlockSpec(memory_space=pl.ANY)] * n_alias
    out_specs = [
        pl.BlockSpec((RT, D_MODEL), lambda i: (i, 0)),
        pl.BlockSpec((None, BT, WINDOW, D_KV), lambda i: (layer, i, 0, 0)),
        pl.BlockSpec((None, BT, WINDOW, D_KV), lambda i: (layer, i, 0, 0)),
        pl.BlockSpec((None, BT * POOL_HIST, D_POOL), lambda i: (layer, i, 0)),
        pl.BlockSpec((None, RT, D_SGU), lambda i: (layer, i, 0)),
    ]
    out_shape = [
        jax.ShapeDtypeStruct((SAMPLE_ROWS, D_MODEL), F32),
        jax.ShapeDtypeStruct((DEPTH, DEC_BATCH, WINDOW, D_KV), F32),
        jax.ShapeDtypeStruct((DEPTH, DEC_BATCH, WINDOW, D_KV), F32),
        jax.ShapeDtypeStruct((DEPTH, DEC_BATCH * POOL_HIST, D_POOL), F32),
        jax.ShapeDtypeStruct((DEPTH, SAMPLE_ROWS, D_SGU), F32),
    ]
    scratch = [
        pltpu.VMEM((N_HEADS, RT, D_KV), BF16),
        pltpu.VMEM((RT, D_KV), F32),
        pltpu.VMEM((RT, D_KV), F32),
        pltpu.VMEM((N_REST_BLOCKS, RT, LANES), F32),
        pltpu.VMEM((D_POOL // LANES, BT * POOL_HIST, LANES), F32),
        pltpu.VMEM((D_POOL // LANES, BT * POOL_HIST, LANES), F32),
        pltpu.VMEM(((D_POOL + D_SGU) // LANES, RT, LANES), F32),
        pltpu.VMEM((D_SGU // LANES, RT, LANES), F32),
    ]
    n_in = len(in_specs) - n_alias
    return pl.pallas_call(
        functools.partial(_sample_mix_body, n_alias),
        grid=(DEC_BATCH // BT,),
        in_specs=in_specs,
        out_specs=out_specs,
        out_shape=out_shape,
        scratch_shapes=scratch,
        input_output_aliases={n_in + i: 1 + i for i in range(n_alias)},
        compiler_params=pltpu.CompilerParams(
            dimension_semantics=("arbitrary",), vmem_limit_bytes=VMEM_LIMIT_SAMPLE),
        name="sample_mix",
    )(z, cos_t, sin_t, sink_rows, ck, cv, state2, pw_b, ps, lng, lnb, wexp, bexp, *prev_outs)


def _sample_out_body(x_ref, mix_ref, w_out_ref, npost_ref, y_ref):
    y = _dot(mix_ref[...].astype(BF16), w_out_ref[...])
    out = x_ref[...].reshape(SAMPLE_ROWS, D_MODEL) + _rms(y, npost_ref[...])
    y_ref[...] = out.reshape(y_ref.shape)


def _sample_out(layer, xs, mix, w_out_b, npost, out_shape):
    full = lambda shape: pl.BlockSpec(shape, lambda i: (0,) * len(shape))
    return pl.pallas_call(
        _sample_out_body,
        grid=(1,),
        in_specs=[
            full(xs.shape),
            full((SAMPLE_ROWS, D_MODEL)),
            pl.BlockSpec((None, D_MODEL, D_MODEL), lambda i: (layer, 0, 0)),
            pl.BlockSpec((None, 1, D_MODEL), lambda i: (layer, 0, 0)),
        ],
        out_specs=full(out_shape),
        out_shape=jax.ShapeDtypeStruct(out_shape, F32),
        compiler_params=pltpu.CompilerParams(
            dimension_semantics=("arbitrary",), vmem_limit_bytes=VMEM_LIMIT_SAMPLE),
        name="sample_out",
    )(xs, mix, w_out_b, npost)


def _rope_tables(pos):
    inv = ROPE_THETA ** (-jnp.arange(0, HEAD_DIM, 2, dtype=F32) / HEAD_DIM)
    ang = pos.astype(F32)[:, None] * inv[None, :]
    c, s = jnp.cos(ang), jnp.sin(ang)
    return jnp.concatenate([c, c, c, c], axis=1), jnp.concatenate([-s, s, -s, s], axis=1)


def kernel(x_prompt, x_sample, cache_k, cache_v, state_pool, w_in, w_out, norm_pre, norm_post,
           attn_sinks, pool_w, pool_scale, sgu_ln_g, sgu_ln_b, sgu_w, sgu_b):
    cos_p, sin_p = _rope_tables(jnp.arange(SEQ))
    cos_s, sin_s = _rope_tables(PAST_LEN + jnp.arange(DEC_SEQ))
    cos_t = jnp.tile(cos_s, (BT, 1))
    sin_t = jnp.tile(sin_s, (BT, 1))

    w_in_b = w_in.astype(BF16)
    w_out_b = w_out.astype(BF16)
    pw_b = pool_w.astype(BF16)
    npre = norm_pre[:, None, :]
    npost = norm_post[:, None, :]
    ps = pool_scale[:, None, :]
    lng = sgu_ln_g[:, None, :]
    lnb = sgu_ln_b[:, None, :]
    sb_full = jnp.broadcast_to(sgu_b[:, :, :, None], (DEPTH, N_SGU_HEADS, CHUNK, LANES))
    wexp = jnp.repeat(sgu_w[:, :, :DEC_SEQ, :DEC_SEQ].transpose(0, 2, 3, 1), LANES, axis=-1
                      ).reshape(DEPTH, DEC_SEQ * DEC_SEQ, D_SGU)
    bexp = jnp.repeat(sgu_b[:, :, :DEC_SEQ].transpose(0, 2, 1), LANES, axis=-1)
    sink_rows = jnp.broadcast_to(attn_sinks[:, :, None, None], (DEPTH, N_HEADS, GR, LANES)
                                 ).reshape(DEPTH, SCORE_ROWS, LANES)

    xp = x_prompt
    xs = x_sample
    ck = cache_k.reshape(DEPTH, DEC_BATCH, WINDOW, D_KV)
    cv = cache_v.reshape(DEPTH, DEC_BATCH, WINDOW, D_KV)
    state2 = state_pool.reshape(DEPTH, DEC_BATCH * POOL_HIST, D_POOL)
    prompt_params = (attn_sinks, w_in_b, w_out_b, npre, npost, pw_b, ps, lng, lnb, sgu_w, sb_full)
    sample_params = (sink_rows, pw_b, ps, lng, lnb, wexp, bexp)
    p_outs, s_outs = [], []
    for layer in range(DEPTH):
        xp, *p_outs = _prompt_layer(layer, xp, cos_p, sin_p, prompt_params, p_outs)
        z = _sample_proj(layer, xs, npre, w_in_b)
        mix, *s_outs = _sample_mix(layer, z, cos_t, sin_t, sample_params, ck, cv, state2, s_outs)
        xs_shape = x_sample.shape if layer == DEPTH - 1 else (SAMPLE_ROWS, D_MODEL)
        xs = _sample_out(layer, xs, mix, w_out_b, npost, xs_shape)

    nk_p, nv_p, np_p = p_outs
    nk_s, nv_s, np_s, vn_s = s_outs
    kv5 = lambda a, nb: a.reshape(DEPTH, nb, WINDOW, N_KV, HEAD_DIM)
    return (xp, xs,
            kv5(nk_p, BATCH), kv5(nv_p, BATCH), np_p,
            kv5(nk_s, DEC_BATCH), kv5(nv_s, DEC_BATCH),
            np_s.reshape(DEPTH, DEC_BATCH, POOL_HIST, D_POOL),
            vn_s.reshape(DEPTH, DEC_BATCH, DEC_SEQ, D_SGU))
```

```python
import functools

import jax
import jax.numpy as jnp
from jax import lax
from jax.experimental import pallas as pl
from jax.experimental.pallas import tpu as pltpu

D_MODEL = 2048
SEQ = 2048
BATCH = 8
DEPTH = 2
DEC_BATCH = 128
DEC_SEQ = 4
PAST_LEN = 8192
HEAD_DIM = 64
HALF_DIM = HEAD_DIM // 2
N_HEADS = 16
N_KV = 4
GROUP = N_HEADS // N_KV
D_ATT = N_HEADS * HEAD_DIM
D_KV = N_KV * HEAD_DIM
WINDOW = 128
ROPE_THETA = 10000.0
D_POOL = 512
POOL_WINDOWS = (2, 4, 8, 16)
POOL_HIST = 15
D_SGU = 512
CHUNK = 128
N_SGU_HEADS = 4
D_IN = 5120
EPS = 1e-6
ATT_SCALE = HEAD_DIM ** -0.5
LOG2_E = 1.4426950408889634

C_Q = 0
C_K = C_Q + D_ATT
C_V = C_K + D_KV
C_GA = C_V + D_KV
C_XB = C_GA + D_ATT
C_GB = C_XB + D_POOL
C_U = C_GB + D_POOL
C_VS = C_U + D_SGU
C_GC = C_VS + D_SGU
assert C_GC + D_SGU == D_IN

LANES = 128
SUBLANES = 8
HIST_PAD = 16
SUB = 256
N_SUB = 2
TM = SUB * N_SUB
N_CHUNKS = SUB // CHUNK
SAMPLE_ROWS = DEC_BATCH * DEC_SEQ
BT = 16
RT = BT * DEC_SEQ
GB = 4
GR = GB * DEC_SEQ
ATT_PIECE = 512
REST_PIECE = 256
N_REST_PIECES = (D_IN - C_XB) // REST_PIECE
Z_U = D_POOL
Z_VS = Z_U + D_SGU
Z_WIDTH = Z_VS + D_SGU
S1_TN = 1024
VMEM_LIMIT_PROMPT = 62 * 1024 * 1024
VMEM_LIMIT_SAMPLE = 48 * 1024 * 1024

F32 = jnp.float32
BF16 = jnp.bfloat16
NEG_INF = float("-inf")


def _rms(x, g):
    return x * lax.rsqrt(jnp.mean(x * x, axis=-1, keepdims=True) + EPS) * g


def _silu(x):
    return x * (1.0 / (1.0 + jnp.exp(-x)))


def _layernorm(x, g, b):
    mu = jnp.mean(x, axis=-1, keepdims=True)
    xc = x - mu
    var = jnp.mean(xc * xc, axis=-1, keepdims=True)
    return xc * lax.rsqrt(var + EPS) * g + b


def _swap_heads(x):
    return pltpu.roll(x, HEAD_DIM, 1)


def _rope(x, cos, sin):
    rows = x.shape[0]
    lane = lax.broadcasted_iota(jnp.int32, (rows, LANES), 1)
    first_half = (lane & HALF_DIM) == 0
    outs = []
    for c in range(x.shape[1] // LANES):
        xc = x[:, c * LANES:(c + 1) * LANES]
        partner = jnp.where(first_half,
                            pltpu.roll(xc, LANES - HALF_DIM, 1),
                            pltpu.roll(xc, HALF_DIM, 1))
        outs.append(xc * cos + partner * sin)
    return jnp.concatenate(outs, axis=1)


def _dot(a, b):
    return jnp.dot(a, b, preferred_element_type=F32)


def _dot_nt(a, b):
    return lax.dot_general(a, b, (((1,), (1,)), ((), ())), preferred_element_type=F32)


def _prompt_body(layer, n_alias, sinks_ref, x_ref, cos_ref, sin_ref, w_in_ref, w_out_ref,
                 npre_ref, npost_ref, pw_ref, ps_ref, lng_ref, lnb_ref, sw_ref, sb_ref, *rest):
    refs = rest[n_alias:]
    kbuf, vbuf, xbext = refs[5], refs[6], refs[7]
    j = pl.program_id(1)

    @pl.when(j == 0)
    def _start_of_sequence():
        kbuf[:, 0:WINDOW, :] = jnp.zeros((2 * N_KV, WINDOW, LANES), BF16)
        vbuf[:, 0:WINDOW, :] = jnp.zeros((N_KV, WINDOW, D_KV), BF16)
        xbext[0:HIST_PAD, :] = jnp.zeros((HIST_PAD, D_POOL), F32)

    @pl.when(j > 0)
    def _carry_from_previous_step():
        for i in range(2 * N_KV):
            kbuf[i, 0:WINDOW, :] = kbuf[i, TM:TM + WINDOW, :]
        for g in range(N_KV):
            vbuf[g, 0:WINDOW, :] = vbuf[g, TM:TM + WINDOW, :]
        xbext[0:HIST_PAD, :] = xbext[TM:TM + HIST_PAD, :]

    for sub in range(N_SUB):
        _prompt_sub_block(layer, sub, j, sinks_ref, x_ref, cos_ref, sin_ref, w_in_ref, w_out_ref,
                          npre_ref, npost_ref, pw_ref, ps_ref, lng_ref, lnb_ref, sw_ref, sb_ref, *refs)


def _prompt_sub_block(layer, sub, j, sinks_ref, x_ref, cos_ref, sin_ref, w_in_ref, w_out_ref,
                      npre_ref, npost_ref, pw_ref, ps_ref, lng_ref, lnb_ref, sw_ref, sb_ref,
                      y_ref, nk_ref, nv_ref, np_ref, qbuf, kbuf, vbuf, xbext, mixbuf, zrest, ybuf):
    base = sub * SUB
    sub_rows = slice(base, base + SUB)
    first_in_sequence = (j == 0) if sub == 0 else None
    h = _rms(x_ref[0, sub_rows], npre_ref[...]).astype(BF16)

    def proj(lo, width):
        return _dot(h, w_in_ref[:, lo:lo + width])

    cos = cos_ref[sub_rows, :]
    sin = sin_ref[sub_rows, :]

    k = _rope(proj(C_K, D_KV), cos, sin)
    v = proj(C_V, D_KV)
    cos_q = cos * (ATT_SCALE * LOG2_E)
    sin_q = sin * (ATT_SCALE * LOG2_E)
    for part in range(D_ATT // ATT_PIECE):
        cols = slice(part * ATT_PIECE, (part + 1) * ATT_PIECE)
        qbuf[:, cols] = _rope(proj(C_Q + part * ATT_PIECE, ATT_PIECE), cos_q, sin_q).astype(BF16)
    for part in range(D_ATT // ATT_PIECE):
        cols = slice(part * ATT_PIECE, (part + 1) * ATT_PIECE)
        mixbuf[:, cols] = _silu(proj(C_GA + part * ATT_PIECE, ATT_PIECE)).astype(BF16)
    if sub == N_SUB - 1:
        nk_ref[0] = k[SUB - WINDOW:SUB]
        nv_ref[0] = v[SUB - WINDOW:SUB]
    lane_k = lax.broadcasted_iota(jnp.int32, (SUB, LANES), 1)
    low = lane_k < HEAD_DIM
    lane_v = lax.broadcasted_iota(jnp.int32, (SUB, D_KV), 1)
    def store_by_chunk(buf, idx, val):
        val = val.astype(BF16)
        for c in range(N_CHUNKS):
            r0 = WINDOW + base + c * CHUNK
            buf[idx, r0:r0 + CHUNK, :] = val[c * CHUNK:(c + 1) * CHUNK]

    for pair in range(2):
        kp = k[:, pair * LANES:(pair + 1) * LANES]
        ks = _swap_heads(kp)
        store_by_chunk(kbuf, 4 * pair + 0, jnp.where(low, kp, 0.0))
        store_by_chunk(kbuf, 4 * pair + 1, jnp.where(low, 0.0, ks))
        store_by_chunk(kbuf, 4 * pair + 2, jnp.where(low, ks, 0.0))
        store_by_chunk(kbuf, 4 * pair + 3, jnp.where(low, 0.0, kp))
    for g in range(N_KV):
        keep_v = (lane_v >= g * HEAD_DIM) & (lane_v < (g + 1) * HEAD_DIM)
        store_by_chunk(vbuf, g, jnp.where(keep_v, v, 0.0))

    qi = lax.broadcasted_iota(jnp.int32, (CHUNK, 2 * WINDOW), 0)
    kj = lax.broadcasted_iota(jnp.int32, (CHUNK, 2 * WINDOW), 1)
    dist = qi + WINDOW - kj
    band_bias = jnp.where((dist >= 0) & (dist < WINDOW), 0.0, NEG_INF)
    if first_in_sequence is None:
        first_bias = band_bias
    else:
        no_prev = jnp.where(first_in_sequence, WINDOW, 0)
        first_bias = jnp.where(kj < no_prev, NEG_INF, band_bias)
    low_c = lax.broadcasted_iota(jnp.int32, (CHUNK, LANES), 1) < HEAD_DIM
    rest_piece = 0

    def project_rest_piece(piece):
        cols = slice(piece * REST_PIECE, (piece + 1) * REST_PIECE)
        lo = C_XB + piece * REST_PIECE
        val = proj(lo, REST_PIECE)
        if C_GB <= lo < C_U:
            mixbuf[:, D_ATT + lo - C_GB:D_ATT + lo - C_GB + REST_PIECE] = _silu(val).astype(BF16)
        elif lo >= C_GC:
            at = D_ATT + D_POOL + lo - C_GC
            mixbuf[:, at:at + REST_PIECE] = _silu(val).astype(BF16)
        else:
            at = lo - C_XB if lo < C_GB else (Z_U + lo - C_U if lo < C_VS else Z_VS + lo - C_VS)
            zrest[:, at:at + REST_PIECE] = val

    def scores(c, g):
        r0 = c * CHUNK
        qs = jnp.concatenate(
            [qbuf[r0:r0 + CHUNK, (2 * g + i) * LANES:(2 * g + i + 1) * LANES] for i in range(2)],
            axis=0)
        return [_dot_nt(qs, kbuf[2 * g + half, base + r0:base + r0 + 2 * WINDOW, :]) for half in range(2)]

    order = [(c, g) for c in range(N_CHUNKS) for g in range(N_KV)]
    while rest_piece < N_REST_PIECES - len(order):
        project_rest_piece(rest_piece)
        rest_piece += 1
    s_next = scores(*order[0])
    for step, (c, g) in enumerate(order):
        bias = first_bias if c == 0 else band_bias
        r0 = c * CHUNK
        s_half = s_next
        if step + 1 < len(order):
            s_next = scores(*order[step + 1])
        blocks = []
        for r in range(GROUP):
            sb = s_half[r % 2][(r // 2) * CHUNK:(r // 2 + 1) * CHUNK] + bias
            sink = sinks_ref[layer, GROUP * g + r] * LOG2_E
            m = jnp.maximum(jnp.max(sb, axis=1, keepdims=True), sink)
            pe = jnp.exp2(sb - m)
            den = jnp.sum(pe, axis=1, keepdims=True) + jnp.exp2(sink - m)
            blocks.append((pe * (1.0 / den)).astype(BF16))
        pg = jnp.concatenate(blocks, axis=0)
        og = _dot(pg, vbuf[g, base + r0:base + r0 + 2 * WINDOW, :])
        o = og if g == 0 else o + og
        if rest_piece < N_REST_PIECES:
            project_rest_piece(rest_piece)
            rest_piece += 1
        if g < N_KV - 1:
            continue
        for pair in range(2):
            for i in range(2):
                a = o[(2 * i) * CHUNK:(2 * i + 1) * CHUNK, pair * LANES:(pair + 1) * LANES]
                b = o[(2 * i + 1) * CHUNK:(2 * i + 2) * CHUNK, pair * LANES:(pair + 1) * LANES]
                for odd, blk in ((0, jnp.where(low_c, a, _swap_heads(b))),
                                 (1, jnp.where(low_c, _swap_heads(a), b))):
                    cols = slice((2 * (2 * pair + odd) + i) * LANES, (2 * (2 * pair + odd) + i + 1) * LANES)
                    mixbuf[r0:r0 + CHUNK, cols] = (blk * mixbuf[r0:r0 + CHUNK, cols]).astype(BF16)

    while rest_piece < N_REST_PIECES:
        project_rest_piece(rest_piece)
        rest_piece += 1

    xb0 = HIST_PAD + base
    xbext[xb0:xb0 + SUB, :] = zrest[:, 0:D_POOL]
    if sub == N_SUB - 1:
        np_ref[0] = xbext[HIST_PAD + TM - POOL_HIST:HIST_PAD + TM, :]
    pos1 = j * TM + base + lax.broadcasted_iota(jnp.int32, (SUB, LANES), 0) + 1
    for gi, w in enumerate(POOL_WINDOWS):
        cols = slice(gi * LANES, (gi + 1) * LANES)
        cur = xbext[xb0:xb0 + SUB, cols]
        acc = cur
        for i in range(1, w):
            acc = acc + xbext[xb0 - i:xb0 - i + SUB, cols]
        cnt = jnp.minimum(w, pos1).astype(F32)
        pooled = acc / cnt - cur
        mixed = _dot(pooled.astype(BF16), pw_ref[gi]) * ps_ref[:, cols]
        out_cols = slice(D_ATT + gi * LANES, D_ATT + (gi + 1) * LANES)
        mixbuf[:, out_cols] = (mixed * mixbuf[:, out_cols]).astype(BF16)

    vn = _layernorm(zrest[:, Z_VS:Z_VS + D_SGU], lng_ref[...], lnb_ref[...]).astype(BF16)
    ti = lax.broadcasted_iota(jnp.int32, (CHUNK, CHUNK), 0)
    si = lax.broadcasted_iota(jnp.int32, (CHUNK, CHUNK), 1)
    for hh in range(N_SGU_HEADS):
        cols = slice(hh * LANES, (hh + 1) * LANES)
        wm = jnp.where(ti >= si, sw_ref[hh], 0.0).astype(BF16)
        for c in range(N_CHUNKS):
            rows = slice(c * CHUNK, (c + 1) * CHUNK)
            mixed = _dot(wm, vn[rows, cols]) + sb_ref[hh]
            out_cols = slice(D_ATT + D_POOL + hh * LANES, D_ATT + D_POOL + (hh + 1) * LANES)
            mixbuf[rows, out_cols] = (
                zrest[rows, Z_U + hh * LANES:Z_U + (hh + 1) * LANES] * mixed * mixbuf[rows, out_cols]
            ).astype(BF16)

    slot = jnp.minimum(j, 0)
    ybuf[slot] = _dot(mixbuf[...], w_out_ref[...])
    y_ref[0, sub_rows] = x_ref[0, sub_rows] + _rms(ybuf[slot], npost_ref[...])


def _layer_spec(shape, layer, grid_rank):
    zeros = (0,) * len(shape)
    if grid_rank == 2:
        index_map = lambda b, j: (layer,) + zeros
    else:
        index_map = lambda i: (layer,) + zeros
    return pl.BlockSpec((None,) + tuple(shape), index_map, pipeline_mode=pl.Buffered(1))


def _prompt_layer(layer, x, cos, sin, params, prev_outs):
    (sinks, w_in_b, w_out_b, npre, npost, pw_b, ps, lng, lnb, sw, sb_full) = params
    grid = (BATCH, SEQ // TM)
    n_alias = len(prev_outs)
    in_specs = [
        pl.BlockSpec(memory_space=pltpu.SMEM),
        pl.BlockSpec((1, TM, D_MODEL), lambda b, j: (b, j, 0)),
        pl.BlockSpec((TM, LANES), lambda b, j: (j, 0)),
        pl.BlockSpec((TM, LANES), lambda b, j: (j, 0)),
        _layer_spec((D_MODEL, D_IN), layer, 2),
        _layer_spec((D_MODEL, D_MODEL), layer, 2),
        _layer_spec((1, D_MODEL), layer, 2),
        _layer_spec((1, D_MODEL), layer, 2),
        _layer_spec((len(POOL_WINDOWS), LANES, LANES), layer, 2),
        _layer_spec((1, D_POOL), layer, 2),
        _layer_spec((1, D_SGU), layer, 2),
        _layer_spec((1, D_SGU), layer, 2),
        _layer_spec((N_SGU_HEADS, CHUNK, CHUNK), layer, 2),
        _layer_spec((N_SGU_HEADS, CHUNK, LANES), layer, 2),
    ] + [pl.BlockSpec(memory_space=pl.ANY)] * n_alias
    out_specs = [
        pl.BlockSpec((1, TM, D_MODEL), lambda b, j: (b, j, 0)),
        pl.BlockSpec((None, 1, WINDOW, D_KV), lambda b, j: (layer, b, 0, 0)),
        pl.BlockSpec((None, 1, WINDOW, D_KV), lambda b, j: (layer, b, 0, 0)),
        pl.BlockSpec((None, 1, POOL_HIST, D_POOL), lambda b, j: (layer, b, 0, 0)),
    ]
    out_shape = [
        jax.ShapeDtypeStruct((BATCH, SEQ, D_MODEL), F32),
        jax.ShapeDtypeStruct((DEPTH, BATCH, WINDOW, D_KV), F32),
        jax.ShapeDtypeStruct((DEPTH, BATCH, WINDOW, D_KV), F32),
        jax.ShapeDtypeStruct((DEPTH, BATCH, POOL_HIST, D_POOL), F32),
    ]
    scratch = [
        pltpu.VMEM((SUB, D_ATT), BF16),
        pltpu.VMEM((2 * N_KV, WINDOW + TM, LANES), BF16),
        pltpu.VMEM((N_KV, WINDOW + TM, D_KV), BF16),
        pltpu.VMEM((HIST_PAD + TM, D_POOL), F32),
        pltpu.VMEM((SUB, D_MODEL), BF16),
        pltpu.VMEM((SUB, Z_WIDTH), F32),
        pltpu.VMEM((1, SUB, D_MODEL), F32),
    ]
    n_in = len(in_specs) - n_alias
    return pl.pallas_call(
        functools.partial(_prompt_body, layer, n_alias),
        grid=grid,
        in_specs=in_specs,
        out_specs=out_specs,
        out_shape=out_shape,
        scratch_shapes=scratch,
        input_output_aliases={n_in + i: 1 + i for i in range(n_alias)},
        compiler_params=pltpu.CompilerParams(
            dimension_semantics=("arbitrary", "arbitrary"),
            vmem_limit_bytes=VMEM_LIMIT_PROMPT),
        name="prompt_layer",
    )(sinks, x, cos, sin, w_in_b, w_out_b, npre, npost, pw_b, ps, lng, lnb, sw, sb_full, *prev_outs)


def _sample_proj_body(x_ref, npre_ref, w_ref, z_ref, hbuf):
    @pl.when(pl.program_id(0) == 0)
    def _norm_once():
        hbuf[...] = _rms(x_ref[...].reshape(SAMPLE_ROWS, D_MODEL), npre_ref[...]).astype(BF16)

    z_ref[...] = _dot(hbuf[...], w_ref[...])


def _sample_proj(layer, xs, npre, w_in_b):
    return pl.pallas_call(
        _sample_proj_body,
        grid=(D_IN // S1_TN,),
        in_specs=[
            pl.BlockSpec(xs.shape, lambda n: (0,) * xs.ndim),
            pl.BlockSpec((None, 1, D_MODEL), lambda n: (layer, 0, 0)),
            pl.BlockSpec((None, D_MODEL, S1_TN), lambda n: (layer, 0, n)),
        ],
        out_specs=pl.BlockSpec((SAMPLE_ROWS, S1_TN), lambda n: (0, n)),
        out_shape=jax.ShapeDtypeStruct((SAMPLE_ROWS, D_IN), F32),
        scratch_shapes=[pltpu.VMEM((SAMPLE_ROWS, D_MODEL), BF16)],
        compiler_params=pltpu.CompilerParams(
            dimension_semantics=("arbitrary",), vmem_limit_bytes=VMEM_LIMIT_SAMPLE),
        name="sample_proj",
    )(xs, npre, w_in_b)


SCORE_ROWS = N_HEADS * GR
N_REST_BLOCKS = (D_IN - C_XB) // LANES


def _sample_mix_body(n_alias, z_ref, cos_ref, sin_ref, sink_ref, ck_ref, cv_ref, st_ref,
                     pw_ref, ps_ref, lng_ref, lnb_ref, wexp_ref, bexp_ref, *rest):
    mix_ref, nk_ref, nv_ref, npool_ref, vn_ref, qbig, knew_f, vnew_f, zc, sc, npc, mo, vo = rest[n_alias:]
    cos = cos_ref[...]
    sin = sin_ref[...]

    q = _rope(z_ref[:, C_Q:C_Q + D_ATT], cos * ATT_SCALE, sin * ATT_SCALE)
    knew_f[...] = _rope(z_ref[:, C_K:C_K + D_KV], cos, sin)
    vnew_f[...] = z_ref[:, C_V:C_V + D_KV]
    low = lax.broadcasted_iota(jnp.int32, (RT, LANES), 1) < HEAD_DIM
    zero_blk = jnp.zeros((RT, LANES), F32)
    for h in range(N_HEADS):
        g = h // GROUP
        src = q[:, (h // 2) * LANES:(h // 2 + 1) * LANES]
        if h % 2 != g % 2:
            src = _swap_heads(src)
        piece = jnp.where(low, src, 0.0) if g % 2 == 0 else jnp.where(low, 0.0, src)
        full = [zero_blk, zero_blk]
        full[g // 2] = piece
        qbig[h] = jnp.concatenate(full, axis=1).astype(BF16)

    row_h = lax.broadcasted_iota(jnp.int32, (SCORE_ROWS, GB * WINDOW), 0)
    col_h = lax.broadcasted_iota(jnp.int32, (SCORE_ROWS, GB * WINDOW), 1)
    same_h = ((row_h >> 2) & (GB - 1)) == (col_h >> 7)
    bias_h = jnp.where(same_h, jnp.where((col_h & (WINDOW - 1)) > (row_h & (DEC_SEQ - 1)), 0.0, NEG_INF),
                       NEG_INF)
    row_n = lax.broadcasted_iota(jnp.int32, (SCORE_ROWS, GR), 0)
    col_n = lax.broadcasted_iota(jnp.int32, (SCORE_ROWS, GR), 1)
    same_n = ((row_n >> 2) & (GB - 1)) == (col_n >> 2)
    bias_n = jnp.where(same_n, jnp.where((col_n & (DEC_SEQ - 1)) <= (row_n & (DEC_SEQ - 1)), 0.0, NEG_INF),
                       NEG_INF)
    sink = sink_ref[:, 0:1]
    low_g = lax.broadcasted_iota(jnp.int32, (GR, LANES), 1) < HEAD_DIM
    row8 = lax.broadcasted_iota(jnp.int32, (SUBLANES, D_KV), 0)

    def roll_in(hist, new_tile):
        rolled = pltpu.roll(hist, WINDOW - DEC_SEQ, 0)
        tail = jnp.where(row8 >= SUBLANES - DEC_SEQ, new_tile, rolled[WINDOW - SUBLANES:])
        return jnp.concatenate([rolled[:WINDOW - SUBLANES], tail], axis=0)

    for grp in range(BT // GB):
        r0 = grp * GR
        lhs = qbig[:, r0:r0 + GR, :].reshape(SCORE_ROWS, D_KV)
        kh = ck_ref[grp * GB:(grp + 1) * GB]
        vh = cv_ref[grp * GB:(grp + 1) * GB]
        kn = knew_f[r0:r0 + GR, :]
        vn_new = vnew_f[r0:r0 + GR, :]
        s_h = _dot_nt(lhs, kh.reshape(GB * WINDOW, D_KV).astype(BF16)) + bias_h
        s_n = _dot_nt(lhs, kn.astype(BF16)) + bias_n
        m = jnp.maximum(jnp.maximum(jnp.max(s_h, axis=1, keepdims=True),
                                    jnp.max(s_n, axis=1, keepdims=True)), sink)
        p_h = jnp.exp(s_h - m)
        p_n = jnp.exp(s_n - m)
        den = (jnp.sum(p_h, axis=1, keepdims=True) + jnp.sum(p_n, axis=1, keepdims=True)
               + jnp.exp(sink - m))
        inv = 1.0 / den
        o = (_dot((p_h * inv).astype(BF16), vh.reshape(GB * WINDOW, D_KV).astype(BF16))
             + _dot((p_n * inv).astype(BF16), vn_new.astype(BF16)))
        for c in range(N_HEADS // 2):
            g = c // 2
            a = o[(2 * c) * GR:(2 * c + 1) * GR, (g // 2) * LANES:(g // 2 + 1) * LANES]
            b = o[(2 * c + 1) * GR:(2 * c + 2) * GR, (g // 2) * LANES:(g // 2 + 1) * LANES]
            blk = jnp.where(low_g, a, _swap_heads(b)) if g % 2 == 0 else jnp.where(low_g, _swap_heads(a), b)
            gate = _silu(z_ref[r0:r0 + GR, C_GA + c * LANES:C_GA + (c + 1) * LANES])
            mix_ref[r0:r0 + GR, c * LANES:(c + 1) * LANES] = blk * gate
        for bb in range(GB):
            k_tile = kn[(bb // 2) * SUBLANES:(bb // 2 + 1) * SUBLANES]
            v_tile = vn_new[(bb // 2) * SUBLANES:(bb // 2 + 1) * SUBLANES]
            if bb % 2 == 0:
                k_tile = pltpu.roll(k_tile, SUBLANES - DEC_SEQ, 0)
                v_tile = pltpu.roll(v_tile, SUBLANES - DEC_SEQ, 0)
            nk_ref[grp * GB + bb] = roll_in(kh[bb], k_tile)
            nv_ref[grp * GB + bb] = roll_in(vh[bb], v_tile)

    for c in range(N_REST_BLOCKS):
        zc[c] = z_ref[:, C_XB + c * LANES:C_XB + (c + 1) * LANES]
    for c in range(D_POOL // LANES):
        sc[c] = st_ref[:, c * LANES:(c + 1) * LANES]

    def tok(t, lo, width):
        blk0 = (lo - C_XB) // LANES
        parts = [zc[blk0 + c, pl.ds(t, BT, stride=DEC_SEQ), :] for c in range(width // LANES)]
        return parts[0] if len(parts) == 1 else jnp.concatenate(parts, axis=1)

    def ext(i, gi):
        if i < POOL_HIST:
            return sc[gi, pl.ds(i, BT, stride=POOL_HIST), :]
        return tok(i - POOL_HIST, C_XB + gi * LANES, LANES)

    for gi, w in enumerate(POOL_WINDOWS):
        for s in range(POOL_HIST):
            npc[gi, pl.ds(s, BT, stride=POOL_HIST), :] = ext(s + DEC_SEQ, gi)
        pooled = []
        for t in range(DEC_SEQ):
            cur = ext(POOL_HIST + t, gi)
            acc = cur
            for i in range(1, w):
                acc = acc + ext(POOL_HIST + t - i, gi)
            cnt = float(min(w, PAST_LEN + t + 1))
            pooled.append(acc / cnt - cur)
        pooled = jnp.concatenate(pooled, axis=0).astype(BF16)
        mixed = _dot(pooled, pw_ref[gi]) * ps_ref[:, gi * LANES:(gi + 1) * LANES]
        for t in range(DEC_SEQ):
            gate = _silu(tok(t, C_GB + gi * LANES, LANES))
            mo[gi, pl.ds(t, BT, stride=DEC_SEQ), :] = mixed[t * BT:(t + 1) * BT] * gate

    vns = []
    for t in range(DEC_SEQ):
        vn_t = _layernorm(tok(t, C_VS, D_SGU), lng_ref[...], lnb_ref[...])
        for c in range(D_SGU // LANES):
            vo[c, pl.ds(t, BT, stride=DEC_SEQ), :] = vn_t[:, c * LANES:(c + 1) * LANES]
        vns.append(vn_t)
    for t in range(DEC_SEQ):
        mixed = bexp_ref[t:t + 1, :]
        for s in range(t + 1):
            mixed = mixed + wexp_ref[DEC_SEQ * t + s:DEC_SEQ * t + s + 1, :] * vns[s]
        out = tok(t, C_U, D_SGU) * mixed * _silu(tok(t, C_GC, D_SGU))
        for c in range(D_SGU // LANES):
            mo[D_POOL // LANES + c, pl.ds(t, BT, stride=DEC_SEQ), :] = out[:, c * LANES:(c + 1) * LANES]

    for c in range(D_POOL // LANES):
        npool_ref[:, c * LANES:(c + 1) * LANES] = npc[c]
    for c in range((D_POOL + D_SGU) // LANES):
        mix_ref[:, D_ATT + c * LANES:D_ATT + (c + 1) * LANES] = mo[c]
    for c in range(D_SGU // LANES):
        vn_ref[:, c * LANES:(c + 1) * LANES] = vo[c]


def _sample_mix(layer, z, cos_t, sin_t, params, ck, cv, state2, prev_outs):
    (sink_rows, pw_b, ps, lng, lnb, wexp, bexp) = params
    n_alias = len(prev_outs)
    const = lambda shape: pl.BlockSpec(shape, lambda i: (0,) * len(shape))
    in_specs = [
        pl.BlockSpec((RT, D_IN), lambda i: (i, 0)),
        const((RT, LANES)), const((RT, LANES)),
        _layer_spec((SCORE_ROWS, LANES), layer, 1),
        pl.BlockSpec((None, BT, WINDOW, D_KV), lambda i: (layer, i, 0, 0)),
        pl.BlockSpec((None, BT, WINDOW, D_KV), lambda i: (layer, i, 0, 0)),
        pl.BlockSpec((None, BT * POOL_HIST, D_POOL), lambda i: (layer, i, 0)),
        _layer_spec((len(POOL_WINDOWS), LANES, LANES), layer, 1),
        _layer_spec((1, D_POOL), layer, 1),
        _layer_spec((1, D_SGU), layer, 1),
        _layer_spec((1, D_SGU), layer, 1),
        _layer_spec((DEC_SEQ * DEC_SEQ, D_SGU), layer, 1),
        _layer_spec((DEC_SEQ, D_SGU), layer, 1),
    ] + [pl.BlockSpec(memory_space=pl.ANY)] * n_alias
    out_specs = [
        pl.BlockSpec((RT, D_MODEL), lambda i: (i, 0)),
        pl.BlockSpec((None, BT, WINDOW, D_KV), lambda i: (layer, i, 0, 0)),
        pl.BlockSpec((None, BT, WINDOW, D_KV), lambda i: (layer, i, 0, 0)),
        pl.BlockSpec((None, BT * POOL_HIST, D_POOL), lambda i: (layer, i, 0)),
        pl.BlockSpec((None, RT, D_SGU), lambda i: (layer, i, 0)),
    ]
    out_shape = [
        jax.ShapeDtypeStruct((SAMPLE_ROWS, D_MODEL), F32),
        jax.ShapeDtypeStruct((DEPTH, DEC_BATCH, WINDOW, D_KV), F32),
        jax.ShapeDtypeStruct((DEPTH, DEC_BATCH, WINDOW, D_KV), F32),
        jax.ShapeDtypeStruct((DEPTH, DEC_BATCH * POOL_HIST, D_POOL), F32),
        jax.ShapeDtypeStruct((DEPTH, SAMPLE_ROWS, D_SGU), F32),
    ]
    scratch = [
        pltpu.VMEM((N_HEADS, RT, D_KV), BF16),
        pltpu.VMEM((RT, D_KV), F32),
        pltpu.VMEM((RT, D_KV), F32),
        pltpu.VMEM((N_REST_BLOCKS, RT, LANES), F32),
        pltpu.VMEM((D_POOL // LANES, BT * POOL_HIST, LANES), F32),
        pltpu.VMEM((D_POOL // LANES, BT * POOL_HIST, LANES), F32),
        pltpu.VMEM(((D_POOL + D_SGU) // LANES, RT, LANES), F32),
        pltpu.VMEM((D_SGU // LANES, RT, LANES), F32),
    ]
    n_in = len(in_specs) - n_alias
    return pl.pallas_call(
        functools.partial(_sample_mix_body, n_alias),
        grid=(DEC_BATCH // BT,),
        in_specs=in_specs,
        out_specs=out_specs,
        out_shape=out_shape,
        scratch_shapes=scratch,
        input_output_aliases={n_in + i: 1 + i for i in range(n_alias)},
        compiler_params=pltpu.CompilerParams(
            dimension_semantics=("arbitrary",), vmem_limit_bytes=VMEM_LIMIT_SAMPLE),
        name="sample_mix",
    )(z, cos_t, sin_t, sink_rows, ck, cv, state2, pw_b, ps, lng, lnb, wexp, bexp, *prev_outs)


def _sample_out_body(x_ref, mix_ref, w_out_ref, npost_ref, y_ref):
    y = _dot(mix_ref[...].astype(BF16), w_out_ref[...])
    out = x_ref[...].reshape(SAMPLE_ROWS, D_MODEL) + _rms(y, npost_ref[...])
    y_ref[...] = out.reshape(y_ref.shape)


def _sample_out(layer, xs, mix, w_out_b, npost, out_shape):
    full = lambda shape: pl.BlockSpec(shape, lambda i: (0,) * len(shape))
    return pl.pallas_call(
        _sample_out_body,
        grid=(1,),
        in_specs=[
            full(xs.shape),
            full((SAMPLE_ROWS, D_MODEL)),
            pl.BlockSpec((None, D_MODEL, D_MODEL), lambda i: (layer, 0, 0)),
            pl.BlockSpec((None, 1, D_MODEL), lambda i: (layer, 0, 0)),
        ],
        out_specs=full(out_shape),
        out_shape=jax.ShapeDtypeStruct(out_shape, F32),
        compiler_params=pltpu.CompilerParams(
            dimension_semantics=("arbitrary",), vmem_limit_bytes=VMEM_LIMIT_SAMPLE),
        name="sample_out",
    )(xs, mix, w_out_b, npost)


def _rope_tables(pos):
    inv = ROPE_THETA ** (-jnp.arange(0, HEAD_DIM, 2, dtype=F32) / HEAD_DIM)
    ang = pos.astype(F32)[:, None] * inv[None, :]
    c, s = jnp.cos(ang), jnp.sin(ang)
    return jnp.concatenate([c, c, c, c], axis=1), jnp.concatenate([-s, s, -s, s], axis=1)


def kernel(x_prompt, x_sample, cache_k, cache_v, state_pool, w_in, w_out, norm_pre, norm_post,
           attn_sinks, pool_w, pool_scale, sgu_ln_g, sgu_ln_b, sgu_w, sgu_b):
    cos_p, sin_p = _rope_tables(jnp.arange(SEQ))
    cos_s, sin_s = _rope_tables(PAST_LEN + jnp.arange(DEC_SEQ))
    cos_t = jnp.tile(cos_s, (BT, 1))
    sin_t = jnp.tile(sin_s, (BT, 1))

    w_in_b = w_in.astype(BF16)
    w_out_b = w_out.astype(BF16)
    pw_b = pool_w.astype(BF16)
    npre = norm_pre[:, None, :]
    npost = norm_post[:, None, :]
    ps = pool_scale[:, None, :]
    lng = sgu_ln_g[:, None, :]
    lnb = sgu_ln_b[:, None, :]
    sb_full = jnp.broadcast_to(sgu_b[:, :, :, None], (DEPTH, N_SGU_HEADS, CHUNK, LANES))
    wexp = jnp.repeat(sgu_w[:, :, :DEC_SEQ, :DEC_SEQ].transpose(0, 2, 3, 1), LANES, axis=-1
                      ).reshape(DEPTH, DEC_SEQ * DEC_SEQ, D_SGU)
    bexp = jnp.repeat(sgu_b[:, :, :DEC_SEQ].transpose(0, 2, 1), LANES, axis=-1)
    sink_rows = jnp.broadcast_to(attn_sinks[:, :, None, None], (DEPTH, N_HEADS, GR, LANES)
                                 ).reshape(DEPTH, SCORE_ROWS, LANES)

    xp = x_prompt
    xs = x_sample
    ck = cache_k.reshape(DEPTH, DEC_BATCH, WINDOW, D_KV)
    cv = cache_v.reshape(DEPTH, DEC_BATCH, WINDOW, D_KV)
    state2 = state_pool.reshape(DEPTH, DEC_BATCH * POOL_HIST, D_POOL)
    prompt_params = (attn_sinks, w_in_b, w_out_b, npre, npost, pw_b, ps, lng, lnb, sgu_w, sb_full)
    sample_params = (sink_rows, pw_b, ps, lng, lnb, wexp, bexp)
    p_outs, s_outs = [], []
    for layer in range(DEPTH):
        xp, *p_outs = _prompt_layer(layer, xp, cos_p, sin_p, prompt_params, p_outs)
        z = _sample_proj(layer, xs, npre, w_in_b)
        mix, *s_outs = _sample_mix(layer, z, cos_t, sin_t, sample_params, ck, cv, state2, s_outs)
        xs_shape = x_sample.shape if layer == DEPTH - 1 else (SAMPLE_ROWS, D_MODEL)
        xs = _sample_out(layer, xs, mix, w_out_b, npost, xs_shape)

    nk_p, nv_p, np_p = p_outs
    nk_s, nv_s, np_s, vn_s = s_outs
    kv5 = lambda a, nb: a.reshape(DEPTH, nb, WINDOW, N_KV, HEAD_DIM)
    return (xp, xs,
            kv5(nk_p, BATCH), kv5(nv_p, BATCH), np_p,
            kv5(nk_s, DEC_BATCH), kv5(nv_s, DEC_BATCH),
            np_s.reshape(DEPTH, DEC_BATCH, POOL_HIST, D_POOL),
            vn_s.reshape(DEPTH, DEC_BATCH, DEC_SEQ, D_SGU))
```

```python
import functools

import jax
import jax.numpy as jnp
from jax import lax
from jax.experimental import pallas as pl
from jax.experimental.pallas import tpu as pltpu

D_MODEL = 2048
SEQ = 2048
BATCH = 8
DEPTH = 2
DEC_BATCH = 128
DEC_SEQ = 4
PAST_LEN = 8192
HEAD_DIM = 64
HALF_DIM = HEAD_DIM // 2
N_HEADS = 16
N_KV = 4
GROUP = N_HEADS // N_KV
D_ATT = N_HEADS * HEAD_DIM
D_KV = N_KV * HEAD_DIM
WINDOW = 128
ROPE_THETA = 10000.0
D_POOL = 512
POOL_WINDOWS = (2, 4, 8, 16)
POOL_HIST = 15
D_SGU = 512
CHUNK = 128
N_SGU_HEADS = 4
D_IN = 5120
EPS = 1e-6
ATT_SCALE = HEAD_DIM ** -0.5
LOG2_E = 1.4426950408889634

C_Q = 0
C_K = C_Q + D_ATT
C_V = C_K + D_KV
C_GA = C_V + D_KV
C_XB = C_GA + D_ATT
C_GB = C_XB + D_POOL
C_U = C_GB + D_POOL
C_VS = C_U + D_SGU
C_GC = C_VS + D_SGU
assert C_GC + D_SGU == D_IN

LANES = 128
SUBLANES = 8
HIST_PAD = 16
SUB = 256
N_SUB = 2
TM = SUB * N_SUB
N_CHUNKS = SUB // CHUNK
SAMPLE_ROWS = DEC_BATCH * DEC_SEQ
BT = 16
RT = BT * DEC_SEQ
GB = 4
GR = GB * DEC_SEQ
ATT_PIECE = 512
REST_PIECE = 256
N_REST_PIECES = (D_IN - C_XB) // REST_PIECE
Z_U = D_POOL
Z_VS = Z_U + D_SGU
Z_WIDTH = Z_VS + D_SGU
S1_TN = 1024
VMEM_LIMIT_PROMPT = 62 * 1024 * 1024
VMEM_LIMIT_SAMPLE = 48 * 1024 * 1024

F32 = jnp.float32
BF16 = jnp.bfloat16
NEG_INF = float("-inf")


def _rms(x, g):
    return x * lax.rsqrt(jnp.mean(x * x, axis=-1, keepdims=True) + EPS) * g


def _silu(x):
    return x * (1.0 / (1.0 + jnp.exp(-x)))


def _layernorm(x, g, b):
    mu = jnp.mean(x, axis=-1, keepdims=True)
    xc = x - mu
    var = jnp.mean(xc * xc, axis=-1, keepdims=True)
    return xc * lax.rsqrt(var + EPS) * g + b


def _swap_heads(x):
    return pltpu.roll(x, HEAD_DIM, 1)


def _rope(x, cos, sin):
    rows = x.shape[0]
    lane = lax.broadcasted_iota(jnp.int32, (rows, LANES), 1)
    first_half = (lane & HALF_DIM) == 0
    outs = []
    for c in range(x.shape[1] // LANES):
        xc = x[:, c * LANES:(c + 1) * LANES]
        partner = jnp.where(first_half,
                            pltpu.roll(xc, LANES - HALF_DIM, 1),
                            pltpu.roll(xc, HALF_DIM, 1))
        outs.append(xc * cos + partner * sin)
    return jnp.concatenate(outs, axis=1)


def _dot(a, b):
    return jnp.dot(a, b, preferred_element_type=F32)


def _dot_nt(a, b):
    return lax.dot_general(a, b, (((1,), (1,)), ((), ())), preferred_element_type=F32)


def _prompt_body(layer, n_alias, sinks_ref, x_ref, cos_ref, sin_ref, w_in_ref, w_out_ref,
                 npre_ref, npost_ref, pw_ref, ps_ref, lng_ref, lnb_ref, sw_ref, sb_ref, *rest):
    refs = rest[n_alias:]
    kbuf, vbuf, xbext = refs[5], refs[6], refs[7]
    j = pl.program_id(1)

    @pl.when(j == 0)
    def _start_of_sequence():
        kbuf[:, 0:WINDOW, :] = jnp.zeros((2 * N_KV, WINDOW, LANES), BF16)
        vbuf[:, 0:WINDOW, :] = jnp.zeros((N_KV, WINDOW, D_KV), BF16)
        xbext[0:HIST_PAD, :] = jnp.zeros((HIST_PAD, D_POOL), F32)

    @pl.when(j > 0)
    def _carry_from_previous_step():
        for i in range(2 * N_KV):
            kbuf[i, 0:WINDOW, :] = kbuf[i, TM:TM + WINDOW, :]
        for g in range(N_KV):
            vbuf[g, 0:WINDOW, :] = vbuf[g, TM:TM + WINDOW, :]
        xbext[0:HIST_PAD, :] = xbext[TM:TM + HIST_PAD, :]

    for sub in range(N_SUB):
        _prompt_sub_block(layer, sub, j, sinks_ref, x_ref, cos_ref, sin_ref, w_in_ref, w_out_ref,
                          npre_ref, npost_ref, pw_ref, ps_ref, lng_ref, lnb_ref, sw_ref, sb_ref, *refs)


def _prompt_sub_block(layer, sub, j, sinks_ref, x_ref, cos_ref, sin_ref, w_in_ref, w_out_ref,
                      npre_ref, npost_ref, pw_ref, ps_ref, lng_ref, lnb_ref, sw_ref, sb_ref,
                      y_ref, nk_ref, nv_ref, np_ref, qbuf, kbuf, vbuf, xbext, mixbuf, zrest, ybuf):
    base = sub * SUB
    sub_rows = slice(base, base + SUB)
    first_in_sequence = (j == 0) if sub == 0 else None
    h = _rms(x_ref[0, sub_rows], npre_ref[...]).astype(BF16)

    def proj(lo, width):
        return _dot(h, w_in_ref[:, lo:lo + width])

    cos = cos_ref[sub_rows, :]
    sin = sin_ref[sub_rows, :]

    k = _rope(proj(C_K, D_KV), cos, sin)
    v = proj(C_V, D_KV)
    cos_q = cos * (ATT_SCALE * LOG2_E)
    sin_q = sin * (ATT_SCALE * LOG2_E)
    for part in range(D_ATT // ATT_PIECE):
        cols = slice(part * ATT_PIECE, (part + 1) * ATT_PIECE)
        qbuf[:, cols] = _rope(proj(C_Q + part * ATT_PIECE, ATT_PIECE), cos_q, sin_q).astype(BF16)
    for part in range(D_ATT // ATT_PIECE):
        cols = slice(part * ATT_PIECE, (part + 1) * ATT_PIECE)
        mixbuf[:, cols] = _silu(proj(C_GA + part * ATT_PIECE, ATT_PIECE)).astype(BF16)
    if sub == N_SUB - 1:
        nk_ref[0] = k[SUB - WINDOW:SUB]
        nv_ref[0] = v[SUB - WINDOW:SUB]
    lane_k = lax.broadcasted_iota(jnp.int32, (SUB, LANES), 1)
    low = lane_k < HEAD_DIM
    lane_v = lax.broadcasted_iota(jnp.int32, (SUB, D_KV), 1)
    def store_by_chunk(buf, idx, val):
        val = val.astype(BF16)
        for c in range(N_CHUNKS):
            r0 = WINDOW + base + c * CHUNK
            buf[idx, r0:r0 + CHUNK, :] = val[c * CHUNK:(c + 1) * CHUNK]

    for pair in range(2):
        kp = k[:, pair * LANES:(pair + 1) * LANES]
        ks = _swap_heads(kp)
        store_by_chunk(kbuf, 4 * pair + 0, jnp.where(low, kp, 0.0))
        store_by_chunk(kbuf, 4 * pair + 1, jnp.where(low, 0.0, ks))
        store_by_chunk(kbuf, 4 * pair + 2, jnp.where(low, ks, 0.0))
        store_by_chunk(kbuf, 4 * pair + 3, jnp.where(low, 0.0, kp))
    for g in range(N_KV):
        keep_v = (lane_v >= g * HEAD_DIM) & (lane_v < (g + 1) * HEAD_DIM)
        store_by_chunk(vbuf, g, jnp.where(keep_v, v, 0.0))

    qi = lax.broadcasted_iota(jnp.int32, (CHUNK, 2 * WINDOW), 0)
    kj = lax.broadcasted_iota(jnp.int32, (CHUNK, 2 * WINDOW), 1)
    dist = qi + WINDOW - kj
    band_bias = jnp.where((dist >= 0) & (dist < WINDOW), 0.0, NEG_INF)
    if first_in_sequence is None:
        first_bias = band_bias
    else:
        no_prev = jnp.where(first_in_sequence, WINDOW, 0)
        first_bias = jnp.where(kj < no_prev, NEG_INF, band_bias)
    low_c = lax.broadcasted_iota(jnp.int32, (CHUNK, LANES), 1) < HEAD_DIM
    rest_piece = 0

    def project_rest_piece(piece):
        cols = slice(piece * REST_PIECE, (piece + 1) * REST_PIECE)
        lo = C_XB + piece * REST_PIECE
        val = proj(lo, REST_PIECE)
        if C_GB <= lo < C_U:
            mixbuf[:, D_ATT + lo - C_GB:D_ATT + lo - C_GB + REST_PIECE] = _silu(val).astype(BF16)
        elif lo >= C_GC:
            at = D_ATT + D_POOL + lo - C_GC
            mixbuf[:, at:at + REST_PIECE] = _silu(val).astype(BF16)
        else:
            at = lo - C_XB if lo < C_GB else (Z_U + lo - C_U if lo < C_VS else Z_VS + lo - C_VS)
            zrest[:, at:at + REST_PIECE] = val

    def scores(c, g):
        r0 = c * CHUNK
        qs = jnp.concatenate(
            [qbuf[r0:r0 + CHUNK, (2 * g + i) * LANES:(2 * g + i + 1) * LANES] for i in range(2)],
            axis=0)
        return [_dot_nt(qs, kbuf[2 * g + half, base + r0:base + r0 + 2 * WINDOW, :]) for half in range(2)]

    order = [(c, g) for c in range(N_CHUNKS) for g in range(N_KV)]
    while rest_piece < N_REST_PIECES - len(order):
        project_rest_piece(rest_piece)
        rest_piece += 1
    s_next = scores(*order[0])
    for step, (c, g) in enumerate(order):
        bias = first_bias if c == 0 else band_bias
        r0 = c * CHUNK
        s_half = s_next
        if step + 1 < len(order):
            s_next = scores(*order[step + 1])
        blocks = []
        for r in range(GROUP):
            sb = s_half[r % 2][(r // 2) * CHUNK:(r // 2 + 1) * CHUNK] + bias
            sink = sinks_ref[layer, GROUP * g + r] * LOG2_E
            m = jnp.maximum(jnp.max(sb, axis=1, keepdims=True), sink)
            pe = jnp.exp2(sb - m)
            den = jnp.sum(pe, axis=1, keepdims=True) + jnp.exp2(sink - m)
            blocks.append((pe * (1.0 / den)).astype(BF16))
        pg = jnp.concatenate(blocks, axis=0)
        og = _dot(pg, vbuf[g, base + r0:base + r0 + 2 * WINDOW, :])
        o = og if g == 0 else o + og
        if rest_piece < N_REST_PIECES:
            project_rest_piece(rest_piece)
            rest_piece += 1
        if g < N_KV - 1:
            continue
        for pair in range(2):
            for i in range(2):
                a = o[(2 * i) * CHUNK:(2 * i + 1) * CHUNK, pair * LANES:(pair + 1) * LANES]
                b = o[(2 * i + 1) * CHUNK:(2 * i + 2) * CHUNK, pair * LANES:(pair + 1) * LANES]
                for odd, blk in ((0, jnp.where(low_c, a, _swap_heads(b))),
                                 (1, jnp.where(low_c, _swap_heads(a), b))):
                    cols = slice((2 * (2 * pair + odd) + i) * LANES, (2 * (2 * pair + odd) + i + 1) * LANES)
                    mixbuf[r0:r0 + CHUNK, cols] = (blk * mixbuf[r0:r0 + CHUNK, cols]).astype(BF16)

    while rest_piece < N_REST_PIECES:
        project_rest_piece(rest_piece)
        rest_piece += 1

    xb0 = HIST_PAD + base
    xbext[xb0:xb0 + SUB, :] = zrest[:, 0:D_POOL]
    if sub == N_SUB - 1:
        np_ref[0] = xbext[HIST_PAD + TM - POOL_HIST:HIST_PAD + TM, :]
    pos1 = j * TM + base + lax.broadcasted_iota(jnp.int32, (SUB, LANES), 0) + 1
    for gi, w in enumerate(POOL_WINDOWS):
        cols = slice(gi * LANES, (gi + 1) * LANES)
        cur = xbext[xb0:xb0 + SUB, cols]
        acc = cur
        for i in range(1, w):
            acc = acc + xbext[xb0 - i:xb0 - i + SUB, cols]
        cnt = jnp.minimum(w, pos1).astype(F32)
        pooled = acc / cnt - cur
        mixed = _dot(pooled.astype(BF16), pw_ref[gi]) * ps_ref[:, cols]
        out_cols = slice(D_ATT + gi * LANES, D_ATT + (gi + 1) * LANES)
        mixbuf[:, out_cols] = (mixed * mixbuf[:, out_cols]).astype(BF16)

    vn = _layernorm(zrest[:, Z_VS:Z_VS + D_SGU], lng_ref[...], lnb_ref[...]).astype(BF16)
    ti = lax.broadcasted_iota(jnp.int32, (CHUNK, CHUNK), 0)
    si = lax.broadcasted_iota(jnp.int32, (CHUNK, CHUNK), 1)
    for hh in range(N_SGU_HEADS):
        cols = slice(hh * LANES, (hh + 1) * LANES)
        wm = jnp.where(ti >= si, sw_ref[hh], 0.0).astype(BF16)
        for c in range(N_CHUNKS):
            rows = slice(c * CHUNK, (c + 1) * CHUNK)
            mixed = _dot(wm, vn[rows, cols]) + sb_ref[hh]
            out_cols = slice(D_ATT + D_POOL + hh * LANES, D_ATT + D_POOL + (hh + 1) * LANES)
            mixbuf[rows, out_cols] = (
                zrest[rows, Z_U + hh * LANES:Z_U + (hh + 1) * LANES] * mixed * mixbuf[rows, out_cols]
            ).astype(BF16)

    slot = jnp.minimum(j, 0)
    y = _dot(mixbuf[...], w_out_ref[...])
    ybuf[slot] = y
    if sub < N_SUB - 1:
        y = ybuf[slot]
    y_ref[0, sub_rows] = x_ref[0, sub_rows] + _rms(y, npost_ref[...])


def _layer_spec(shape, layer, grid_rank):
    zeros = (0,) * len(shape)
    if grid_rank == 2:
        index_map = lambda b, j: (layer,) + zeros
    else:
        index_map = lambda i: (layer,) + zeros
    return pl.BlockSpec((None,) + tuple(shape), index_map, pipeline_mode=pl.Buffered(1))


def _prompt_layer(layer, x, cos, sin, params, prev_outs):
    (sinks, w_in_b, w_out_b, npre, npost, pw_b, ps, lng, lnb, sw, sb_full) = params
    grid = (BATCH, SEQ // TM)
    n_alias = len(prev_outs)
    in_specs = [
        pl.BlockSpec(memory_space=pltpu.SMEM),
        pl.BlockSpec((1, TM, D_MODEL), lambda b, j: (b, j, 0)),
        pl.BlockSpec((TM, LANES), lambda b, j: (j, 0)),
        pl.BlockSpec((TM, LANES), lambda b, j: (j, 0)),
        _layer_spec((D_MODEL, D_IN), layer, 2),
        _layer_spec((D_MODEL, D_MODEL), layer, 2),
        _layer_spec((1, D_MODEL), layer, 2),
        _layer_spec((1, D_MODEL), layer, 2),
        _layer_spec((len(POOL_WINDOWS), LANES, LANES), layer, 2),
        _layer_spec((1, D_POOL), layer, 2),
        _layer_spec((1, D_SGU), layer, 2),
        _layer_spec((1, D_SGU), layer, 2),
        _layer_spec((N_SGU_HEADS, CHUNK, CHUNK), layer, 2),
        _layer_spec((N_SGU_HEADS, CHUNK, LANES), layer, 2),
    ] + [pl.BlockSpec(memory_space=pl.ANY)] * n_alias
    out_specs = [
        pl.BlockSpec((1, TM, D_MODEL), lambda b, j: (b, j, 0)),
        pl.BlockSpec((None, 1, WINDOW, D_KV), lambda b, j: (layer, b, 0, 0)),
        pl.BlockSpec((None, 1, WINDOW, D_KV), lambda b, j: (layer, b, 0, 0)),
        pl.BlockSpec((None, 1, POOL_HIST, D_POOL), lambda b, j: (layer, b, 0, 0)),
    ]
    out_shape = [
        jax.ShapeDtypeStruct((BATCH, SEQ, D_MODEL), F32),
        jax.ShapeDtypeStruct((DEPTH, BATCH, WINDOW, D_KV), F32),
        jax.ShapeDtypeStruct((DEPTH, BATCH, WINDOW, D_KV), F32),
        jax.ShapeDtypeStruct((DEPTH, BATCH, POOL_HIST, D_POOL), F32),
    ]
    scratch = [
        pltpu.VMEM((SUB, D_ATT), BF16),
        pltpu.VMEM((2 * N_KV, WINDOW + TM, LANES), BF16),
        pltpu.VMEM((N_KV, WINDOW + TM, D_KV), BF16),
        pltpu.VMEM((HIST_PAD + TM, D_POOL), F32),
        pltpu.VMEM((SUB, D_MODEL), BF16),
        pltpu.VMEM((SUB, Z_WIDTH), F32),
        pltpu.VMEM((1, SUB, D_MODEL), F32),
    ]
    n_in = len(in_specs) - n_alias
    return pl.pallas_call(
        functools.partial(_prompt_body, layer, n_alias),
        grid=grid,
        in_specs=in_specs,
        out_specs=out_specs,
        out_shape=out_shape,
        scratch_shapes=scratch,
        input_output_aliases={n_in + i: 1 + i for i in range(n_alias)},
        compiler_params=pltpu.CompilerParams(
            dimension_semantics=("arbitrary", "arbitrary"),
            vmem_limit_bytes=VMEM_LIMIT_PROMPT),
        name="prompt_layer",
    )(sinks, x, cos, sin, w_in_b, w_out_b, npre, npost, pw_b, ps, lng, lnb, sw, sb_full, *prev_outs)


def _sample_proj_body(x_ref, npre_ref, w_ref, z_ref, hbuf):
    @pl.when(pl.program_id(0) == 0)
    def _norm_once():
        hbuf[...] = _rms(x_ref[...].reshape(SAMPLE_ROWS, D_MODEL), npre_ref[...]).astype(BF16)

    z_ref[...] = _dot(hbuf[...], w_ref[...])


def _sample_proj(layer, xs, npre, w_in_b):
    return pl.pallas_call(
        _sample_proj_body,
        grid=(D_IN // S1_TN,),
        in_specs=[
            pl.BlockSpec(xs.shape, lambda n: (0,) * xs.ndim),
            pl.BlockSpec((None, 1, D_MODEL), lambda n: (layer, 0, 0)),
            pl.BlockSpec((None, D_MODEL, S1_TN), lambda n: (layer, 0, n)),
        ],
        out_specs=pl.BlockSpec((SAMPLE_ROWS, S1_TN), lambda n: (0, n)),
        out_shape=jax.ShapeDtypeStruct((SAMPLE_ROWS, D_IN), F32),
        scratch_shapes=[pltpu.VMEM((SAMPLE_ROWS, D_MODEL), BF16)],
        compiler_params=pltpu.CompilerParams(
            dimension_semantics=("arbitrary",), vmem_limit_bytes=VMEM_LIMIT_SAMPLE),
        name="sample_proj",
    )(xs, npre, w_in_b)


SCORE_ROWS = N_HEADS * GR
N_REST_BLOCKS = (D_IN - C_XB) // LANES


def _sample_mix_body(n_alias, z_ref, cos_ref, sin_ref, sink_ref, ck_ref, cv_ref, st_ref,
                     pw_ref, ps_ref, lng_ref, lnb_ref, wexp_ref, bexp_ref, *rest):
    mix_ref, nk_ref, nv_ref, npool_ref, vn_ref, qbig, knew_f, vnew_f, zc, sc, npc, mo, vo = rest[n_alias:]
    cos = cos_ref[...]
    sin = sin_ref[...]

    q = _rope(z_ref[:, C_Q:C_Q + D_ATT], cos * ATT_SCALE, sin * ATT_SCALE)
    knew_f[...] = _rope(z_ref[:, C_K:C_K + D_KV], cos, sin)
    vnew_f[...] = z_ref[:, C_V:C_V + D_KV]
    low = lax.broadcasted_iota(jnp.int32, (RT, LANES), 1) < HEAD_DIM
    zero_blk = jnp.zeros((RT, LANES), F32)
    for h in range(N_HEADS):
        g = h // GROUP
        src = q[:, (h // 2) * LANES:(h // 2 + 1) * LANES]
        if h % 2 != g % 2:
            src = _swap_heads(src)
        piece = jnp.where(low, src, 0.0) if g % 2 == 0 else jnp.where(low, 0.0, src)
        full = [zero_blk, zero_blk]
        full[g // 2] = piece
        qbig[h] = jnp.concatenate(full, axis=1).astype(BF16)

    row_h = lax.broadcasted_iota(jnp.int32, (SCORE_ROWS, GB * WINDOW), 0)
    col_h = lax.broadcasted_iota(jnp.int32, (SCORE_ROWS, GB * WINDOW), 1)
    same_h = ((row_h >> 2) & (GB - 1)) == (col_h >> 7)
    bias_h = jnp.where(same_h, jnp.where((col_h & (WINDOW - 1)) > (row_h & (DEC_SEQ - 1)), 0.0, NEG_INF),
                       NEG_INF)
    row_n = lax.broadcasted_iota(jnp.int32, (SCORE_ROWS, GR), 0)
    col_n = lax.broadcasted_iota(jnp.int32, (SCORE_ROWS, GR), 1)
    same_n = ((row_n >> 2) & (GB - 1)) == (col_n >> 2)
    bias_n = jnp.where(same_n, jnp.where((col_n & (DEC_SEQ - 1)) <= (row_n & (DEC_SEQ - 1)), 0.0, NEG_INF),
                       NEG_INF)
    sink = sink_ref[:, 0:1]
    low_g = lax.broadcasted_iota(jnp.int32, (GR, LANES), 1) < HEAD_DIM
    row8 = lax.broadcasted_iota(jnp.int32, (SUBLANES, D_KV), 0)

    def roll_in(hist, new_tile):
        rolled = pltpu.roll(hist, WINDOW - DEC_SEQ, 0)
        tail = jnp.where(row8 >= SUBLANES - DEC_SEQ, new_tile, rolled[WINDOW - SUBLANES:])
        return jnp.concatenate([rolled[:WINDOW - SUBLANES], tail], axis=0)

    for grp in range(BT // GB):
        r0 = grp * GR
        lhs = qbig[:, r0:r0 + GR, :].reshape(SCORE_ROWS, D_KV)
        kh = ck_ref[grp * GB:(grp + 1) * GB]
        vh = cv_ref[grp * GB:(grp + 1) * GB]
        kn = knew_f[r0:r0 + GR, :]
        vn_new = vnew_f[r0:r0 + GR, :]
        s_h = _dot_nt(lhs, kh.reshape(GB * WINDOW, D_KV).astype(BF16)) + bias_h
        s_n = _dot_nt(lhs, kn.astype(BF16)) + bias_n
        m = jnp.maximum(jnp.maximum(jnp.max(s_h, axis=1, keepdims=True),
                                    jnp.max(s_n, axis=1, keepdims=True)), sink)
        p_h = jnp.exp(s_h - m)
        p_n = jnp.exp(s_n - m)
        den = (jnp.sum(p_h, axis=1, keepdims=True) + jnp.sum(p_n, axis=1, keepdims=True)
               + jnp.exp(sink - m))
        inv = 1.0 / den
        o = (_dot((p_h * inv).astype(BF16), vh.reshape(GB * WINDOW, D_KV).astype(BF16))
             + _dot((p_n * inv).astype(BF16), vn_new.astype(BF16)))
        for c in range(N_HEADS // 2):
            g = c // 2
            a = o[(2 * c) * GR:(2 * c + 1) * GR, (g // 2) * LANES:(g // 2 + 1) * LANES]
            b = o[(2 * c + 1) * GR:(2 * c + 2) * GR, (g // 2) * LANES:(g // 2 + 1) * LANES]
            blk = jnp.where(low_g, a, _swap_heads(b)) if g % 2 == 0 else jnp.where(low_g, _swap_heads(a), b)
            gate = _silu(z_ref[r0:r0 + GR, C_GA + c * LANES:C_GA + (c + 1) * LANES])
            mix_ref[r0:r0 + GR, c * LANES:(c + 1) * LANES] = blk * gate
        for bb in range(GB):
            k_tile = kn[(bb // 2) * SUBLANES:(bb // 2 + 1) * SUBLANES]
            v_tile = vn_new[(bb // 2) * SUBLANES:(bb // 2 + 1) * SUBLANES]
            if bb % 2 == 0:
                k_tile = pltpu.roll(k_tile, SUBLANES - DEC_SEQ, 0)
                v_tile = pltpu.roll(v_tile, SUBLANES - DEC_SEQ, 0)
            nk_ref[grp * GB + bb] = roll_in(kh[bb], k_tile)
            nv_ref[grp * GB + bb] = roll_in(vh[bb], v_tile)

    for c in range(N_REST_BLOCKS):
        zc[c] = z_ref[:, C_XB + c * LANES:C_XB + (c + 1) * LANES]
    for c in range(D_POOL // LANES):
        sc[c] = st_ref[:, c * LANES:(c + 1) * LANES]

    def tok(t, lo, width):
        blk0 = (lo - C_XB) // LANES
        parts = [zc[blk0 + c, pl.ds(t, BT, stride=DEC_SEQ), :] for c in range(width // LANES)]
        return parts[0] if len(parts) == 1 else jnp.concatenate(parts, axis=1)

    def ext(i, gi):
        if i < POOL_HIST:
            return sc[gi, pl.ds(i, BT, stride=POOL_HIST), :]
        return tok(i - POOL_HIST, C_XB + gi * LANES, LANES)

    for gi, w in enumerate(POOL_WINDOWS):
        for s in range(POOL_HIST):
            npc[gi, pl.ds(s, BT, stride=POOL_HIST), :] = ext(s + DEC_SEQ, gi)
        pooled = []
        for t in range(DEC_SEQ):
            cur = ext(POOL_HIST + t, gi)
            acc = cur
            for i in range(1, w):
                acc = acc + ext(POOL_HIST + t - i, gi)
            cnt = float(min(w, PAST_LEN + t + 1))
            pooled.append(acc / cnt - cur)
        pooled = jnp.concatenate(pooled, axis=0).astype(BF16)
        mixed = _dot(pooled, pw_ref[gi]) * ps_ref[:, gi * LANES:(gi + 1) * LANES]
        for t in range(DEC_SEQ):
            gate = _silu(tok(t, C_GB + gi * LANES, LANES))
            mo[gi, pl.ds(t, BT, stride=DEC_SEQ), :] = mixed[t * BT:(t + 1) * BT] * gate

    vns = []
    for t in range(DEC_SEQ):
        vn_t = _layernorm(tok(t, C_VS, D_SGU), lng_ref[...], lnb_ref[...])
        for c in range(D_SGU // LANES):
            vo[c, pl.ds(t, BT, stride=DEC_SEQ), :] = vn_t[:, c * LANES:(c + 1) * LANES]
        vns.append(vn_t)
    for t in range(DEC_SEQ):
        mixed = bexp_ref[t:t + 1, :]
        for s in range(t + 1):
            mixed = mixed + wexp_ref[DEC_SEQ * t + s:DEC_SEQ * t + s + 1, :] * vns[s]
        out = tok(t, C_U, D_SGU) * mixed * _silu(tok(t, C_GC, D_SGU))
        for c in range(D_SGU // LANES):
            mo[D_POOL // LANES + c, pl.ds(t, BT, stride=DEC_SEQ), :] = out[:, c * LANES:(c + 1) * LANES]

    for c in range(D_POOL // LANES):
        npool_ref[:, c * LANES:(c + 1) * LANES] = npc[c]
    for c in range((D_POOL + D_SGU) // LANES):
        mix_ref[:, D_ATT + c * LANES:D_ATT + (c + 1) * LANES] = mo[c]
    for c in range(D_SGU // LANES):
        vn_ref[:, c * LANES:(c + 1) * LANES] = vo[c]


def _sample_mix(layer, z, cos_t, sin_t, params, ck, cv, state2, prev_outs):
    (sink_rows, pw_b, ps, lng, lnb, wexp, bexp) = params
    n_alias = len(prev_outs)
    const = lambda shape: pl.BlockSpec(shape, lambda i: (0,) * len(shape))
    in_specs = [
        pl.BlockSpec((RT, D_IN), lambda i: (i, 0)),
        const((RT, LANES)), const((RT, LANES)),
        _layer_spec((SCORE_ROWS, LANES), layer, 1),
        pl.BlockSpec((None, BT, WINDOW, D_KV), lambda i: (layer, i, 0, 0)),
        pl.BlockSpec((None, BT, WINDOW, D_KV), lambda i: (layer, i, 0, 0)),
        pl.BlockSpec((None, BT * POOL_HIST, D_POOL), lambda i: (layer, i, 0)),
        _layer_spec((len(POOL_WINDOWS), LANES, LANES), layer, 1),
        _layer_spec((1, D_POOL), layer, 1),
        _layer_spec((1, D_SGU), layer, 1),
        _layer_spec((1, D_SGU), layer, 1),
        _layer_spec((DEC_SEQ * DEC_SEQ, D_SGU), layer, 1),
        _layer_spec((DEC_SEQ, D_SGU), layer, 1),
    ] + [pl.BlockSpec(memory_space=pl.ANY)] * n_alias
    out_specs = [
        pl.BlockSpec((RT, D_MODEL), lambda i: (i, 0)),
        pl.BlockSpec((None, BT, WINDOW, D_KV), lambda i: (layer, i, 0, 0)),
        pl.BlockSpec((None, BT, WINDOW, D_KV), lambda i: (layer, i, 0, 0)),
        pl.BlockSpec((None, BT * POOL_HIST, D_POOL), lambda i: (layer, i, 0)),
        pl.BlockSpec((None, RT, D_SGU), lambda i: (layer, i, 0)),
    ]
    out_shape = [
        jax.ShapeDtypeStruct((SAMPLE_ROWS, D_MODEL), F32),
        jax.ShapeDtypeStruct((DEPTH, DEC_BATCH, WINDOW, D_KV), F32),
        jax.ShapeDtypeStruct((DEPTH, DEC_BATCH, WINDOW, D_KV), F32),
        jax.ShapeDtypeStruct((DEPTH, DEC_BATCH * POOL_HIST, D_POOL), F32),
        jax.ShapeDtypeStruct((DEPTH, SAMPLE_ROWS, D_SGU), F32),
    ]
    scratch = [
        pltpu.VMEM((N_HEADS, RT, D_KV), BF16),
        pltpu.VMEM((RT, D_KV), F32),
        pltpu.VMEM((RT, D_KV), F32),
        pltpu.VMEM((N_REST_BLOCKS, RT, LANES), F32),
        pltpu.VMEM((D_POOL // LANES, BT * POOL_HIST, LANES), F32),
        pltpu.VMEM((D_POOL // LANES, BT * POOL_HIST, LANES), F32),
        pltpu.VMEM(((D_POOL + D_SGU) // LANES, RT, LANES), F32),
        pltpu.VMEM((D_SGU // LANES, RT, LANES), F32),
    ]
    n_in = len(in_specs) - n_alias
    return pl.pallas_call(
        functools.partial(_sample_mix_body, n_alias),
        grid=(DEC_BATCH // BT,),
        in_specs=in_specs,
        out_specs=out_specs,
        out_shape=out_shape,
        scratch_shapes=scratch,
        input_output_aliases={n_in + i: 1 + i for i in range(n_alias)},
        compiler_params=pltpu.CompilerParams(
            dimension_semantics=("arbitrary",), vmem_limit_bytes=VMEM_LIMIT_SAMPLE),
        name="sample_mix",
    )(z, cos_t, sin_t, sink_rows, ck, cv, state2, pw_b, ps, lng, lnb, wexp, bexp, *prev_outs)


def _sample_out_body(x_ref, mix_ref, w_out_ref, npost_ref, y_ref):
    y = _dot(mix_ref[...].astype(BF16), w_out_ref[...])
    out = x_ref[...].reshape(SAMPLE_ROWS, D_MODEL) + _rms(y, npost_ref[...])
    y_ref[...] = out.reshape(y_ref.shape)


def _sample_out(layer, xs, mix, w_out_b, npost, out_shape):
    full = lambda shape: pl.BlockSpec(shape, lambda i: (0,) * len(shape))
    return pl.pallas_call(
        _sample_out_body,
        grid=(1,),
        in_specs=[
            full(xs.shape),
            full((SAMPLE_ROWS, D_MODEL)),
            pl.BlockSpec((None, D_MODEL, D_MODEL), lambda i: (layer, 0, 0)),
            pl.BlockSpec((None, 1, D_MODEL), lambda i: (layer, 0, 0)),
        ],
        out_specs=full(out_shape),
        out_shape=jax.ShapeDtypeStruct(out_shape, F32),
        compiler_params=pltpu.CompilerParams(
            dimension_semantics=("arbitrary",), vmem_limit_bytes=VMEM_LIMIT_SAMPLE),
        name="sample_out",
    )(xs, mix, w_out_b, npost)


def _rope_tables(pos):
    inv = ROPE_THETA ** (-jnp.arange(0, HEAD_DIM, 2, dtype=F32) / HEAD_DIM)
    ang = pos.astype(F32)[:, None] * inv[None, :]
    c, s = jnp.cos(ang), jnp.sin(ang)
    return jnp.concatenate([c, c, c, c], axis=1), jnp.concatenate([-s, s, -s, s], axis=1)


def kernel(x_prompt, x_sample, cache_k, cache_v, state_pool, w_in, w_out, norm_pre, norm_post,
           attn_sinks, pool_w, pool_scale, sgu_ln_g, sgu_ln_b, sgu_w, sgu_b):
    cos_p, sin_p = _rope_tables(jnp.arange(SEQ))
    cos_s, sin_s = _rope_tables(PAST_LEN + jnp.arange(DEC_SEQ))
    cos_t = jnp.tile(cos_s, (BT, 1))
    sin_t = jnp.tile(sin_s, (BT, 1))

    w_in_b = w_in.astype(BF16)
    w_out_b = w_out.astype(BF16)
    pw_b = pool_w.astype(BF16)
    npre = norm_pre[:, None, :]
    npost = norm_post[:, None, :]
    ps = pool_scale[:, None, :]
    lng = sgu_ln_g[:, None, :]
    lnb = sgu_ln_b[:, None, :]
    sb_full = jnp.broadcast_to(sgu_b[:, :, :, None], (DEPTH, N_SGU_HEADS, CHUNK, LANES))
    wexp = jnp.repeat(sgu_w[:, :, :DEC_SEQ, :DEC_SEQ].transpose(0, 2, 3, 1), LANES, axis=-1
                      ).reshape(DEPTH, DEC_SEQ * DEC_SEQ, D_SGU)
    bexp = jnp.repeat(sgu_b[:, :, :DEC_SEQ].transpose(0, 2, 1), LANES, axis=-1)
    sink_rows = jnp.broadcast_to(attn_sinks[:, :, None, None], (DEPTH, N_HEADS, GR, LANES)
                                 ).reshape(DEPTH, SCORE_ROWS, LANES)

    xp = x_prompt
    xs = x_sample
    ck = cache_k.reshape(DEPTH, DEC_BATCH, WINDOW, D_KV)
    cv = cache_v.reshape(DEPTH, DEC_BATCH, WINDOW, D_KV)
    state2 = state_pool.reshape(DEPTH, DEC_BATCH * POOL_HIST, D_POOL)
    prompt_params = (attn_sinks, w_in_b, w_out_b, npre, npost, pw_b, ps, lng, lnb, sgu_w, sb_full)
    sample_params = (sink_rows, pw_b, ps, lng, lnb, wexp, bexp)
    p_outs, s_outs = [], []
    for layer in range(DEPTH):
        xp, *p_outs = _prompt_layer(layer, xp, cos_p, sin_p, prompt_params, p_outs)
        z = _sample_proj(layer, xs, npre, w_in_b)
        mix, *s_outs = _sample_mix(layer, z, cos_t, sin_t, sample_params, ck, cv, state2, s_outs)
        xs_shape = x_sample.shape if layer == DEPTH - 1 else (SAMPLE_ROWS, D_MODEL)
        xs = _sample_out(layer, xs, mix, w_out_b, npost, xs_shape)

    nk_p, nv_p, np_p = p_outs
    nk_s, nv_s, np_s, vn_s = s_outs
    kv5 = lambda a, nb: a.reshape(DEPTH, nb, WINDOW, N_KV, HEAD_DIM)
    return (xp, xs,
            kv5(nk_p, BATCH), kv5(nv_p, BATCH), np_p,
            kv5(nk_s, DEC_BATCH), kv5(nv_s, DEC_BATCH),
            np_s.reshape(DEPTH, DEC_BATCH, POOL_HIST, D_POOL),
            vn_s.reshape(DEPTH, DEC_BATCH, DEC_SEQ, D_SGU))
```

```python
import functools

import jax
import jax.numpy as jnp
from jax import lax
from jax.experimental import pallas as pl
from jax.experimental.pallas import tpu as pltpu

D_MODEL = 2048
SEQ = 2048
BATCH = 8
DEPTH = 2
DEC_BATCH = 128
DEC_SEQ = 4
PAST_LEN = 8192
HEAD_DIM = 64
HALF_DIM = HEAD_DIM // 2
N_HEADS = 16
N_KV = 4
GROUP = N_HEADS // N_KV
D_ATT = N_HEADS * HEAD_DIM
D_KV = N_KV * HEAD_DIM
WINDOW = 128
ROPE_THETA = 10000.0
D_POOL = 512
POOL_WINDOWS = (2, 4, 8, 16)
POOL_HIST = 15
D_SGU = 512
CHUNK = 128
N_SGU_HEADS = 4
D_IN = 5120
EPS = 1e-6
ATT_SCALE = HEAD_DIM ** -0.5
LOG2_E = 1.4426950408889634

C_Q = 0
C_K = C_Q + D_ATT
C_V = C_K + D_KV
C_GA = C_V + D_KV
C_XB = C_GA + D_ATT
C_GB = C_XB + D_POOL
C_U = C_GB + D_POOL
C_VS = C_U + D_SGU
C_GC = C_VS + D_SGU
assert C_GC + D_SGU == D_IN

LANES = 128
SUBLANES = 8
HIST_PAD = 16
SUB = 256
N_SUB = 2
TM = SUB * N_SUB
N_CHUNKS = SUB // CHUNK
SAMPLE_ROWS = DEC_BATCH * DEC_SEQ
BT = 32
RT = BT * DEC_SEQ
assert RT == WINDOW
GB = 4
GR = GB * DEC_SEQ
ATT_PIECE = 512
REST_PIECE = 256
N_REST_PIECES = (D_IN - C_XB) // REST_PIECE
Z_U = D_POOL
Z_VS = Z_U + D_SGU
Z_WIDTH = Z_VS + D_SGU
S1_TN = 1024
VMEM_LIMIT_PROMPT = 62 * 1024 * 1024
VMEM_LIMIT_SAMPLE = 56 * 1024 * 1024

F32 = jnp.float32
BF16 = jnp.bfloat16
NEG_INF = float("-inf")


def _rms(x, g):
    return x * lax.rsqrt(jnp.mean(x * x, axis=-1, keepdims=True) + EPS) * g


def _silu(x):
    return x * (1.0 / (1.0 + jnp.exp(-x)))


def _layernorm(x, g, b):
    mu = jnp.mean(x, axis=-1, keepdims=True)
    xc = x - mu
    var = jnp.mean(xc * xc, axis=-1, keepdims=True)
    return xc * lax.rsqrt(var + EPS) * g + b


def _swap_heads(x):
    return pltpu.roll(x, HEAD_DIM, 1)


def _rope(x, cos, sin):
    rows = x.shape[0]
    lane = lax.broadcasted_iota(jnp.int32, (rows, LANES), 1)
    first_half = (lane & HALF_DIM) == 0
    outs = []
    for c in range(x.shape[1] // LANES):
        xc = x[:, c * LANES:(c + 1) * LANES]
        partner = jnp.where(first_half,
                            pltpu.roll(xc, LANES - HALF_DIM, 1),
                            pltpu.roll(xc, HALF_DIM, 1))
        outs.append(xc * cos + partner * sin)
    return jnp.concatenate(outs, axis=1)


def _dot(a, b):
    return jnp.dot(a, b, preferred_element_type=F32)


def _dot_nt(a, b):
    return lax.dot_general(a, b, (((1,), (1,)), ((), ())), preferred_element_type=F32)


def _prompt_body(layer, n_alias, sinks_ref, x_ref, cos_ref, sin_ref, w_in_ref, w_out_ref,
                 npre_ref, npost_ref, pw_ref, ps_ref, lng_ref, lnb_ref, sw_ref, sb_ref, *rest):
    refs = rest[n_alias:]
    kbuf, vbuf, xbext = refs[5], refs[6], refs[7]
    j = pl.program_id(1)

    @pl.when(j == 0)
    def _start_of_sequence():
        kbuf[:, 0:WINDOW, :] = jnp.zeros((2 * N_KV, WINDOW, LANES), BF16)
        vbuf[:, 0:WINDOW, :] = jnp.zeros((N_KV, WINDOW, D_KV), BF16)
        xbext[0:HIST_PAD, :] = jnp.zeros((HIST_PAD, D_POOL), F32)

    @pl.when(j > 0)
    def _carry_from_previous_step():
        for i in range(2 * N_KV):
            kbuf[i, 0:WINDOW, :] = kbuf[i, TM:TM + WINDOW, :]
        for g in range(N_KV):
            vbuf[g, 0:WINDOW, :] = vbuf[g, TM:TM + WINDOW, :]
        xbext[0:HIST_PAD, :] = xbext[TM:TM + HIST_PAD, :]

    for sub in range(N_SUB):
        _prompt_sub_block(layer, sub, j, sinks_ref, x_ref, cos_ref, sin_ref, w_in_ref, w_out_ref,
                          npre_ref, npost_ref, pw_ref, ps_ref, lng_ref, lnb_ref, sw_ref, sb_ref, *refs)


def _prompt_sub_block(layer, sub, j, sinks_ref, x_ref, cos_ref, sin_ref, w_in_ref, w_out_ref,
                      npre_ref, npost_ref, pw_ref, ps_ref, lng_ref, lnb_ref, sw_ref, sb_ref,
                      y_ref, nk_ref, nv_ref, np_ref, qbuf, kbuf, vbuf, xbext, mixbuf, zrest, ybuf):
    base = sub * SUB
    sub_rows = slice(base, base + SUB)
    first_in_sequence = (j == 0) if sub == 0 else None
    h = _rms(x_ref[0, sub_rows], npre_ref[...]).astype(BF16)

    def proj(lo, width):
        return _dot(h, w_in_ref[:, lo:lo + width])

    cos = cos_ref[sub_rows, :]
    sin = sin_ref[sub_rows, :]

    k = _rope(proj(C_K, D_KV), cos, sin)
    v = proj(C_V, D_KV)
    cos_q = cos * (ATT_SCALE * LOG2_E)
    sin_q = sin * (ATT_SCALE * LOG2_E)
    for part in range(D_ATT // ATT_PIECE):
        cols = slice(part * ATT_PIECE, (part + 1) * ATT_PIECE)
        qbuf[:, cols] = _rope(proj(C_Q + part * ATT_PIECE, ATT_PIECE), cos_q, sin_q).astype(BF16)
    for part in range(D_ATT // ATT_PIECE):
        cols = slice(part * ATT_PIECE, (part + 1) * ATT_PIECE)
        mixbuf[:, cols] = _silu(proj(C_GA + part * ATT_PIECE, ATT_PIECE)).astype(BF16)
    if sub == N_SUB - 1:
        nk_ref[0] = k[SUB - WINDOW:SUB]
        nv_ref[0] = v[SUB - WINDOW:SUB]
    lane_k = lax.broadcasted_iota(jnp.int32, (SUB, LANES), 1)
    low = lane_k < HEAD_DIM
    lane_v = lax.broadcasted_iota(jnp.int32, (SUB, D_KV), 1)
    def store_by_chunk(buf, idx, val):
        val = val.astype(BF16)
        for c in range(N_CHUNKS):
            r0 = WINDOW + base + c * CHUNK
            buf[idx, r0:r0 + CHUNK, :] = val[c * CHUNK:(c + 1) * CHUNK]

    for pair in range(2):
        kp = k[:, pair * LANES:(pair + 1) * LANES]
        ks = _swap_heads(kp)
        store_by_chunk(kbuf, 4 * pair + 0, jnp.where(low, kp, 0.0))
        store_by_chunk(kbuf, 4 * pair + 1, jnp.where(low, 0.0, ks))
        store_by_chunk(kbuf, 4 * pair + 2, jnp.where(low, ks, 0.0))
        store_by_chunk(kbuf, 4 * pair + 3, jnp.where(low, 0.0, kp))
    for g in range(N_KV):
        keep_v = (lane_v >= g * HEAD_DIM) & (lane_v < (g + 1) * HEAD_DIM)
        store_by_chunk(vbuf, g, jnp.where(keep_v, v, 0.0))

    qi = lax.broadcasted_iota(jnp.int32, (CHUNK, 2 * WINDOW), 0)
    kj = lax.broadcasted_iota(jnp.int32, (CHUNK, 2 * WINDOW), 1)
    dist = qi + WINDOW - kj
    band_bias = jnp.where((dist >= 0) & (dist < WINDOW), 0.0, NEG_INF)
    if first_in_sequence is None:
        first_bias = band_bias
    else:
        no_prev = jnp.where(first_in_sequence, WINDOW, 0)
        first_bias = jnp.where(kj < no_prev, NEG_INF, band_bias)
    low_c = lax.broadcasted_iota(jnp.int32, (CHUNK, LANES), 1) < HEAD_DIM
    rest_piece = 0

    def project_rest_piece(piece):
        cols = slice(piece * REST_PIECE, (piece + 1) * REST_PIECE)
        lo = C_XB + piece * REST_PIECE
        val = proj(lo, REST_PIECE)
        if C_GB <= lo < C_U:
            mixbuf[:, D_ATT + lo - C_GB:D_ATT + lo - C_GB + REST_PIECE] = _silu(val).astype(BF16)
        elif lo >= C_GC:
            at = D_ATT + D_POOL + lo - C_GC
            mixbuf[:, at:at + REST_PIECE] = _silu(val).astype(BF16)
        else:
            at = lo - C_XB if lo < C_GB else (Z_U + lo - C_U if lo < C_VS else Z_VS + lo - C_VS)
            zrest[:, at:at + REST_PIECE] = val

    def scores(c, g):
        r0 = c * CHUNK
        qs = jnp.concatenate(
            [qbuf[r0:r0 + CHUNK, (2 * g + i) * LANES:(2 * g + i + 1) * LANES] for i in range(2)],
            axis=0)
        return [_dot_nt(qs, kbuf[2 * g + half, base + r0:base + r0 + 2 * WINDOW, :]) for half in range(2)]

    order = [(c, g) for c in range(N_CHUNKS) for g in range(N_KV)]
    while rest_piece < N_REST_PIECES - len(order):
        project_rest_piece(rest_piece)
        rest_piece += 1
    s_next = scores(*order[0])
    for step, (c, g) in enumerate(order):
        bias = first_bias if c == 0 else band_bias
        r0 = c * CHUNK
        s_half = s_next
        if step + 1 < len(order):
            s_next = scores(*order[step + 1])
        blocks = []
        for r in range(GROUP):
            sb = s_half[r % 2][(r // 2) * CHUNK:(r // 2 + 1) * CHUNK] + bias
            sink = sinks_ref[layer, GROUP * g + r] * LOG2_E
            m = jnp.maximum(jnp.max(sb, axis=1, keepdims=True), sink)
            pe = jnp.exp2(sb - m)
            den = jnp.sum(pe, axis=1, keepdims=True) + jnp.exp2(sink - m)
            blocks.append((pe * (1.0 / den)).astype(BF16))
        pg = jnp.concatenate(blocks, axis=0)
        og = _dot(pg, vbuf[g, base + r0:base + r0 + 2 * WINDOW, :])
        o = og if g == 0 else o + og
        if rest_piece < N_REST_PIECES:
            project_rest_piece(rest_piece)
            rest_piece += 1
        if g < N_KV - 1:
            continue
        for pair in range(2):
            for i in range(2):
                a = o[(2 * i) * CHUNK:(2 * i + 1) * CHUNK, pair * LANES:(pair + 1) * LANES]
                b = o[(2 * i + 1) * CHUNK:(2 * i + 2) * CHUNK, pair * LANES:(pair + 1) * LANES]
                for odd, blk in ((0, jnp.where(low_c, a, _swap_heads(b))),
                                 (1, jnp.where(low_c, _swap_heads(a), b))):
                    cols = slice((2 * (2 * pair + odd) + i) * LANES, (2 * (2 * pair + odd) + i + 1) * LANES)
                    mixbuf[r0:r0 + CHUNK, cols] = (blk * mixbuf[r0:r0 + CHUNK, cols]).astype(BF16)

    while rest_piece < N_REST_PIECES:
        project_rest_piece(rest_piece)
        rest_piece += 1

    xb0 = HIST_PAD + base
    xbext[xb0:xb0 + SUB, :] = zrest[:, 0:D_POOL]
    if sub == N_SUB - 1:
        np_ref[0] = xbext[HIST_PAD + TM - POOL_HIST:HIST_PAD + TM, :]
    pos1 = j * TM + base + lax.broadcasted_iota(jnp.int32, (SUB, LANES), 0) + 1
    for gi, w in enumerate(POOL_WINDOWS):
        cols = slice(gi * LANES, (gi + 1) * LANES)
        cur = xbext[xb0:xb0 + SUB, cols]
        acc = cur
        for i in range(1, w):
            acc = acc + xbext[xb0 - i:xb0 - i + SUB, cols]
        cnt = jnp.minimum(w, pos1).astype(F32)
        pooled = acc / cnt - cur
        mixed = _dot(pooled.astype(BF16), pw_ref[gi]) * ps_ref[:, cols]
        out_cols = slice(D_ATT + gi * LANES, D_ATT + (gi + 1) * LANES)
        mixbuf[:, out_cols] = (mixed * mixbuf[:, out_cols]).astype(BF16)

    vn = _layernorm(zrest[:, Z_VS:Z_VS + D_SGU], lng_ref[...], lnb_ref[...]).astype(BF16)
    ti = lax.broadcasted_iota(jnp.int32, (CHUNK, CHUNK), 0)
    si = lax.broadcasted_iota(jnp.int32, (CHUNK, CHUNK), 1)
    for hh in range(N_SGU_HEADS):
        cols = slice(hh * LANES, (hh + 1) * LANES)
        wm = jnp.where(ti >= si, sw_ref[hh], 0.0).astype(BF16)
        for c in range(N_CHUNKS):
            rows = slice(c * CHUNK, (c + 1) * CHUNK)
            mixed = _dot(wm, vn[rows, cols]) + sb_ref[hh]
            out_cols = slice(D_ATT + D_POOL + hh * LANES, D_ATT + D_POOL + (hh + 1) * LANES)
            mixbuf[rows, out_cols] = (
                zrest[rows, Z_U + hh * LANES:Z_U + (hh + 1) * LANES] * mixed * mixbuf[rows, out_cols]
            ).astype(BF16)

    slot = jnp.minimum(j, 0)
    y = _dot(mixbuf[...], w_out_ref[...])
    ybuf[slot] = y
    if sub < N_SUB - 1:
        y = ybuf[slot]
    y_ref[0, sub_rows] = x_ref[0, sub_rows] + _rms(y, npost_ref[...])


def _layer_spec(shape, layer, grid_rank):
    zeros = (0,) * len(shape)
    if grid_rank == 2:
        index_map = lambda b, j: (layer,) + zeros
    else:
        index_map = lambda i: (layer,) + zeros
    return pl.BlockSpec((None,) + tuple(shape), index_map, pipeline_mode=pl.Buffered(1))


def _prompt_layer(layer, x, cos, sin, params, prev_outs):
    (sinks, w_in_b, w_out_b, npre, npost, pw_b, ps, lng, lnb, sw, sb_full) = params
    grid = (BATCH, SEQ // TM)
    n_alias = len(prev_outs)
    in_specs = [
        pl.BlockSpec(memory_space=pltpu.SMEM),
        pl.BlockSpec((1, TM, D_MODEL), lambda b, j: (b, j, 0)),
        pl.BlockSpec((TM, LANES), lambda b, j: (j, 0)),
        pl.BlockSpec((TM, LANES), lambda b, j: (j, 0)),
        _layer_spec((D_MODEL, D_IN), layer, 2),
        _layer_spec((D_MODEL, D_MODEL), layer, 2),
        _layer_spec((1, D_MODEL), layer, 2),
        _layer_spec((1, D_MODEL), layer, 2),
        _layer_spec((len(POOL_WINDOWS), LANES, LANES), layer, 2),
        _layer_spec((1, D_POOL), layer, 2),
        _layer_spec((1, D_SGU), layer, 2),
        _layer_spec((1, D_SGU), layer, 2),
        _layer_spec((N_SGU_HEADS, CHUNK, CHUNK), layer, 2),
        _layer_spec((N_SGU_HEADS, CHUNK, LANES), layer, 2),
    ] + [pl.BlockSpec(memory_space=pl.ANY)] * n_alias
    out_specs = [
        pl.BlockSpec((1, TM, D_MODEL), lambda b, j: (b, j, 0)),
        pl.BlockSpec((None, 1, WINDOW, D_KV), lambda b, j: (layer, b, 0, 0)),
        pl.BlockSpec((None, 1, WINDOW, D_KV), lambda b, j: (layer, b, 0, 0)),
        pl.BlockSpec((None, 1, POOL_HIST, D_POOL), lambda b, j: (layer, b, 0, 0)),
    ]
    out_shape = [
        jax.ShapeDtypeStruct((BATCH, SEQ, D_MODEL), F32),
        jax.ShapeDtypeStruct((DEPTH, BATCH, WINDOW, D_KV), F32),
        jax.ShapeDtypeStruct((DEPTH, BATCH, WINDOW, D_KV), F32),
        jax.ShapeDtypeStruct((DEPTH, BATCH, POOL_HIST, D_POOL), F32),
    ]
    scratch = [
        pltpu.VMEM((SUB, D_ATT), BF16),
        pltpu.VMEM((2 * N_KV, WINDOW + TM, LANES), BF16),
        pltpu.VMEM((N_KV, WINDOW + TM, D_KV), BF16),
        pltpu.VMEM((HIST_PAD + TM, D_POOL), F32),
        pltpu.VMEM((SUB, D_MODEL), BF16),
        pltpu.VMEM((SUB, Z_WIDTH), F32),
        pltpu.VMEM((1, SUB, D_MODEL), F32),
    ]
    n_in = len(in_specs) - n_alias
    return pl.pallas_call(
        functools.partial(_prompt_body, layer, n_alias),
        grid=grid,
        in_specs=in_specs,
        out_specs=out_specs,
        out_shape=out_shape,
        scratch_shapes=scratch,
        input_output_aliases={n_in + i: 1 + i for i in range(n_alias)},
        compiler_params=pltpu.CompilerParams(
            dimension_semantics=("arbitrary", "arbitrary"),
            vmem_limit_bytes=VMEM_LIMIT_PROMPT),
        name="prompt_layer",
    )(sinks, x, cos, sin, w_in_b, w_out_b, npre, npost, pw_b, ps, lng, lnb, sw, sb_full, *prev_outs)


def _sample_proj_body(x_ref, npre_ref, w_ref, z_ref, hbuf):
    @pl.when(pl.program_id(0) == 0)
    def _norm_once():
        hbuf[...] = _rms(x_ref[...].reshape(SAMPLE_ROWS, D_MODEL), npre_ref[...]).astype(BF16)

    z_ref[...] = _dot(hbuf[...], w_ref[...])


def _sample_proj(layer, xs, npre, w_in_b):
    return pl.pallas_call(
        _sample_proj_body,
        grid=(D_IN // S1_TN,),
        in_specs=[
            pl.BlockSpec(xs.shape, lambda n: (0,) * xs.ndim),
            pl.BlockSpec((None, 1, D_MODEL), lambda n: (layer, 0, 0)),
            pl.BlockSpec((None, D_MODEL, S1_TN), lambda n: (layer, 0, n)),
        ],
        out_specs=pl.BlockSpec((SAMPLE_ROWS, S1_TN), lambda n: (0, n)),
        out_shape=jax.ShapeDtypeStruct((SAMPLE_ROWS, D_IN), F32),
        scratch_shapes=[pltpu.VMEM((SAMPLE_ROWS, D_MODEL), BF16)],
        compiler_params=pltpu.CompilerParams(
            dimension_semantics=("arbitrary",), vmem_limit_bytes=VMEM_LIMIT_SAMPLE),
        name="sample_proj",
    )(xs, npre, w_in_b)


SCORE_ROWS = N_HEADS * GR
N_REST_BLOCKS = (D_IN - C_XB) // LANES


def _sample_mix_body(n_alias, z_ref, cos_ref, sin_ref, sink_ref, ck_ref, cv_ref, st_ref,
                     pw_ref, ps_ref, lng_ref, lnb_ref, wexp_ref, bexp_ref, *rest):
    (mix_ref, nk_ref, nv_ref, npool_ref, vn_ref,
     qbig, knew_f, vnew_f, knew_t, vnew_t, zc, sc, npc, mo, vo) = rest[n_alias:]
    cos = cos_ref[...]
    sin = sin_ref[...]

    q = _rope(z_ref[:, C_Q:C_Q + D_ATT], cos * ATT_SCALE, sin * ATT_SCALE)
    k_new = _rope(z_ref[:, C_K:C_K + D_KV], cos, sin)
    v_new = z_ref[:, C_V:C_V + D_KV]
    knew_f[...] = k_new
    vnew_f[...] = v_new
    knew_t[...] = k_new.T
    vnew_t[...] = v_new.T
    low = lax.broadcasted_iota(jnp.int32, (RT, LANES), 1) < HEAD_DIM
    zero_blk = jnp.zeros((RT, LANES), F32)
    for h in range(N_HEADS):
        g = h // GROUP
        src = q[:, (h // 2) * LANES:(h // 2 + 1) * LANES]
        if h % 2 != g % 2:
            src = _swap_heads(src)
        piece = jnp.where(low, src, 0.0) if g % 2 == 0 else jnp.where(low, 0.0, src)
        full = [zero_blk, zero_blk]
        full[g // 2] = piece
        qbig[h] = jnp.concatenate(full, axis=1).astype(BF16)

    row_h = lax.broadcasted_iota(jnp.int32, (SCORE_ROWS, GB * WINDOW), 0)
    col_h = lax.broadcasted_iota(jnp.int32, (SCORE_ROWS, GB * WINDOW), 1)
    same_h = ((row_h >> 2) & (GB - 1)) == (col_h >> 7)
    bias_h = jnp.where(same_h, jnp.where((col_h & (WINDOW - 1)) > (row_h & (DEC_SEQ - 1)), 0.0, NEG_INF),
                       NEG_INF)
    row_n = lax.broadcasted_iota(jnp.int32, (SCORE_ROWS, GR), 0)
    col_n = lax.broadcasted_iota(jnp.int32, (SCORE_ROWS, GR), 1)
    same_n = ((row_n >> 2) & (GB - 1)) == (col_n >> 2)
    bias_n = jnp.where(same_n, jnp.where((col_n & (DEC_SEQ - 1)) <= (row_n & (DEC_SEQ - 1)), 0.0, NEG_INF),
                       NEG_INF)
    sink = sink_ref[:, 0:1]
    low_g = lax.broadcasted_iota(jnp.int32, (GR, LANES), 1) < HEAD_DIM
    key_lane = lax.broadcasted_iota(jnp.int32, (D_KV, WINDOW), 1)

    def roll_in(hist_t, new_t, bi):
        rolled = pltpu.roll(hist_t, WINDOW - DEC_SEQ, 1)
        placed = pltpu.roll(new_t, (WINDOW - DEC_SEQ - DEC_SEQ * bi) % RT, 1)
        return jnp.where(key_lane >= WINDOW - DEC_SEQ, placed, rolled)

    for grp in range(BT // GB):
        r0 = grp * GR
        lhs = qbig[:, r0:r0 + GR, :].reshape(SCORE_ROWS, D_KV)
        kh_t = jnp.concatenate([ck_ref[grp * GB + bb] for bb in range(GB)], axis=1)
        vh_t = jnp.concatenate([cv_ref[grp * GB + bb] for bb in range(GB)], axis=1)
        kn = knew_f[r0:r0 + GR, :]
        vn_new = vnew_f[r0:r0 + GR, :]
        s_h = _dot(lhs, kh_t.astype(BF16)) + bias_h
        s_n = _dot_nt(lhs, kn.astype(BF16)) + bias_n
        m = jnp.maximum(jnp.maximum(jnp.max(s_h, axis=1, keepdims=True),
                                    jnp.max(s_n, axis=1, keepdims=True)), sink)
        p_h = jnp.exp(s_h - m)
        p_n = jnp.exp(s_n - m)
        den = (jnp.sum(p_h, axis=1, keepdims=True) + jnp.sum(p_n, axis=1, keepdims=True)
               + jnp.exp(sink - m))
        inv = 1.0 / den
        o = (_dot_nt((p_h * inv).astype(BF16), vh_t.astype(BF16))
             + _dot((p_n * inv).astype(BF16), vn_new.astype(BF16)))
        for c in range(N_HEADS // 2):
            g = c // 2
            a = o[(2 * c) * GR:(2 * c + 1) * GR, (g // 2) * LANES:(g // 2 + 1) * LANES]
            b = o[(2 * c + 1) * GR:(2 * c + 2) * GR, (g // 2) * LANES:(g // 2 + 1) * LANES]
            blk = jnp.where(low_g, a, _swap_heads(b)) if g % 2 == 0 else jnp.where(low_g, _swap_heads(a), b)
            gate = _silu(z_ref[r0:r0 + GR, C_GA + c * LANES:C_GA + (c + 1) * LANES])
            mix_ref[r0:r0 + GR, c * LANES:(c + 1) * LANES] = blk * gate
        for bb in range(GB):
            bi = grp * GB + bb
            nk_ref[bi] = roll_in(ck_ref[bi], knew_t[...], bi)
            nv_ref[bi] = roll_in(cv_ref[bi], vnew_t[...], bi)

    for c in range(N_REST_BLOCKS):
        zc[c] = z_ref[:, C_XB + c * LANES:C_XB + (c + 1) * LANES]
    for c in range(D_POOL // LANES):
        sc[c] = st_ref[:, c * LANES:(c + 1) * LANES]

    def tok(t, lo, width):
        blk0 = (lo - C_XB) // LANES
        parts = [zc[blk0 + c, pl.ds(t, BT, stride=DEC_SEQ), :] for c in range(width // LANES)]
        return parts[0] if len(parts) == 1 else jnp.concatenate(parts, axis=1)

    def ext(i, gi):
        if i < POOL_HIST:
            return sc[gi, pl.ds(i, BT, stride=POOL_HIST), :]
        return tok(i - POOL_HIST, C_XB + gi * LANES, LANES)

    for gi, w in enumerate(POOL_WINDOWS):
        for s in range(POOL_HIST):
            npc[gi, pl.ds(s, BT, stride=POOL_HIST), :] = ext(s + DEC_SEQ, gi)
        pooled = []
        for t in range(DEC_SEQ):
            cur = ext(POOL_HIST + t, gi)
            acc = cur
            for i in range(1, w):
                acc = acc + ext(POOL_HIST + t - i, gi)
            cnt = float(min(w, PAST_LEN + t + 1))
            pooled.append(acc / cnt - cur)
        pooled = jnp.concatenate(pooled, axis=0).astype(BF16)
        mixed = _dot(pooled, pw_ref[gi]) * ps_ref[:, gi * LANES:(gi + 1) * LANES]
        for t in range(DEC_SEQ):
            gate = _silu(tok(t, C_GB + gi * LANES, LANES))
            mo[gi, pl.ds(t, BT, stride=DEC_SEQ), :] = mixed[t * BT:(t + 1) * BT] * gate

    vns = []
    for t in range(DEC_SEQ):
        vn_t = _layernorm(tok(t, C_VS, D_SGU), lng_ref[...], lnb_ref[...])
        for c in range(D_SGU // LANES):
            vo[c, pl.ds(t, BT, stride=DEC_SEQ), :] = vn_t[:, c * LANES:(c + 1) * LANES]
        vns.append(vn_t)
    for t in range(DEC_SEQ):
        mixed = bexp_ref[t:t + 1, :]
        for s in range(t + 1):
            mixed = mixed + wexp_ref[DEC_SEQ * t + s:DEC_SEQ * t + s + 1, :] * vns[s]
        out = tok(t, C_U, D_SGU) * mixed * _silu(tok(t, C_GC, D_SGU))
        for c in range(D_SGU // LANES):
            mo[D_POOL // LANES + c, pl.ds(t, BT, stride=DEC_SEQ), :] = out[:, c * LANES:(c + 1) * LANES]

    for c in range(D_POOL // LANES):
        npool_ref[:, c * LANES:(c + 1) * LANES] = npc[c]
    for c in range((D_POOL + D_SGU) // LANES):
        mix_ref[:, D_ATT + c * LANES:D_ATT + (c + 1) * LANES] = mo[c]
    for c in range(D_SGU // LANES):
        vn_ref[:, c * LANES:(c + 1) * LANES] = vo[c]


def _sample_mix(layer, z, cos_t, sin_t, params, ck, cv, state2, prev_outs):
    (sink_rows, pw_b, ps, lng, lnb, wexp, bexp) = params
    n_alias = len(prev_outs)
    const = lambda shape: pl.BlockSpec(shape, lambda i: (0,) * len(shape))
    in_specs = [
        pl.BlockSpec((RT, D_IN), lambda i: (i, 0)),
        const((RT, LANES)), const((RT, LANES)),
        _layer_spec((SCORE_ROWS, LANES), layer, 1),
        pl.BlockSpec((None, BT, D_KV, WINDOW), lambda i: (layer, i, 0, 0)),
        pl.BlockSpec((None, BT, D_KV, WINDOW), lambda i: (layer, i, 0, 0)),
        pl.BlockSpec((None, BT * POOL_HIST, D_POOL), lambda i: (layer, i, 0)),
        _layer_spec((len(POOL_WINDOWS), LANES, LANES), layer, 1),
        _layer_spec((1, D_POOL), layer, 1),
        _layer_spec((1, D_SGU), layer, 1),
        _layer_spec((1, D_SGU), layer, 1),
        _layer_spec((DEC_SEQ * DEC_SEQ, D_SGU), layer, 1),
        _layer_spec((DEC_SEQ, D_SGU), layer, 1),
    ] + [pl.BlockSpec(memory_space=pl.ANY)] * n_alias
    out_specs = [
        pl.BlockSpec((RT, D_MODEL), lambda i: (i, 0)),
        pl.BlockSpec((None, BT, D_KV, WINDOW), lambda i: (layer, i, 0, 0)),
        pl.BlockSpec((None, BT, D_KV, WINDOW), lambda i: (layer, i, 0, 0)),
        pl.BlockSpec((None, BT * POOL_HIST, D_POOL), lambda i: (layer, i, 0)),
        pl.BlockSpec((None, RT, D_SGU), lambda i: (layer, i, 0)),
    ]
    out_shape = [
        jax.ShapeDtypeStruct((SAMPLE_ROWS, D_MODEL), F32),
        jax.ShapeDtypeStruct((DEPTH, DEC_BATCH, D_KV, WINDOW), F32),
        jax.ShapeDtypeStruct((DEPTH, DEC_BATCH, D_KV, WINDOW), F32),
        jax.ShapeDtypeStruct((DEPTH, DEC_BATCH * POOL_HIST, D_POOL), F32),
        jax.ShapeDtypeStruct((DEPTH, SAMPLE_ROWS, D_SGU), F32),
    ]
    scratch = [
        pltpu.VMEM((N_HEADS, RT, D_KV), BF16),
        pltpu.VMEM((RT, D_KV), F32),
        pltpu.VMEM((RT, D_KV), F32),
        pltpu.VMEM((D_KV, RT), F32),
        pltpu.VMEM((D_KV, RT), F32),
        pltpu.VMEM((N_REST_BLOCKS, RT, LANES), F32),
        pltpu.VMEM((D_POOL // LANES, BT * POOL_HIST, LANES), F32),
        pltpu.VMEM((D_POOL // LANES, BT * POOL_HIST, LANES), F32),
        pltpu.VMEM(((D_POOL + D_SGU) // LANES, RT, LANES), F32),
        pltpu.VMEM((D_SGU // LANES, RT, LANES), F32),
    ]
    n_in = len(in_specs) - n_alias
    return pl.pallas_call(
        functools.partial(_sample_mix_body, n_alias),
        grid=(DEC_BATCH // BT,),
        in_specs=in_specs,
        out_specs=out_specs,
        out_shape=out_shape,
        scratch_shapes=scratch,
        input_output_aliases={n_in + i: 1 + i for i in range(n_alias)},
        compiler_params=pltpu.CompilerParams(
            dimension_semantics=("arbitrary",), vmem_limit_bytes=VMEM_LIMIT_SAMPLE),
        name="sample_mix",
    )(z, cos_t, sin_t, sink_rows, ck, cv, state2, pw_b, ps, lng, lnb, wexp, bexp, *prev_outs)


def _sample_out_body(x_ref, mix_ref, w_out_ref, npost_ref, y_ref):
    y = _dot(mix_ref[...].astype(BF16), w_out_ref[...])
    out = x_ref[...].reshape(SAMPLE_ROWS, D_MODEL) + _rms(y, npost_ref[...])
    y_ref[...] = out.reshape(y_ref.shape)


def _sample_out(layer, xs, mix, w_out_b, npost, out_shape):
    full = lambda shape: pl.BlockSpec(shape, lambda i: (0,) * len(shape))
    return pl.pallas_call(
        _sample_out_body,
        grid=(1,),
        in_specs=[
            full(xs.shape),
            full((SAMPLE_ROWS, D_MODEL)),
            pl.BlockSpec((None, D_MODEL, D_MODEL), lambda i: (layer, 0, 0)),
            pl.BlockSpec((None, 1, D_MODEL), lambda i: (layer, 0, 0)),
        ],
        out_specs=full(out_shape),
        out_shape=jax.ShapeDtypeStruct(out_shape, F32),
        compiler_params=pltpu.CompilerParams(
            dimension_semantics=("arbitrary",), vmem_limit_bytes=VMEM_LIMIT_SAMPLE),
        name="sample_out",
    )(xs, mix, w_out_b, npost)


def _rope_tables(pos):
    inv = ROPE_THETA ** (-jnp.arange(0, HEAD_DIM, 2, dtype=F32) / HEAD_DIM)
    ang = pos.astype(F32)[:, None] * inv[None, :]
    c, s = jnp.cos(ang), jnp.sin(ang)
    return jnp.concatenate([c, c, c, c], axis=1), jnp.concatenate([-s, s, -s, s], axis=1)


def kernel(x_prompt, x_sample, cache_k, cache_v, state_pool, w_in, w_out, norm_pre, norm_post,
           attn_sinks, pool_w, pool_scale, sgu_ln_g, sgu_ln_b, sgu_w, sgu_b):
    cos_p, sin_p = _rope_tables(jnp.arange(SEQ))
    cos_s, sin_s = _rope_tables(PAST_LEN + jnp.arange(DEC_SEQ))
    cos_t = jnp.tile(cos_s, (BT, 1))
    sin_t = jnp.tile(sin_s, (BT, 1))

    w_in_b = w_in.astype(BF16)
    w_out_b = w_out.astype(BF16)
    pw_b = pool_w.astype(BF16)
    npre = norm_pre[:, None, :]
    npost = norm_post[:, None, :]
    ps = pool_scale[:, None, :]
    lng = sgu_ln_g[:, None, :]
    lnb = sgu_ln_b[:, None, :]
    sb_full = jnp.broadcast_to(sgu_b[:, :, :, None], (DEPTH, N_SGU_HEADS, CHUNK, LANES))
    wexp = jnp.repeat(sgu_w[:, :, :DEC_SEQ, :DEC_SEQ].transpose(0, 2, 3, 1), LANES, axis=-1
                      ).reshape(DEPTH, DEC_SEQ * DEC_SEQ, D_SGU)
    bexp = jnp.repeat(sgu_b[:, :, :DEC_SEQ].transpose(0, 2, 1), LANES, axis=-1)
    sink_rows = jnp.broadcast_to(attn_sinks[:, :, None, None], (DEPTH, N_HEADS, GR, LANES)
                                 ).reshape(DEPTH, SCORE_ROWS, LANES)

    xp = x_prompt
    xs = x_sample
    to_key_minor = lambda c: c.transpose(0, 1, 3, 4, 2).reshape(DEPTH, DEC_BATCH, D_KV, WINDOW)
    from_key_minor = lambda c: c.reshape(DEPTH, DEC_BATCH, N_KV, HEAD_DIM, WINDOW).transpose(0, 1, 4, 2, 3)
    ck = to_key_minor(cache_k)
    cv = to_key_minor(cache_v)
    state2 = state_pool.reshape(DEPTH, DEC_BATCH * POOL_HIST, D_POOL)
    prompt_params = (attn_sinks, w_in_b, w_out_b, npre, npost, pw_b, ps, lng, lnb, sgu_w, sb_full)
    sample_params = (sink_rows, pw_b, ps, lng, lnb, wexp, bexp)
    p_outs, s_outs = [], []
    for layer in range(DEPTH):
        xp, *p_outs = _prompt_layer(layer, xp, cos_p, sin_p, prompt_params, p_outs)
        z = _sample_proj(layer, xs, npre, w_in_b)
        mix, *s_outs = _sample_mix(layer, z, cos_t, sin_t, sample_params, ck, cv, state2, s_outs)
        xs_shape = x_sample.shape if layer == DEPTH - 1 else (SAMPLE_ROWS, D_MODEL)
        xs = _sample_out(layer, xs, mix, w_out_b, npost, xs_shape)

    nk_p, nv_p, np_p = p_outs
    nk_s, nv_s, np_s, vn_s = s_outs
    kv5 = lambda a, nb: a.reshape(DEPTH, nb, WINDOW, N_KV, HEAD_DIM)
    return (xp, xs,
            kv5(nk_p, BATCH), kv5(nv_p, BATCH), np_p,
            from_key_minor(nk_s), from_key_minor(nv_s),
            np_s.reshape(DEPTH, DEC_BATCH, POOL_HIST, D_POOL),
            vn_s.reshape(DEPTH, DEC_BATCH, DEC_SEQ, D_SGU))
```

```python
import functools

import jax
import jax.numpy as jnp
from jax import lax
from jax.experimental import pallas as pl
from jax.experimental.pallas import tpu as pltpu

D_MODEL = 2048
SEQ = 2048
BATCH = 8
DEPTH = 2
DEC_BATCH = 128
DEC_SEQ = 4
PAST_LEN = 8192
HEAD_DIM = 64
HALF_DIM = HEAD_DIM // 2
N_HEADS = 16
N_KV = 4
GROUP = N_HEADS // N_KV
D_ATT = N_HEADS * HEAD_DIM
D_KV = N_KV * HEAD_DIM
WINDOW = 128
ROPE_THETA = 10000.0
D_POOL = 512
POOL_WINDOWS = (2, 4, 8, 16)
POOL_HIST = 15
D_SGU = 512
CHUNK = 128
N_SGU_HEADS = 4
D_IN = 5120
EPS = 1e-6
ATT_SCALE = HEAD_DIM ** -0.5
LOG2_E = 1.4426950408889634

C_Q = 0
C_K = C_Q + D_ATT
C_V = C_K + D_KV
C_GA = C_V + D_KV
C_XB = C_GA + D_ATT
C_GB = C_XB + D_POOL
C_U = C_GB + D_POOL
C_VS = C_U + D_SGU
C_GC = C_VS + D_SGU
assert C_GC + D_SGU == D_IN

LANES = 128
SUBLANES = 8
HIST_PAD = 16
SUB = 256
N_SUB = 2
TM = SUB * N_SUB
N_CHUNKS = SUB // CHUNK
SAMPLE_ROWS = DEC_BATCH * DEC_SEQ
BT = 32
RT = BT * DEC_SEQ
assert RT == WINDOW
GB = 4
GR = GB * DEC_SEQ
ATT_PIECE = 512
REST_PIECE = 256
N_REST_PIECES = (D_IN - C_XB) // REST_PIECE
Z_U = D_POOL
Z_VS = Z_U + D_SGU
Z_WIDTH = Z_VS + D_SGU
S1_TN = 1024
VMEM_LIMIT_PROMPT = 62 * 1024 * 1024
VMEM_LIMIT_SAMPLE = 56 * 1024 * 1024

F32 = jnp.float32
BF16 = jnp.bfloat16
NEG_INF = float("-inf")


def _rms(x, g):
    return x * lax.rsqrt(jnp.mean(x * x, axis=-1, keepdims=True) + EPS) * g


def _silu(x):
    return x * (1.0 / (1.0 + jnp.exp(-x)))


def _layernorm(x, g, b):
    mu = jnp.mean(x, axis=-1, keepdims=True)
    xc = x - mu
    var = jnp.mean(xc * xc, axis=-1, keepdims=True)
    return xc * lax.rsqrt(var + EPS) * g + b


def _swap_heads(x):
    return pltpu.roll(x, HEAD_DIM, 1)


def _rope(x, cos, sin):
    rows = x.shape[0]
    lane = lax.broadcasted_iota(jnp.int32, (rows, LANES), 1)
    first_half = (lane & HALF_DIM) == 0
    outs = []
    for c in range(x.shape[1] // LANES):
        xc = x[:, c * LANES:(c + 1) * LANES]
        partner = jnp.where(first_half,
                            pltpu.roll(xc, LANES - HALF_DIM, 1),
                            pltpu.roll(xc, HALF_DIM, 1))
        outs.append(xc * cos + partner * sin)
    return jnp.concatenate(outs, axis=1)


def _dot(a, b):
    return jnp.dot(a, b, preferred_element_type=F32)


def _dot_nt(a, b):
    return lax.dot_general(a, b, (((1,), (1,)), ((), ())), preferred_element_type=F32)


def _prompt_body(layer, n_alias, sinks_ref, x_ref, cos_ref, sin_ref, w_in_ref, w_out_ref,
                 npre_ref, npost_ref, pw_ref, ps_ref, lng_ref, lnb_ref, sw_ref, sb_ref, *rest):
    refs = rest[n_alias:]
    kbuf, vbuf, xbext = refs[5], refs[6], refs[7]
    j = pl.program_id(1)

    @pl.when(j == 0)
    def _start_of_sequence():
        kbuf[:, 0:WINDOW, :] = jnp.zeros((2 * N_KV, WINDOW, LANES), BF16)
        vbuf[:, 0:WINDOW, :] = jnp.zeros((N_KV, WINDOW, D_KV), BF16)
        xbext[0:HIST_PAD, :] = jnp.zeros((HIST_PAD, D_POOL), F32)

    @pl.when(j > 0)
    def _carry_from_previous_step():
        for i in range(2 * N_KV):
            kbuf[i, 0:WINDOW, :] = kbuf[i, TM:TM + WINDOW, :]
        for g in range(N_KV):
            vbuf[g, 0:WINDOW, :] = vbuf[g, TM:TM + WINDOW, :]
        xbext[0:HIST_PAD, :] = xbext[TM:TM + HIST_PAD, :]

    for sub in range(N_SUB):
        _prompt_sub_block(layer, sub, j, sinks_ref, x_ref, cos_ref, sin_ref, w_in_ref, w_out_ref,
                          npre_ref, npost_ref, pw_ref, ps_ref, lng_ref, lnb_ref, sw_ref, sb_ref, *refs)


def _prompt_sub_block(layer, sub, j, sinks_ref, x_ref, cos_ref, sin_ref, w_in_ref, w_out_ref,
                      npre_ref, npost_ref, pw_ref, ps_ref, lng_ref, lnb_ref, sw_ref, sb_ref,
                      y_ref, nk_ref, nv_ref, np_ref, qbuf, kbuf, vbuf, xbext, mixbuf, zrest, ybuf):
    base = sub * SUB
    sub_rows = slice(base, base + SUB)
    first_in_sequence = (j == 0) if sub == 0 else None
    h = _rms(x_ref[0, sub_rows], npre_ref[...]).astype(BF16)

    def proj(lo, width):
        return _dot(h, w_in_ref[:, lo:lo + width])

    cos = cos_ref[sub_rows, :]
    sin = sin_ref[sub_rows, :]

    k = _rope(proj(C_K, D_KV), cos, sin)
    v = proj(C_V, D_KV)
    cos_q = cos * (ATT_SCALE * LOG2_E)
    sin_q = sin * (ATT_SCALE * LOG2_E)
    for part in range(D_ATT // ATT_PIECE):
        cols = slice(part * ATT_PIECE, (part + 1) * ATT_PIECE)
        qbuf[:, cols] = _rope(proj(C_Q + part * ATT_PIECE, ATT_PIECE), cos_q, sin_q).astype(BF16)
    for part in range(D_ATT // ATT_PIECE):
        cols = slice(part * ATT_PIECE, (part + 1) * ATT_PIECE)
        mixbuf[:, cols] = _silu(proj(C_GA + part * ATT_PIECE, ATT_PIECE)).astype(BF16)
    if sub == N_SUB - 1:
        nk_ref[0] = k[SUB - WINDOW:SUB]
        nv_ref[0] = v[SUB - WINDOW:SUB]
    lane_k = lax.broadcasted_iota(jnp.int32, (SUB, LANES), 1)
    low = lane_k < HEAD_DIM
    lane_v = lax.broadcasted_iota(jnp.int32, (SUB, D_KV), 1)
    def store_by_chunk(buf, idx, val):
        val = val.astype(BF16)
        for c in range(N_CHUNKS):
            r0 = WINDOW + base + c * CHUNK
            buf[idx, r0:r0 + CHUNK, :] = val[c * CHUNK:(c + 1) * CHUNK]

    for pair in range(2):
        kp = k[:, pair * LANES:(pair + 1) * LANES]
        ks = _swap_heads(kp)
        store_by_chunk(kbuf, 4 * pair + 0, jnp.where(low, kp, 0.0))
        store_by_chunk(kbuf, 4 * pair + 1, jnp.where(low, 0.0, ks))
        store_by_chunk(kbuf, 4 * pair + 2, jnp.where(low, ks, 0.0))
        store_by_chunk(kbuf, 4 * pair + 3, jnp.where(low, 0.0, kp))
    for g in range(N_KV):
        keep_v = (lane_v >= g * HEAD_DIM) & (lane_v < (g + 1) * HEAD_DIM)
        store_by_chunk(vbuf, g, jnp.where(keep_v, v, 0.0))

    qi = lax.broadcasted_iota(jnp.int32, (CHUNK, 2 * WINDOW), 0)
    kj = lax.broadcasted_iota(jnp.int32, (CHUNK, 2 * WINDOW), 1)
    dist = qi + WINDOW - kj
    band_bias = jnp.where((dist >= 0) & (dist < WINDOW), 0.0, NEG_INF)
    if first_in_sequence is None:
        first_bias = band_bias
    else:
        no_prev = jnp.where(first_in_sequence, WINDOW, 0)
        first_bias = jnp.where(kj < no_prev, NEG_INF, band_bias)
    low_c = lax.broadcasted_iota(jnp.int32, (CHUNK, LANES), 1) < HEAD_DIM
    rest_piece = 0

    def project_rest_piece(piece):
        cols = slice(piece * REST_PIECE, (piece + 1) * REST_PIECE)
        lo = C_XB + piece * REST_PIECE
        val = proj(lo, REST_PIECE)
        if C_GB <= lo < C_U:
            mixbuf[:, D_ATT + lo - C_GB:D_ATT + lo - C_GB + REST_PIECE] = _silu(val).astype(BF16)
        elif lo >= C_GC:
            at = D_ATT + D_POOL + lo - C_GC
            mixbuf[:, at:at + REST_PIECE] = _silu(val).astype(BF16)
        else:
            at = lo - C_XB if lo < C_GB else (Z_U + lo - C_U if lo < C_VS else Z_VS + lo - C_VS)
            zrest[:, at:at + REST_PIECE] = val

    def scores(c, g):
        r0 = c * CHUNK
        qs = jnp.concatenate(
            [qbuf[r0:r0 + CHUNK, (2 * g + i) * LANES:(2 * g + i + 1) * LANES] for i in range(2)],
            axis=0)
        return [_dot_nt(qs, kbuf[2 * g + half, base + r0:base + r0 + 2 * WINDOW, :]) for half in range(2)]

    order = [(c, g) for c in range(N_CHUNKS) for g in range(N_KV)]
    while rest_piece < N_REST_PIECES - len(order):
        project_rest_piece(rest_piece)
        rest_piece += 1
    s_next = scores(*order[0])
    for step, (c, g) in enumerate(order):
        bias = first_bias if c == 0 else band_bias
        r0 = c * CHUNK
        s_half = s_next
        if step + 1 < len(order):
            s_next = scores(*order[step + 1])
        blocks = []
        for r in range(GROUP):
            sb = s_half[r % 2][(r // 2) * CHUNK:(r // 2 + 1) * CHUNK] + bias
            sink = sinks_ref[layer, GROUP * g + r] * LOG2_E
            m = jnp.maximum(jnp.max(sb, axis=1, keepdims=True), sink)
            pe = jnp.exp2(sb - m)
            den = jnp.sum(pe, axis=1, keepdims=True) + jnp.exp2(sink - m)
            blocks.append((pe * (1.0 / den)).astype(BF16))
        pg = jnp.concatenate(blocks, axis=0)
        og = _dot(pg, vbuf[g, base + r0:base + r0 + 2 * WINDOW, :])
        o = og if g == 0 else o + og
        if rest_piece < N_REST_PIECES:
            project_rest_piece(rest_piece)
            rest_piece += 1
        if g < N_KV - 1:
            continue
        for pair in range(2):
            for i in range(2):
                a = o[(2 * i) * CHUNK:(2 * i + 1) * CHUNK, pair * LANES:(pair + 1) * LANES]
                b = o[(2 * i + 1) * CHUNK:(2 * i + 2) * CHUNK, pair * LANES:(pair + 1) * LANES]
                for odd, blk in ((0, jnp.where(low_c, a, _swap_heads(b))),
                                 (1, jnp.where(low_c, _swap_heads(a), b))):
                    cols = slice((2 * (2 * pair + odd) + i) * LANES, (2 * (2 * pair + odd) + i + 1) * LANES)
                    mixbuf[r0:r0 + CHUNK, cols] = (blk * mixbuf[r0:r0 + CHUNK, cols]).astype(BF16)

    while rest_piece < N_REST_PIECES:
        project_rest_piece(rest_piece)
        rest_piece += 1

    xb0 = HIST_PAD + base
    xbext[xb0:xb0 + SUB, :] = zrest[:, 0:D_POOL]
    if sub == N_SUB - 1:
        np_ref[0] = xbext[HIST_PAD + TM - POOL_HIST:HIST_PAD + TM, :]
    pos1 = j * TM + base + lax.broadcasted_iota(jnp.int32, (SUB, LANES), 0) + 1
    for gi, w in enumerate(POOL_WINDOWS):
        cols = slice(gi * LANES, (gi + 1) * LANES)
        cur = xbext[xb0:xb0 + SUB, cols]
        acc = cur
        for i in range(1, w):
            acc = acc + xbext[xb0 - i:xb0 - i + SUB, cols]
        cnt = jnp.minimum(w, pos1).astype(F32)
        pooled = acc / cnt - cur
        mixed = _dot(pooled.astype(BF16), pw_ref[gi]) * ps_ref[:, cols]
        out_cols = slice(D_ATT + gi * LANES, D_ATT + (gi + 1) * LANES)
        mixbuf[:, out_cols] = (mixed * mixbuf[:, out_cols]).astype(BF16)

    vn = _layernorm(zrest[:, Z_VS:Z_VS + D_SGU], lng_ref[...], lnb_ref[...]).astype(BF16)
    ti = lax.broadcasted_iota(jnp.int32, (CHUNK, CHUNK), 0)
    si = lax.broadcasted_iota(jnp.int32, (CHUNK, CHUNK), 1)
    for hh in range(N_SGU_HEADS):
        cols = slice(hh * LANES, (hh + 1) * LANES)
        wm = jnp.where(ti >= si, sw_ref[hh], 0.0).astype(BF16)
        for c in range(N_CHUNKS):
            rows = slice(c * CHUNK, (c + 1) * CHUNK)
            mixed = _dot(wm, vn[rows, cols]) + sb_ref[hh]
            out_cols = slice(D_ATT + D_POOL + hh * LANES, D_ATT + D_POOL + (hh + 1) * LANES)
            mixbuf[rows, out_cols] = (
                zrest[rows, Z_U + hh * LANES:Z_U + (hh + 1) * LANES] * mixed * mixbuf[rows, out_cols]
            ).astype(BF16)

    slot = jnp.minimum(j, 0)
    y = _dot(mixbuf[...], w_out_ref[...])
    ybuf[slot] = y
    if sub < N_SUB - 1:
        y = ybuf[slot]
    y_ref[0, sub_rows] = x_ref[0, sub_rows] + _rms(y, npost_ref[...])


def _layer_spec(shape, layer, grid_rank):
    zeros = (0,) * len(shape)
    if grid_rank == 2:
        index_map = lambda b, j: (layer,) + zeros
    else:
        index_map = lambda i: (layer,) + zeros
    return pl.BlockSpec((None,) + tuple(shape), index_map, pipeline_mode=pl.Buffered(1))


def _prompt_layer(layer, x, cos, sin, params, prev_outs):
    (sinks, w_in_b, w_out_b, npre, npost, pw_b, ps, lng, lnb, sw, sb_full) = params
    grid = (BATCH, SEQ // TM)
    n_alias = len(prev_outs)
    in_specs = [
        pl.BlockSpec(memory_space=pltpu.SMEM),
        pl.BlockSpec((1, TM, D_MODEL), lambda b, j: (b, j, 0)),
        pl.BlockSpec((TM, LANES), lambda b, j: (j, 0)),
        pl.BlockSpec((TM, LANES), lambda b, j: (j, 0)),
        _layer_spec((D_MODEL, D_IN), layer, 2),
        _layer_spec((D_MODEL, D_MODEL), layer, 2),
        _layer_spec((1, D_MODEL), layer, 2),
        _layer_spec((1, D_MODEL), layer, 2),
        _layer_spec((len(POOL_WINDOWS), LANES, LANES), layer, 2),
        _layer_spec((1, D_POOL), layer, 2),
        _layer_spec((1, D_SGU), layer, 2),
        _layer_spec((1, D_SGU), layer, 2),
        _layer_spec((N_SGU_HEADS, CHUNK, CHUNK), layer, 2),
        _layer_spec((N_SGU_HEADS, CHUNK, LANES), layer, 2),
    ] + [pl.BlockSpec(memory_space=pl.ANY)] * n_alias
    out_specs = [
        pl.BlockSpec((1, TM, D_MODEL), lambda b, j: (b, j, 0)),
        pl.BlockSpec((None, 1, WINDOW, D_KV), lambda b, j: (layer, b, 0, 0)),
        pl.BlockSpec((None, 1, WINDOW, D_KV), lambda b, j: (layer, b, 0, 0)),
        pl.BlockSpec((None, 1, POOL_HIST, D_POOL), lambda b, j: (layer, b, 0, 0)),
    ]
    out_shape = [
        jax.ShapeDtypeStruct((BATCH, SEQ, D_MODEL), F32),
        jax.ShapeDtypeStruct((DEPTH, BATCH, WINDOW, D_KV), F32),
        jax.ShapeDtypeStruct((DEPTH, BATCH, WINDOW, D_KV), F32),
        jax.ShapeDtypeStruct((DEPTH, BATCH, POOL_HIST, D_POOL), F32),
    ]
    scratch = [
        pltpu.VMEM((SUB, D_ATT), BF16),
        pltpu.VMEM((2 * N_KV, WINDOW + TM, LANES), BF16),
        pltpu.VMEM((N_KV, WINDOW + TM, D_KV), BF16),
        pltpu.VMEM((HIST_PAD + TM, D_POOL), F32),
        pltpu.VMEM((SUB, D_MODEL), BF16),
        pltpu.VMEM((SUB, Z_WIDTH), F32),
        pltpu.VMEM((1, SUB, D_MODEL), F32),
    ]
    n_in = len(in_specs) - n_alias
    return pl.pallas_call(
        functools.partial(_prompt_body, layer, n_alias),
        grid=grid,
        in_specs=in_specs,
        out_specs=out_specs,
        out_shape=out_shape,
        scratch_shapes=scratch,
        input_output_aliases={n_in + i: 1 + i for i in range(n_alias)},
        compiler_params=pltpu.CompilerParams(
            dimension_semantics=("arbitrary", "arbitrary"),
            vmem_limit_bytes=VMEM_LIMIT_PROMPT),
        name="prompt_layer",
    )(sinks, x, cos, sin, w_in_b, w_out_b, npre, npost, pw_b, ps, lng, lnb, sw, sb_full, *prev_outs)


def _sample_proj_body(x_ref, npre_ref, w_ref, z_ref, hbuf):
    @pl.when(pl.program_id(0) == 0)
    def _norm_once():
        hbuf[...] = _rms(x_ref[...].reshape(SAMPLE_ROWS, D_MODEL), npre_ref[...]).astype(BF16)

    z_ref[...] = _dot(hbuf[...], w_ref[...])


def _sample_proj(layer, xs, npre, w_in_b):
    return pl.pallas_call(
        _sample_proj_body,
        grid=(D_IN // S1_TN,),
        in_specs=[
            pl.BlockSpec(xs.shape, lambda n: (0,) * xs.ndim),
            pl.BlockSpec((None, 1, D_MODEL), lambda n: (layer, 0, 0)),
            pl.BlockSpec((None, D_MODEL, S1_TN), lambda n: (layer, 0, n)),
        ],
        out_specs=pl.BlockSpec((SAMPLE_ROWS, S1_TN), lambda n: (0, n)),
        out_shape=jax.ShapeDtypeStruct((SAMPLE_ROWS, D_IN), F32),
        scratch_shapes=[pltpu.VMEM((SAMPLE_ROWS, D_MODEL), BF16)],
        compiler_params=pltpu.CompilerParams(
            dimension_semantics=("arbitrary",), vmem_limit_bytes=VMEM_LIMIT_SAMPLE),
        name="sample_proj",
    )(xs, npre, w_in_b)


SCORE_ROWS = N_HEADS * GR
N_REST_BLOCKS = (D_IN - C_XB) // LANES


def _sample_mix_body(n_alias, z_ref, cos_ref, sin_ref, sink_ref, ck_ref, cv_ref, st_ref,
                     pw_ref, ps_ref, lng_ref, lnb_ref, wexp_ref, bexp_ref, *rest):
    (mix_ref, nk_ref, nv_ref, npool_ref, vn_ref,
     qbig, knew_f, vnew_f, knew_t, vnew_t, zc, sc, npc, mo, vo) = rest[n_alias:]
    cos = cos_ref[...]
    sin = sin_ref[...]

    q = _rope(z_ref[:, C_Q:C_Q + D_ATT], cos * ATT_SCALE, sin * ATT_SCALE)
    k_new = _rope(z_ref[:, C_K:C_K + D_KV], cos, sin)
    v_new = z_ref[:, C_V:C_V + D_KV]
    knew_f[...] = k_new
    vnew_f[...] = v_new
    knew_t[...] = k_new.T
    vnew_t[...] = v_new.T
    low = lax.broadcasted_iota(jnp.int32, (RT, LANES), 1) < HEAD_DIM
    zero_blk = jnp.zeros((RT, LANES), F32)
    for h in range(N_HEADS):
        g = h // GROUP
        src = q[:, (h // 2) * LANES:(h // 2 + 1) * LANES]
        if h % 2 != g % 2:
            src = _swap_heads(src)
        piece = jnp.where(low, src, 0.0) if g % 2 == 0 else jnp.where(low, 0.0, src)
        full = [zero_blk, zero_blk]
        full[g // 2] = piece
        qbig[h] = jnp.concatenate(full, axis=1).astype(BF16)

    row_h = lax.broadcasted_iota(jnp.int32, (SCORE_ROWS, GB * WINDOW), 0)
    col_h = lax.broadcasted_iota(jnp.int32, (SCORE_ROWS, GB * WINDOW), 1)
    same_h = ((row_h >> 2) & (GB - 1)) == (col_h >> 7)
    bias_h = jnp.where(same_h, jnp.where((col_h & (WINDOW - 1)) > (row_h & (DEC_SEQ - 1)), 0.0, NEG_INF),
                       NEG_INF)
    row_n = lax.broadcasted_iota(jnp.int32, (SCORE_ROWS, GR), 0)
    col_n = lax.broadcasted_iota(jnp.int32, (SCORE_ROWS, GR), 1)
    same_n = ((row_n >> 2) & (GB - 1)) == (col_n >> 2)
    bias_n = jnp.where(same_n, jnp.where((col_n & (DEC_SEQ - 1)) <= (row_n & (DEC_SEQ - 1)), 0.0, NEG_INF),
                       NEG_INF)
    sink = sink_ref[:, 0:1]
    low_g = lax.broadcasted_iota(jnp.int32, (GR, LANES), 1) < HEAD_DIM
    key_lane = lax.broadcasted_iota(jnp.int32, (D_KV, WINDOW), 1)
    src_key = lax.broadcasted_iota(jnp.int32, (WINDOW, WINDOW), 0)
    dst_key = lax.broadcasted_iota(jnp.int32, (WINDOW, WINDOW), 1)
    shift_keys = jnp.where(src_key == dst_key + DEC_SEQ, 1.0, 0.0).astype(BF16)

    def roll_in(hist_t, new_t, bi):
        hi = hist_t.astype(BF16)
        rest1 = hist_t - hi.astype(F32)
        mid = rest1.astype(BF16)
        lo = (rest1 - mid.astype(F32)).astype(BF16)
        rolled = _dot(hi, shift_keys) + _dot(mid, shift_keys) + _dot(lo, shift_keys)
        placed = pltpu.roll(new_t, (WINDOW - DEC_SEQ - DEC_SEQ * bi) % RT, 1)
        return jnp.where(key_lane >= WINDOW - DEC_SEQ, placed, rolled)

    for grp in range(BT // GB):
        r0 = grp * GR
        lhs = qbig[:, r0:r0 + GR, :].reshape(SCORE_ROWS, D_KV)
        kh_t = jnp.concatenate([ck_ref[grp * GB + bb] for bb in range(GB)], axis=1)
        vh_t = jnp.concatenate([cv_ref[grp * GB + bb] for bb in range(GB)], axis=1)
        kn = knew_f[r0:r0 + GR, :]
        vn_new = vnew_f[r0:r0 + GR, :]
        s_h = _dot(lhs, kh_t.astype(BF16)) + bias_h
        s_n = _dot_nt(lhs, kn.astype(BF16)) + bias_n
        m = jnp.maximum(jnp.maximum(jnp.max(s_h, axis=1, keepdims=True),
                                    jnp.max(s_n, axis=1, keepdims=True)), sink)
        p_h = jnp.exp(s_h - m)
        p_n = jnp.exp(s_n - m)
        den = (jnp.sum(p_h, axis=1, keepdims=True) + jnp.sum(p_n, axis=1, keepdims=True)
               + jnp.exp(sink - m))
        inv = 1.0 / den
        o = (_dot_nt((p_h * inv).astype(BF16), vh_t.astype(BF16))
             + _dot((p_n * inv).astype(BF16), vn_new.astype(BF16)))
        for c in range(N_HEADS // 2):
            g = c // 2
            a = o[(2 * c) * GR:(2 * c + 1) * GR, (g // 2) * LANES:(g // 2 + 1) * LANES]
            b = o[(2 * c + 1) * GR:(2 * c + 2) * GR, (g // 2) * LANES:(g // 2 + 1) * LANES]
            blk = jnp.where(low_g, a, _swap_heads(b)) if g % 2 == 0 else jnp.where(low_g, _swap_heads(a), b)
            gate = _silu(z_ref[r0:r0 + GR, C_GA + c * LANES:C_GA + (c + 1) * LANES])
            mix_ref[r0:r0 + GR, c * LANES:(c + 1) * LANES] = blk * gate
        for bb in range(GB):
            bi = grp * GB + bb
            nk_ref[bi] = roll_in(ck_ref[bi], knew_t[...], bi)
            nv_ref[bi] = roll_in(cv_ref[bi], vnew_t[...], bi)

    for c in range(N_REST_BLOCKS):
        zc[c] = z_ref[:, C_XB + c * LANES:C_XB + (c + 1) * LANES]
    for c in range(D_POOL // LANES):
        sc[c] = st_ref[:, c * LANES:(c + 1) * LANES]

    def tok(t, lo, width):
        blk0 = (lo - C_XB) // LANES
        parts = [zc[blk0 + c, pl.ds(t, BT, stride=DEC_SEQ), :] for c in range(width // LANES)]
        return parts[0] if len(parts) == 1 else jnp.concatenate(parts, axis=1)

    def ext(i, gi):
        if i < POOL_HIST:
            return sc[gi, pl.ds(i, BT, stride=POOL_HIST), :]
        return tok(i - POOL_HIST, C_XB + gi * LANES, LANES)

    for gi, w in enumerate(POOL_WINDOWS):
        for s in range(POOL_HIST):
            npc[gi, pl.ds(s, BT, stride=POOL_HIST), :] = ext(s + DEC_SEQ, gi)
        pooled = []
        for t in range(DEC_SEQ):
            cur = ext(POOL_HIST + t, gi)
            acc = cur
            for i in range(1, w):
                acc = acc + ext(POOL_HIST + t - i, gi)
            cnt = float(min(w, PAST_LEN + t + 1))
            pooled.append(acc / cnt - cur)
        pooled = jnp.concatenate(pooled, axis=0).astype(BF16)
        mixed = _dot(pooled, pw_ref[gi]) * ps_ref[:, gi * LANES:(gi + 1) * LANES]
        for t in range(DEC_SEQ):
            gate = _silu(tok(t, C_GB + gi * LANES, LANES))
            mo[gi, pl.ds(t, BT, stride=DEC_SEQ), :] = mixed[t * BT:(t + 1) * BT] * gate

    vns = []
    for t in range(DEC_SEQ):
        vn_t = _layernorm(tok(t, C_VS, D_SGU), lng_ref[...], lnb_ref[...])
        for c in range(D_SGU // LANES):
            vo[c, pl.ds(t, BT, stride=DEC_SEQ), :] = vn_t[:, c * LANES:(c + 1) * LANES]
        vns.append(vn_t)
    for t in range(DEC_SEQ):
        mixed = bexp_ref[t:t + 1, :]
        for s in range(t + 1):
            mixed = mixed + wexp_ref[DEC_SEQ * t + s:DEC_SEQ * t + s + 1, :] * vns[s]
        out = tok(t, C_U, D_SGU) * mixed * _silu(tok(t, C_GC, D_SGU))
        for c in range(D_SGU // LANES):
            mo[D_POOL // LANES + c, pl.ds(t, BT, stride=DEC_SEQ), :] = out[:, c * LANES:(c + 1) * LANES]

    for c in range(D_POOL // LANES):
        npool_ref[:, c * LANES:(c + 1) * LANES] = npc[c]
    for c in range((D_POOL + D_SGU) // LANES):
        mix_ref[:, D_ATT + c * LANES:D_ATT + (c + 1) * LANES] = mo[c]
    for c in range(D_SGU // LANES):
        vn_ref[:, c * LANES:(c + 1) * LANES] = vo[c]


def _sample_mix(layer, z, cos_t, sin_t, params, ck, cv, state2, prev_outs):
    (sink_rows, pw_b, ps, lng, lnb, wexp, bexp) = params
    n_alias = len(prev_outs)
    const = lambda shape: pl.BlockSpec(shape, lambda i: (0,) * len(shape))
    in_specs = [
        pl.BlockSpec((RT, D_IN), lambda i: (i, 0)),
        const((RT, LANES)), const((RT, LANES)),
        _layer_spec((SCORE_ROWS, LANES), layer, 1),
        pl.BlockSpec((None, BT, D_KV, WINDOW), lambda i: (layer, i, 0, 0)),
        pl.BlockSpec((None, BT, D_KV, WINDOW), lambda i: (layer, i, 0, 0)),
        pl.BlockSpec((None, BT * POOL_HIST, D_POOL), lambda i: (layer, i, 0)),
        _layer_spec((len(POOL_WINDOWS), LANES, LANES), layer, 1),
        _layer_spec((1, D_POOL), layer, 1),
        _layer_spec((1, D_SGU), layer, 1),
        _layer_spec((1, D_SGU), layer, 1),
        _layer_spec((DEC_SEQ * DEC_SEQ, D_SGU), layer, 1),
        _layer_spec((DEC_SEQ, D_SGU), layer, 1),
    ] + [pl.BlockSpec(memory_space=pl.ANY)] * n_alias
    out_specs = [
        pl.BlockSpec((RT, D_MODEL), lambda i: (i, 0)),
        pl.BlockSpec((None, BT, D_KV, WINDOW), lambda i: (layer, i, 0, 0)),
        pl.BlockSpec((None, BT, D_KV, WINDOW), lambda i: (layer, i, 0, 0)),
        pl.BlockSpec((None, BT * POOL_HIST, D_POOL), lambda i: (layer, i, 0)),
        pl.BlockSpec((None, RT, D_SGU), lambda i: (layer, i, 0)),
    ]
    out_shape = [
        jax.ShapeDtypeStruct((SAMPLE_ROWS, D_MODEL), F32),
        jax.ShapeDtypeStruct((DEPTH, DEC_BATCH, D_KV, WINDOW), F32),
        jax.ShapeDtypeStruct((DEPTH, DEC_BATCH, D_KV, WINDOW), F32),
        jax.ShapeDtypeStruct((DEPTH, DEC_BATCH * POOL_HIST, D_POOL), F32),
        jax.ShapeDtypeStruct((DEPTH, SAMPLE_ROWS, D_SGU), F32),
    ]
    scratch = [
        pltpu.VMEM((N_HEADS, RT, D_KV), BF16),
        pltpu.VMEM((RT, D_KV), F32),
        pltpu.VMEM((RT, D_KV), F32),
        pltpu.VMEM((D_KV, RT), F32),
        pltpu.VMEM((D_KV, RT), F32),
        pltpu.VMEM((N_REST_BLOCKS, RT, LANES), F32),
        pltpu.VMEM((D_POOL // LANES, BT * POOL_HIST, LANES), F32),
        pltpu.VMEM((D_POOL // LANES, BT * POOL_HIST, LANES), F32),
        pltpu.VMEM(((D_POOL + D_SGU) // LANES, RT, LANES), F32),
        pltpu.VMEM((D_SGU // LANES, RT, LANES), F32),
    ]
    n_in = len(in_specs) - n_alias
    return pl.pallas_call(
        functools.partial(_sample_mix_body, n_alias),
        grid=(DEC_BATCH // BT,),
        in_specs=in_specs,
        out_specs=out_specs,
        out_shape=out_shape,
        scratch_shapes=scratch,
        input_output_aliases={n_in + i: 1 + i for i in range(n_alias)},
        compiler_params=pltpu.CompilerParams(
            dimension_semantics=("arbitrary",), vmem_limit_bytes=VMEM_LIMIT_SAMPLE),
        name="sample_mix",
    )(z, cos_t, sin_t, sink_rows, ck, cv, state2, pw_b, ps, lng, lnb, wexp, bexp, *prev_outs)


def _sample_out_body(x_ref, mix_ref, w_out_ref, npost_ref, y_ref):
    y = _dot(mix_ref[...].astype(BF16), w_out_ref[...])
    out = x_ref[...].reshape(SAMPLE_ROWS, D_MODEL) + _rms(y, npost_ref[...])
    y_ref[...] = out.reshape(y_ref.shape)


def _sample_out(layer, xs, mix, w_out_b, npost, out_shape):
    full = lambda shape: pl.BlockSpec(shape, lambda i: (0,) * len(shape))
    return pl.pallas_call(
        _sample_out_body,
        grid=(1,),
        in_specs=[
            full(xs.shape),
            full((SAMPLE_ROWS, D_MODEL)),
            pl.BlockSpec((None, D_MODEL, D_MODEL), lambda i: (layer, 0, 0)),
            pl.BlockSpec((None, 1, D_MODEL), lambda i: (layer, 0, 0)),
        ],
        out_specs=full(out_shape),
        out_shape=jax.ShapeDtypeStruct(out_shape, F32),
        compiler_params=pltpu.CompilerParams(
            dimension_semantics=("arbitrary",), vmem_limit_bytes=VMEM_LIMIT_SAMPLE),
        name="sample_out",
    )(xs, mix, w_out_b, npost)


def _rope_tables(pos):
    inv = ROPE_THETA ** (-jnp.arange(0, HEAD_DIM, 2, dtype=F32) / HEAD_DIM)
    ang = pos.astype(F32)[:, None] * inv[None, :]
    c, s = jnp.cos(ang), jnp.sin(ang)
    return jnp.concatenate([c, c, c, c], axis=1), jnp.concatenate([-s, s, -s, s], axis=1)


def kernel(x_prompt, x_sample, cache_k, cache_v, state_pool, w_in, w_out, norm_pre, norm_post,
           attn_sinks, pool_w, pool_scale, sgu_ln_g, sgu_ln_b, sgu_w, sgu_b):
    cos_p, sin_p = _rope_tables(jnp.arange(SEQ))
    cos_s, sin_s = _rope_tables(PAST_LEN + jnp.arange(DEC_SEQ))
    cos_t = jnp.tile(cos_s, (BT, 1))
    sin_t = jnp.tile(sin_s, (BT, 1))

    w_in_b = w_in.astype(BF16)
    w_out_b = w_out.astype(BF16)
    pw_b = pool_w.astype(BF16)
    npre = norm_pre[:, None, :]
    npost = norm_post[:, None, :]
    ps = pool_scale[:, None, :]
    lng = sgu_ln_g[:, None, :]
    lnb = sgu_ln_b[:, None, :]
    sb_full = jnp.broadcast_to(sgu_b[:, :, :, None], (DEPTH, N_SGU_HEADS, CHUNK, LANES))
    wexp = jnp.repeat(sgu_w[:, :, :DEC_SEQ, :DEC_SEQ].transpose(0, 2, 3, 1), LANES, axis=-1
                      ).reshape(DEPTH, DEC_SEQ * DEC_SEQ, D_SGU)
    bexp = jnp.repeat(sgu_b[:, :, :DEC_SEQ].transpose(0, 2, 1), LANES, axis=-1)
    sink_rows = jnp.broadcast_to(attn_sinks[:, :, None, None], (DEPTH, N_HEADS, GR, LANES)
                                 ).reshape(DEPTH, SCORE_ROWS, LANES)

    xp = x_prompt
    xs = x_sample
    to_key_minor = lambda c: c.transpose(0, 1, 3, 4, 2).reshape(DEPTH, DEC_BATCH, D_KV, WINDOW)
    from_key_minor = lambda c: c.reshape(DEPTH, DEC_BATCH, N_KV, HEAD_DIM, WINDOW).transpose(0, 1, 4, 2, 3)
    ck = to_key_minor(cache_k)
    cv = to_key_minor(cache_v)
    state2 = state_pool.reshape(DEPTH, DEC_BATCH * POOL_HIST, D_POOL)
    prompt_params = (attn_sinks, w_in_b, w_out_b, npre, npost, pw_b, ps, lng, lnb, sgu_w, sb_full)
    sample_params = (sink_rows, pw_b, ps, lng, lnb, wexp, bexp)
    p_outs, s_outs = [], []
    for layer in range(DEPTH):
        xp, *p_outs = _prompt_layer(layer, xp, cos_p, sin_p, prompt_params, p_outs)
        z = _sample_proj(layer, xs, npre, w_in_b)
        mix, *s_outs = _sample_mix(layer, z, cos_t, sin_t, sample_params, ck, cv, state2, s_outs)
        xs_shape = x_sample.shape if layer == DEPTH - 1 else (SAMPLE_ROWS, D_MODEL)
        xs = _sample_out(layer, xs, mix, w_out_b, npost, xs_shape)

    nk_p, nv_p, np_p = p_outs
    nk_s, nv_s, np_s, vn_s = s_outs
    kv5 = lambda a, nb: a.reshape(DEPTH, nb, WINDOW, N_KV, HEAD_DIM)
    return (xp, xs,
            kv5(nk_p, BATCH), kv5(nv_p, BATCH), np_p,
            from_key_minor(nk_s), from_key_minor(nv_s),
            np_s.reshape(DEPTH, DEC_BATCH, POOL_HIST, D_POOL),
            vn_s.reshape(DEPTH, DEC_BATCH, DEC_SEQ, D_SGU))
```

```python
import functools

import jax
import jax.numpy as jnp
from jax import lax
from jax.experimental import pallas as pl
from jax.experimental.pallas import tpu as pltpu

D_MODEL = 2048
SEQ = 2048
BATCH = 8
DEPTH = 2
DEC_BATCH = 128
DEC_SEQ = 4
PAST_LEN = 8192
HEAD_DIM = 64
HALF_DIM = HEAD_DIM // 2
N_HEADS = 16
N_KV = 4
GROUP = N_HEADS // N_KV
D_ATT = N_HEADS * HEAD_DIM
D_KV = N_KV * HEAD_DIM
WINDOW = 128
ROPE_THETA = 10000.0
D_POOL = 512
POOL_WINDOWS = (2, 4, 8, 16)
POOL_HIST = 15
D_SGU = 512
CHUNK = 128
N_SGU_HEADS = 4
D_IN = 5120
EPS = 1e-6
ATT_SCALE = HEAD_DIM ** -0.5
LOG2_E = 1.4426950408889634

C_Q = 0
C_K = C_Q + D_ATT
C_V = C_K + D_KV
C_GA = C_V + D_KV
C_XB = C_GA + D_ATT
C_GB = C_XB + D_POOL
C_U = C_GB + D_POOL
C_VS = C_U + D_SGU
C_GC = C_VS + D_SGU
assert C_GC + D_SGU == D_IN

LANES = 128
SUBLANES = 8
HIST_PAD = 16
SUB = 256
N_SUB = 2
TM = SUB * N_SUB
N_CHUNKS = SUB // CHUNK
SAMPLE_ROWS = DEC_BATCH * DEC_SEQ
BT = 32
RT = BT * DEC_SEQ
assert RT == WINDOW
GB = 4
GR = GB * DEC_SEQ
ATT_PIECE = 512
REST_PIECE = 256
N_REST_PIECES = (D_IN - C_XB) // REST_PIECE
Z_U = D_POOL
Z_VS = Z_U + D_SGU
Z_WIDTH = Z_VS + D_SGU
S1_TN = 1024
VMEM_LIMIT_PROMPT = 62 * 1024 * 1024
VMEM_LIMIT_SAMPLE = 56 * 1024 * 1024

F32 = jnp.float32
BF16 = jnp.bfloat16
NEG_INF = float("-inf")


def _rms(x, g):
    return x * lax.rsqrt(jnp.mean(x * x, axis=-1, keepdims=True) + EPS) * g


def _silu(x):
    return x * (1.0 / (1.0 + jnp.exp(-x)))


def _layernorm(x, g, b):
    mu = jnp.mean(x, axis=-1, keepdims=True)
    xc = x - mu
    var = jnp.mean(xc * xc, axis=-1, keepdims=True)
    return xc * lax.rsqrt(var + EPS) * g + b


def _swap_heads(x):
    return pltpu.roll(x, HEAD_DIM, 1)


def _rope(x, cos, sin):
    rows = x.shape[0]
    lane = lax.broadcasted_iota(jnp.int32, (rows, LANES), 1)
    first_half = (lane & HALF_DIM) == 0
    outs = []
    for c in range(x.shape[1] // LANES):
        xc = x[:, c * LANES:(c + 1) * LANES]
        partner = jnp.where(first_half,
                            pltpu.roll(xc, LANES - HALF_DIM, 1),
                            pltpu.roll(xc, HALF_DIM, 1))
        outs.append(xc * cos + partner * sin)
    return jnp.concatenate(outs, axis=1)


def _dot(a, b):
    return jnp.dot(a, b, preferred_element_type=F32)


def _dot_nt(a, b):
    return lax.dot_general(a, b, (((1,), (1,)), ((), ())), preferred_element_type=F32)


def _prompt_body(layer, n_alias, sinks_ref, x_ref, cos_ref, sin_ref, w_in_ref, w_out_ref,
                 npre_ref, npost_ref, pw_ref, ps_ref, lng_ref, lnb_ref, sw_ref, sb_ref, *rest):
    refs = rest[n_alias:]
    kbuf, vbuf, xbext = refs[5], refs[6], refs[7]
    j = pl.program_id(1)

    @pl.when(j == 0)
    def _start_of_sequence():
        kbuf[:, 0:WINDOW, :] = jnp.zeros((2 * N_KV, WINDOW, LANES), BF16)
        vbuf[:, 0:WINDOW, :] = jnp.zeros((N_KV, WINDOW, D_KV), BF16)
        xbext[0:HIST_PAD, :] = jnp.zeros((HIST_PAD, D_POOL), F32)

    @pl.when(j > 0)
    def _carry_from_previous_step():
        for i in range(2 * N_KV):
            kbuf[i, 0:WINDOW, :] = kbuf[i, TM:TM + WINDOW, :]
        for g in range(N_KV):
            vbuf[g, 0:WINDOW, :] = vbuf[g, TM:TM + WINDOW, :]
        xbext[0:HIST_PAD, :] = xbext[TM:TM + HIST_PAD, :]

    for sub in range(N_SUB):
        _prompt_sub_block(layer, sub, j, sinks_ref, x_ref, cos_ref, sin_ref, w_in_ref, w_out_ref,
                          npre_ref, npost_ref, pw_ref, ps_ref, lng_ref, lnb_ref, sw_ref, sb_ref, *refs)


def _prompt_sub_block(layer, sub, j, sinks_ref, x_ref, cos_ref, sin_ref, w_in_ref, w_out_ref,
                      npre_ref, npost_ref, pw_ref, ps_ref, lng_ref, lnb_ref, sw_ref, sb_ref,
                      y_ref, nk_ref, nv_ref, np_ref, qbuf, kbuf, vbuf, xbext, mixbuf, zrest, ybuf):
    base = sub * SUB
    sub_rows = slice(base, base + SUB)
    first_in_sequence = (j == 0) if sub == 0 else None
    h = _rms(x_ref[0, sub_rows], npre_ref[...]).astype(BF16)

    def proj(lo, width):
        return _dot(h, w_in_ref[:, lo:lo + width])

    cos = cos_ref[sub_rows, :]
    sin = sin_ref[sub_rows, :]

    k = _rope(proj(C_K, D_KV), cos, sin)
    v = proj(C_V, D_KV)
    cos_q = cos * (ATT_SCALE * LOG2_E)
    sin_q = sin * (ATT_SCALE * LOG2_E)
    for part in range(D_ATT // ATT_PIECE):
        cols = slice(part * ATT_PIECE, (part + 1) * ATT_PIECE)
        qbuf[:, cols] = _rope(proj(C_Q + part * ATT_PIECE, ATT_PIECE), cos_q, sin_q).astype(BF16)
    for part in range(D_ATT // ATT_PIECE):
        cols = slice(part * ATT_PIECE, (part + 1) * ATT_PIECE)
        mixbuf[:, cols] = _silu(proj(C_GA + part * ATT_PIECE, ATT_PIECE)).astype(BF16)
    if sub == N_SUB - 1:
        nk_ref[0] = k[SUB - WINDOW:SUB]
        nv_ref[0] = v[SUB - WINDOW:SUB]
    lane_k = lax.broadcasted_iota(jnp.int32, (SUB, LANES), 1)
    low = lane_k < HEAD_DIM
    lane_v = lax.broadcasted_iota(jnp.int32, (SUB, D_KV), 1)
    def store_by_chunk(buf, idx, val):
        val = val.astype(BF16)
        for c in range(N_CHUNKS):
            r0 = WINDOW + base + c * CHUNK
            buf[idx, r0:r0 + CHUNK, :] = val[c * CHUNK:(c + 1) * CHUNK]

    for pair in range(2):
        kp = k[:, pair * LANES:(pair + 1) * LANES]
        ks = _swap_heads(kp)
        store_by_chunk(kbuf, 4 * pair + 0, jnp.where(low, kp, 0.0))
        store_by_chunk(kbuf, 4 * pair + 1, jnp.where(low, 0.0, ks))
        store_by_chunk(kbuf, 4 * pair + 2, jnp.where(low, ks, 0.0))
        store_by_chunk(kbuf, 4 * pair + 3, jnp.where(low, 0.0, kp))
    for g in range(N_KV):
        keep_v = (lane_v >= g * HEAD_DIM) & (lane_v < (g + 1) * HEAD_DIM)
        store_by_chunk(vbuf, g, jnp.where(keep_v, v, 0.0))

    qi = lax.broadcasted_iota(jnp.int32, (CHUNK, 2 * WINDOW), 0)
    kj = lax.broadcasted_iota(jnp.int32, (CHUNK, 2 * WINDOW), 1)
    dist = qi + WINDOW - kj
    band_bias = jnp.where((dist >= 0) & (dist < WINDOW), 0.0, NEG_INF)
    if first_in_sequence is None:
        first_bias = band_bias
    else:
        no_prev = jnp.where(first_in_sequence, WINDOW, 0)
        first_bias = jnp.where(kj < no_prev, NEG_INF, band_bias)
    low_c = lax.broadcasted_iota(jnp.int32, (CHUNK, LANES), 1) < HEAD_DIM
    rest_piece = 0

    def project_rest_piece(piece):
        cols = slice(piece * REST_PIECE, (piece + 1) * REST_PIECE)
        lo = C_XB + piece * REST_PIECE
        val = proj(lo, REST_PIECE)
        if C_GB <= lo < C_U:
            mixbuf[:, D_ATT + lo - C_GB:D_ATT + lo - C_GB + REST_PIECE] = _silu(val).astype(BF16)
        elif lo >= C_GC:
            at = D_ATT + D_POOL + lo - C_GC
            mixbuf[:, at:at + REST_PIECE] = _silu(val).astype(BF16)
        else:
            at = lo - C_XB if lo < C_GB else (Z_U + lo - C_U if lo < C_VS else Z_VS + lo - C_VS)
            zrest[:, at:at + REST_PIECE] = val

    def scores(c, g):
        r0 = c * CHUNK
        qs = jnp.concatenate(
            [qbuf[r0:r0 + CHUNK, (2 * g + i) * LANES:(2 * g + i + 1) * LANES] for i in range(2)],
            axis=0)
        return [_dot_nt(qs, kbuf[2 * g + half, base + r0:base + r0 + 2 * WINDOW, :]) for half in range(2)]

    order = [(c, g) for c in range(N_CHUNKS) for g in range(N_KV)]
    while rest_piece < N_REST_PIECES - len(order):
        project_rest_piece(rest_piece)
        rest_piece += 1
    s_next = scores(*order[0])
    for step, (c, g) in enumerate(order):
        bias = first_bias if c == 0 else band_bias
        r0 = c * CHUNK
        s_half = s_next
        if step + 1 < len(order):
            s_next = scores(*order[step + 1])
        blocks = []
        for r in range(GROUP):
            sb = s_half[r % 2][(r // 2) * CHUNK:(r // 2 + 1) * CHUNK] + bias
            sink = sinks_ref[layer, GROUP * g + r] * LOG2_E
            m = jnp.maximum(jnp.max(sb, axis=1, keepdims=True), sink)
            pe = jnp.exp2(sb - m)
            den = jnp.sum(pe, axis=1, keepdims=True) + jnp.exp2(sink - m)
            blocks.append((pe * (1.0 / den)).astype(BF16))
        pg = jnp.concatenate(blocks, axis=0)
        og = _dot(pg, vbuf[g, base + r0:base + r0 + 2 * WINDOW, :])
        o = og if g == 0 else o + og
        if rest_piece < N_REST_PIECES:
            project_rest_piece(rest_piece)
            rest_piece += 1
        if g < N_KV - 1:
            continue
        for pair in range(2):
            for i in range(2):
                a = o[(2 * i) * CHUNK:(2 * i + 1) * CHUNK, pair * LANES:(pair + 1) * LANES]
                b = o[(2 * i + 1) * CHUNK:(2 * i + 2) * CHUNK, pair * LANES:(pair + 1) * LANES]
                for odd, blk in ((0, jnp.where(low_c, a, _swap_heads(b))),
                                 (1, jnp.where(low_c, _swap_heads(a), b))):
                    cols = slice((2 * (2 * pair + odd) + i) * LANES, (2 * (2 * pair + odd) + i + 1) * LANES)
                    mixbuf[r0:r0 + CHUNK, cols] = (blk * mixbuf[r0:r0 + CHUNK, cols]).astype(BF16)

    while rest_piece < N_REST_PIECES:
        project_rest_piece(rest_piece)
        rest_piece += 1

    xb0 = HIST_PAD + base
    xbext[xb0:xb0 + SUB, :] = zrest[:, 0:D_POOL]
    if sub == N_SUB - 1:
        np_ref[0] = xbext[HIST_PAD + TM - POOL_HIST:HIST_PAD + TM, :]
    pos1 = j * TM + base + lax.broadcasted_iota(jnp.int32, (SUB, LANES), 0) + 1
    for gi, w in enumerate(POOL_WINDOWS):
        cols = slice(gi * LANES, (gi + 1) * LANES)
        cur = xbext[xb0:xb0 + SUB, cols]
        acc = cur
        for i in range(1, w):
            acc = acc + xbext[xb0 - i:xb0 - i + SUB, cols]
        cnt = jnp.minimum(w, pos1).astype(F32)
        pooled = acc / cnt - cur
        mixed = _dot(pooled.astype(BF16), pw_ref[gi]) * ps_ref[:, cols]
        out_cols = slice(D_ATT + gi * LANES, D_ATT + (gi + 1) * LANES)
        mixbuf[:, out_cols] = (mixed * mixbuf[:, out_cols]).astype(BF16)

    vn = _layernorm(zrest[:, Z_VS:Z_VS + D_SGU], lng_ref[...], lnb_ref[...]).astype(BF16)
    ti = lax.broadcasted_iota(jnp.int32, (CHUNK, CHUNK), 0)
    si = lax.broadcasted_iota(jnp.int32, (CHUNK, CHUNK), 1)
    for hh in range(N_SGU_HEADS):
        cols = slice(hh * LANES, (hh + 1) * LANES)
        wm = jnp.where(ti >= si, sw_ref[hh], 0.0).astype(BF16)
        for c in range(N_CHUNKS):
            rows = slice(c * CHUNK, (c + 1) * CHUNK)
            mixed = _dot(wm, vn[rows, cols]) + sb_ref[hh]
            out_cols = slice(D_ATT + D_POOL + hh * LANES, D_ATT + D_POOL + (hh + 1) * LANES)
            mixbuf[rows, out_cols] = (
                zrest[rows, Z_U + hh * LANES:Z_U + (hh + 1) * LANES] * mixed * mixbuf[rows, out_cols]
            ).astype(BF16)

    slot = jnp.minimum(j, 0)
    y = _dot(mixbuf[...], w_out_ref[...])
    ybuf[slot] = y
    if sub < N_SUB - 1:
        y = ybuf[slot]
    y_ref[0, sub_rows] = x_ref[0, sub_rows] + _rms(y, npost_ref[...])


def _layer_spec(shape, layer, grid_rank):
    zeros = (0,) * len(shape)
    if grid_rank == 2:
        index_map = lambda b, j: (layer,) + zeros
    else:
        index_map = lambda i: (layer,) + zeros
    return pl.BlockSpec((None,) + tuple(shape), index_map, pipeline_mode=pl.Buffered(1))


def _prompt_layer(layer, x, cos, sin, params, prev_outs):
    (sinks, w_in_b, w_out_b, npre, npost, pw_b, ps, lng, lnb, sw, sb_full) = params
    grid = (BATCH, SEQ // TM)
    n_alias = len(prev_outs)
    in_specs = [
        pl.BlockSpec(memory_space=pltpu.SMEM),
        pl.BlockSpec((1, TM, D_MODEL), lambda b, j: (b, j, 0)),
        pl.BlockSpec((TM, LANES), lambda b, j: (j, 0)),
        pl.BlockSpec((TM, LANES), lambda b, j: (j, 0)),
        _layer_spec((D_MODEL, D_IN), layer, 2),
        _layer_spec((D_MODEL, D_MODEL), layer, 2),
        _layer_spec((1, D_MODEL), layer, 2),
        _layer_spec((1, D_MODEL), layer, 2),
        _layer_spec((len(POOL_WINDOWS), LANES, LANES), layer, 2),
        _layer_spec((1, D_POOL), layer, 2),
        _layer_spec((1, D_SGU), layer, 2),
        _layer_spec((1, D_SGU), layer, 2),
        _layer_spec((N_SGU_HEADS, CHUNK, CHUNK), layer, 2),
        _layer_spec((N_SGU_HEADS, CHUNK, LANES), layer, 2),
    ] + [pl.BlockSpec(memory_space=pl.ANY)] * n_alias
    out_specs = [
        pl.BlockSpec((1, TM, D_MODEL), lambda b, j: (b, j, 0)),
        pl.BlockSpec((None, 1, WINDOW, D_KV), lambda b, j: (layer, b, 0, 0)),
        pl.BlockSpec((None, 1, WINDOW, D_KV), lambda b, j: (layer, b, 0, 0)),
        pl.BlockSpec((None, 1, POOL_HIST, D_POOL), lambda b, j: (layer, b, 0, 0)),
    ]
    out_shape = [
        jax.ShapeDtypeStruct((BATCH, SEQ, D_MODEL), F32),
        jax.ShapeDtypeStruct((DEPTH, BATCH, WINDOW, D_KV), F32),
        jax.ShapeDtypeStruct((DEPTH, BATCH, WINDOW, D_KV), F32),
        jax.ShapeDtypeStruct((DEPTH, BATCH, POOL_HIST, D_POOL), F32),
    ]
    scratch = [
        pltpu.VMEM((SUB, D_ATT), BF16),
        pltpu.VMEM((2 * N_KV, WINDOW + TM, LANES), BF16),
        pltpu.VMEM((N_KV, WINDOW + TM, D_KV), BF16),
        pltpu.VMEM((HIST_PAD + TM, D_POOL), F32),
        pltpu.VMEM((SUB, D_MODEL), BF16),
        pltpu.VMEM((SUB, Z_WIDTH), F32),
        pltpu.VMEM((1, SUB, D_MODEL), F32),
    ]
    n_in = len(in_specs) - n_alias
    return pl.pallas_call(
        functools.partial(_prompt_body, layer, n_alias),
        grid=grid,
        in_specs=in_specs,
        out_specs=out_specs,
        out_shape=out_shape,
        scratch_shapes=scratch,
        input_output_aliases={n_in + i: 1 + i for i in range(n_alias)},
        compiler_params=pltpu.CompilerParams(
            dimension_semantics=("arbitrary", "arbitrary"),
            vmem_limit_bytes=VMEM_LIMIT_PROMPT),
        name="prompt_layer",
    )(sinks, x, cos, sin, w_in_b, w_out_b, npre, npost, pw_b, ps, lng, lnb, sw, sb_full, *prev_outs)


def _sample_proj_body(x_ref, npre_ref, w_ref, z_ref, hbuf):
    @pl.when(pl.program_id(0) == 0)
    def _norm_once():
        hbuf[...] = _rms(x_ref[...].reshape(SAMPLE_ROWS, D_MODEL), npre_ref[...]).astype(BF16)

    z_ref[...] = _dot(hbuf[...], w_ref[...])


def _sample_proj(layer, xs, npre, w_in_b):
    return pl.pallas_call(
        _sample_proj_body,
        grid=(D_IN // S1_TN,),
        in_specs=[
            pl.BlockSpec(xs.shape, lambda n: (0,) * xs.ndim),
            pl.BlockSpec((None, 1, D_MODEL), lambda n: (layer, 0, 0)),
            pl.BlockSpec((None, D_MODEL, S1_TN), lambda n: (layer, 0, n)),
        ],
        out_specs=pl.BlockSpec((SAMPLE_ROWS, S1_TN), lambda n: (0, n)),
        out_shape=jax.ShapeDtypeStruct((SAMPLE_ROWS, D_IN), F32),
        scratch_shapes=[pltpu.VMEM((SAMPLE_ROWS, D_MODEL), BF16)],
        compiler_params=pltpu.CompilerParams(
            dimension_semantics=("arbitrary",), vmem_limit_bytes=VMEM_LIMIT_SAMPLE),
        name="sample_proj",
    )(xs, npre, w_in_b)


SCORE_ROWS = N_HEADS * GR
N_REST_BLOCKS = (D_IN - C_XB) // LANES


def _sample_mix_body(n_alias, z_ref, cos_ref, sin_ref, sink_ref, ck_ref, cv_ref, st_ref,
                     pw_ref, ps_ref, lng_ref, lnb_ref, wexp_ref, bexp_ref, *rest):
    (mix_ref, nk_ref, nv_ref, npool_ref, vn_ref,
     qbig, knew_f, vnew_f, knew_t, vnew_t, zc, mo, vo) = rest[n_alias:]
    cos = cos_ref[...]
    sin = sin_ref[...]

    q = _rope(z_ref[:, C_Q:C_Q + D_ATT], cos * ATT_SCALE, sin * ATT_SCALE)
    k_new = _rope(z_ref[:, C_K:C_K + D_KV], cos, sin)
    v_new = z_ref[:, C_V:C_V + D_KV]
    knew_f[...] = k_new
    vnew_f[...] = v_new
    knew_t[...] = k_new.T
    vnew_t[...] = v_new.T
    low = lax.broadcasted_iota(jnp.int32, (RT, LANES), 1) < HEAD_DIM
    zero_blk = jnp.zeros((RT, LANES), F32)
    for h in range(N_HEADS):
        g = h // GROUP
        src = q[:, (h // 2) * LANES:(h // 2 + 1) * LANES]
        if h % 2 != g % 2:
            src = _swap_heads(src)
        piece = jnp.where(low, src, 0.0) if g % 2 == 0 else jnp.where(low, 0.0, src)
        full = [zero_blk, zero_blk]
        full[g // 2] = piece
        qbig[h] = jnp.concatenate(full, axis=1).astype(BF16)

    row_h = lax.broadcasted_iota(jnp.int32, (SCORE_ROWS, GB * WINDOW), 0)
    col_h = lax.broadcasted_iota(jnp.int32, (SCORE_ROWS, GB * WINDOW), 1)
    same_h = ((row_h >> 2) & (GB - 1)) == (col_h >> 7)
    bias_h = jnp.where(same_h, jnp.where((col_h & (WINDOW - 1)) > (row_h & (DEC_SEQ - 1)), 0.0, NEG_INF),
                       NEG_INF)
    row_n = lax.broadcasted_iota(jnp.int32, (SCORE_ROWS, GR), 0)
    col_n = lax.broadcasted_iota(jnp.int32, (SCORE_ROWS, GR), 1)
    same_n = ((row_n >> 2) & (GB - 1)) == (col_n >> 2)
    bias_n = jnp.where(same_n, jnp.where((col_n & (DEC_SEQ - 1)) <= (row_n & (DEC_SEQ - 1)), 0.0, NEG_INF),
                       NEG_INF)
    sink = sink_ref[:, 0:1]
    low_g = lax.broadcasted_iota(jnp.int32, (GR, LANES), 1) < HEAD_DIM
    key_lane = lax.broadcasted_iota(jnp.int32, (D_KV, WINDOW), 1)
    src_key = lax.broadcasted_iota(jnp.int32, (WINDOW, WINDOW), 0)
    dst_key = lax.broadcasted_iota(jnp.int32, (WINDOW, WINDOW), 1)
    shift_keys = jnp.where(src_key == dst_key + DEC_SEQ, 1.0, 0.0).astype(BF16)

    def roll_in(hist_t, new_t, bi):
        hi = hist_t.astype(BF16)
        rest1 = hist_t - hi.astype(F32)
        mid = rest1.astype(BF16)
        lo = (rest1 - mid.astype(F32)).astype(BF16)
        rolled = _dot(hi, shift_keys) + _dot(mid, shift_keys) + _dot(lo, shift_keys)
        placed = pltpu.roll(new_t, (WINDOW - DEC_SEQ - DEC_SEQ * bi) % RT, 1)
        return jnp.where(key_lane >= WINDOW - DEC_SEQ, placed, rolled)

    for grp in range(BT // GB):
        r0 = grp * GR
        lhs = qbig[:, r0:r0 + GR, :].reshape(SCORE_ROWS, D_KV)
        kh_t = jnp.concatenate([ck_ref[grp * GB + bb] for bb in range(GB)], axis=1)
        vh_t = jnp.concatenate([cv_ref[grp * GB + bb] for bb in range(GB)], axis=1)
        kn = knew_f[r0:r0 + GR, :]
        vn_new = vnew_f[r0:r0 + GR, :]
        s_h = _dot(lhs, kh_t.astype(BF16)) + bias_h
        s_n = _dot_nt(lhs, kn.astype(BF16)) + bias_n
        m = jnp.maximum(jnp.maximum(jnp.max(s_h, axis=1, keepdims=True),
                                    jnp.max(s_n, axis=1, keepdims=True)), sink)
        p_h = jnp.exp(s_h - m)
        p_n = jnp.exp(s_n - m)
        den = (jnp.sum(p_h, axis=1, keepdims=True) + jnp.sum(p_n, axis=1, keepdims=True)
               + jnp.exp(sink - m))
        inv = 1.0 / den
        o = (_dot_nt((p_h * inv).astype(BF16), vh_t.astype(BF16))
             + _dot((p_n * inv).astype(BF16), vn_new.astype(BF16)))
        for c in range(N_HEADS // 2):
            g = c // 2
            a = o[(2 * c) * GR:(2 * c + 1) * GR, (g // 2) * LANES:(g // 2 + 1) * LANES]
            b = o[(2 * c + 1) * GR:(2 * c + 2) * GR, (g // 2) * LANES:(g // 2 + 1) * LANES]
            blk = jnp.where(low_g, a, _swap_heads(b)) if g % 2 == 0 else jnp.where(low_g, _swap_heads(a), b)
            gate = _silu(z_ref[r0:r0 + GR, C_GA + c * LANES:C_GA + (c + 1) * LANES])
            mix_ref[r0:r0 + GR, c * LANES:(c + 1) * LANES] = blk * gate
        for bb in range(GB):
            bi = grp * GB + bb
            nk_ref[bi] = roll_in(ck_ref[bi], knew_t[...], bi)
            nv_ref[bi] = roll_in(cv_ref[bi], vnew_t[...], bi)

    for c in range(N_REST_BLOCKS):
        zc[c] = z_ref[:, C_XB + c * LANES:C_XB + (c + 1) * LANES]

    def tok(t, lo, width):
        blk0 = (lo - C_XB) // LANES
        parts = [zc[blk0 + c, pl.ds(t, BT, stride=DEC_SEQ), :] for c in range(width // LANES)]
        return parts[0] if len(parts) == 1 else jnp.concatenate(parts, axis=1)

    def ext(i, gi):
        if i < POOL_HIST:
            return st_ref[i, :, gi * LANES:(gi + 1) * LANES]
        return tok(i - POOL_HIST, C_XB + gi * LANES, LANES)

    for gi, w in enumerate(POOL_WINDOWS):
        for s in range(POOL_HIST):
            npool_ref[s, :, gi * LANES:(gi + 1) * LANES] = ext(s + DEC_SEQ, gi)
        pooled = []
        for t in range(DEC_SEQ):
            cur = ext(POOL_HIST + t, gi)
            acc = cur
            for i in range(1, w):
                acc = acc + ext(POOL_HIST + t - i, gi)
            cnt = float(min(w, PAST_LEN + t + 1))
            pooled.append(acc / cnt - cur)
        pooled = jnp.concatenate(pooled, axis=0).astype(BF16)
        mixed = _dot(pooled, pw_ref[gi]) * ps_ref[:, gi * LANES:(gi + 1) * LANES]
        for t in range(DEC_SEQ):
            gate = _silu(tok(t, C_GB + gi * LANES, LANES))
            mo[gi, pl.ds(t, BT, stride=DEC_SEQ), :] = mixed[t * BT:(t + 1) * BT] * gate

    vns = []
    for t in range(DEC_SEQ):
        vn_t = _layernorm(tok(t, C_VS, D_SGU), lng_ref[...], lnb_ref[...])
        for c in range(D_SGU // LANES):
            vo[c, pl.ds(t, BT, stride=DEC_SEQ), :] = vn_t[:, c * LANES:(c + 1) * LANES]
        vns.append(vn_t)
    for t in range(DEC_SEQ):
        mixed = bexp_ref[t:t + 1, :]
        for s in range(t + 1):
            mixed = mixed + wexp_ref[DEC_SEQ * t + s:DEC_SEQ * t + s + 1, :] * vns[s]
        out = tok(t, C_U, D_SGU) * mixed * _silu(tok(t, C_GC, D_SGU))
        for c in range(D_SGU // LANES):
            mo[D_POOL // LANES + c, pl.ds(t, BT, stride=DEC_SEQ), :] = out[:, c * LANES:(c + 1) * LANES]

    for c in range((D_POOL + D_SGU) // LANES):
        mix_ref[:, D_ATT + c * LANES:D_ATT + (c + 1) * LANES] = mo[c]
    for c in range(D_SGU // LANES):
        vn_ref[:, c * LANES:(c + 1) * LANES] = vo[c]


def _sample_mix(layer, z, cos_t, sin_t, params, ck, cv, state2, prev_outs):
    (sink_rows, pw_b, ps, lng, lnb, wexp, bexp) = params
    n_alias = len(prev_outs)
    const = lambda shape: pl.BlockSpec(shape, lambda i: (0,) * len(shape))
    in_specs = [
        pl.BlockSpec((RT, D_IN), lambda i: (i, 0)),
        const((RT, LANES)), const((RT, LANES)),
        _layer_spec((SCORE_ROWS, LANES), layer, 1),
        pl.BlockSpec((None, BT, D_KV, WINDOW), lambda i: (layer, i, 0, 0)),
        pl.BlockSpec((None, BT, D_KV, WINDOW), lambda i: (layer, i, 0, 0)),
        pl.BlockSpec((None, POOL_HIST, BT, D_POOL), lambda i: (layer, 0, i, 0)),
        _layer_spec((len(POOL_WINDOWS), LANES, LANES), layer, 1),
        _layer_spec((1, D_POOL), layer, 1),
        _layer_spec((1, D_SGU), layer, 1),
        _layer_spec((1, D_SGU), layer, 1),
        _layer_spec((DEC_SEQ * DEC_SEQ, D_SGU), layer, 1),
        _layer_spec((DEC_SEQ, D_SGU), layer, 1),
    ] + [pl.BlockSpec(memory_space=pl.ANY)] * n_alias
    out_specs = [
        pl.BlockSpec((RT, D_MODEL), lambda i: (i, 0)),
        pl.BlockSpec((None, BT, D_KV, WINDOW), lambda i: (layer, i, 0, 0)),
        pl.BlockSpec((None, BT, D_KV, WINDOW), lambda i: (layer, i, 0, 0)),
        pl.BlockSpec((None, POOL_HIST, BT, D_POOL), lambda i: (layer, 0, i, 0)),
        pl.BlockSpec((None, RT, D_SGU), lambda i: (layer, i, 0)),
    ]
    out_shape = [
        jax.ShapeDtypeStruct((SAMPLE_ROWS, D_MODEL), F32),
        jax.ShapeDtypeStruct((DEPTH, DEC_BATCH, D_KV, WINDOW), F32),
        jax.ShapeDtypeStruct((DEPTH, DEC_BATCH, D_KV, WINDOW), F32),
        jax.ShapeDtypeStruct((DEPTH, POOL_HIST, DEC_BATCH, D_POOL), F32),
        jax.ShapeDtypeStruct((DEPTH, SAMPLE_ROWS, D_SGU), F32),
    ]
    scratch = [
        pltpu.VMEM((N_HEADS, RT, D_KV), BF16),
        pltpu.VMEM((RT, D_KV), F32),
        pltpu.VMEM((RT, D_KV), F32),
        pltpu.VMEM((D_KV, RT), F32),
        pltpu.VMEM((D_KV, RT), F32),
        pltpu.VMEM((N_REST_BLOCKS, RT, LANES), F32),
        pltpu.VMEM(((D_POOL + D_SGU) // LANES, RT, LANES), F32),
        pltpu.VMEM((D_SGU // LANES, RT, LANES), F32),
    ]
    n_in = len(in_specs) - n_alias
    return pl.pallas_call(
        functools.partial(_sample_mix_body, n_alias),
        grid=(DEC_BATCH // BT,),
        in_specs=in_specs,
        out_specs=out_specs,
        out_shape=out_shape,
        scratch_shapes=scratch,
        input_output_aliases={n_in + i: 1 + i for i in range(n_alias)},
        compiler_params=pltpu.CompilerParams(
            dimension_semantics=("arbitrary",), vmem_limit_bytes=VMEM_LIMIT_SAMPLE),
        name="sample_mix",
    )(z, cos_t, sin_t, sink_rows, ck, cv, state2, pw_b, ps, lng, lnb, wexp, bexp, *prev_outs)


def _sample_out_body(x_ref, mix_ref, w_out_ref, npost_ref, y_ref):
    y = _dot(mix_ref[...].astype(BF16), w_out_ref[...])
    out = x_ref[...].reshape(SAMPLE_ROWS, D_MODEL) + _rms(y, npost_ref[...])
    y_ref[...] = out.reshape(y_ref.shape)


def _sample_out(layer, xs, mix, w_out_b, npost, out_shape):
    full = lambda shape: pl.BlockSpec(shape, lambda i: (0,) * len(shape))
    return pl.pallas_call(
        _sample_out_body,
        grid=(1,),
        in_specs=[
            full(xs.shape),
            full((SAMPLE_ROWS, D_MODEL)),
            pl.BlockSpec((None, D_MODEL, D_MODEL), lambda i: (layer, 0, 0)),
            pl.BlockSpec((None, 1, D_MODEL), lambda i: (layer, 0, 0)),
        ],
        out_specs=full(out_shape),
        out_shape=jax.ShapeDtypeStruct(out_shape, F32),
        compiler_params=pltpu.CompilerParams(
            dimension_semantics=("arbitrary",), vmem_limit_bytes=VMEM_LIMIT_SAMPLE),
        name="sample_out",
    )(xs, mix, w_out_b, npost)


def _rope_tables(pos):
    inv = ROPE_THETA ** (-jnp.arange(0, HEAD_DIM, 2, dtype=F32) / HEAD_DIM)
    ang = pos.astype(F32)[:, None] * inv[None, :]
    c, s = jnp.cos(ang), jnp.sin(ang)
    return jnp.concatenate([c, c, c, c], axis=1), jnp.concatenate([-s, s, -s, s], axis=1)


def kernel(x_prompt, x_sample, cache_k, cache_v, state_pool, w_in, w_out, norm_pre, norm_post,
           attn_sinks, pool_w, pool_scale, sgu_ln_g, sgu_ln_b, sgu_w, sgu_b):
    cos_p, sin_p = _rope_tables(jnp.arange(SEQ))
    cos_s, sin_s = _rope_tables(PAST_LEN + jnp.arange(DEC_SEQ))
    cos_t = jnp.tile(cos_s, (BT, 1))
    sin_t = jnp.tile(sin_s, (BT, 1))

    w_in_b = w_in.astype(BF16)
    w_out_b = w_out.astype(BF16)
    pw_b = pool_w.astype(BF16)
    npre = norm_pre[:, None, :]
    npost = norm_post[:, None, :]
    ps = pool_scale[:, None, :]
    lng = sgu_ln_g[:, None, :]
    lnb = sgu_ln_b[:, None, :]
    sb_full = jnp.broadcast_to(sgu_b[:, :, :, None], (DEPTH, N_SGU_HEADS, CHUNK, LANES))
    wexp = jnp.repeat(sgu_w[:, :, :DEC_SEQ, :DEC_SEQ].transpose(0, 2, 3, 1), LANES, axis=-1
                      ).reshape(DEPTH, DEC_SEQ * DEC_SEQ, D_SGU)
    bexp = jnp.repeat(sgu_b[:, :, :DEC_SEQ].transpose(0, 2, 1), LANES, axis=-1)
    sink_rows = jnp.broadcast_to(attn_sinks[:, :, None, None], (DEPTH, N_HEADS, GR, LANES)
                                 ).reshape(DEPTH, SCORE_ROWS, LANES)

    xp = x_prompt
    xs = x_sample
    to_key_minor = lambda c: c.transpose(0, 1, 3, 4, 2).reshape(DEPTH, DEC_BATCH, D_KV, WINDOW)
    from_key_minor = lambda c: c.reshape(DEPTH, DEC_BATCH, N_KV, HEAD_DIM, WINDOW).transpose(0, 1, 4, 2, 3)
    ck = to_key_minor(cache_k)
    cv = to_key_minor(cache_v)
    state2 = state_pool.transpose(0, 2, 1, 3)
    prompt_params = (attn_sinks, w_in_b, w_out_b, npre, npost, pw_b, ps, lng, lnb, sgu_w, sb_full)
    sample_params = (sink_rows, pw_b, ps, lng, lnb, wexp, bexp)
    p_outs, s_outs = [], []
    for layer in range(DEPTH):
        xp, *p_outs = _prompt_layer(layer, xp, cos_p, sin_p, prompt_params, p_outs)
        z = _sample_proj(layer, xs, npre, w_in_b)
        mix, *s_outs = _sample_mix(layer, z, cos_t, sin_t, sample_params, ck, cv, state2, s_outs)
        xs_shape = x_sample.shape if layer == DEPTH - 1 else (SAMPLE_ROWS, D_MODEL)
        xs = _sample_out(layer, xs, mix, w_out_b, npost, xs_shape)

    nk_p, nv_p, np_p = p_outs
    nk_s, nv_s, np_s, vn_s = s_outs
    kv5 = lambda a, nb: a.reshape(DEPTH, nb, WINDOW, N_KV, HEAD_DIM)
    return (xp, xs,
            kv5(nk_p, BATCH), kv5(nv_p, BATCH), np_p,
            from_key_minor(nk_s), from_key_minor(nv_s),
            np_s.transpose(0, 2, 1, 3),
            vn_s.reshape(DEPTH, DEC_BATCH, DEC_SEQ, D_SGU))
```

```python
import functools

import jax
import jax.numpy as jnp
import numpy as np
from jax import lax
from jax.experimental import pallas as pl
from jax.experimental.pallas import tpu as pltpu

D_MODEL = 2048
SEQ = 2048
BATCH = 8
DEPTH = 2
DEC_BATCH = 128
DEC_SEQ = 4
PAST_LEN = 8192
HEAD_DIM = 64
HALF_DIM = HEAD_DIM // 2
N_HEADS = 16
N_KV = 4
GROUP = N_HEADS // N_KV
D_ATT = N_HEADS * HEAD_DIM
D_KV = N_KV * HEAD_DIM
WINDOW = 128
ROPE_THETA = 10000.0
D_POOL = 512
POOL_WINDOWS = (2, 4, 8, 16)
POOL_HIST = 15
D_SGU = 512
CHUNK = 128
N_SGU_HEADS = 4
D_IN = 5120
EPS = 1e-6
ATT_SCALE = HEAD_DIM ** -0.5
LOG2_E = 1.4426950408889634

C_Q = 0
C_K = C_Q + D_ATT
C_V = C_K + D_KV
C_GA = C_V + D_KV
C_XB = C_GA + D_ATT
C_GB = C_XB + D_POOL
C_U = C_GB + D_POOL
C_VS = C_U + D_SGU
C_GC = C_VS + D_SGU
assert C_GC + D_SGU == D_IN

LANES = 128
SUBLANES = 8
HIST_PAD = 16
SUB = 256
N_SUB = 2
TM = SUB * N_SUB
N_CHUNKS = SUB // CHUNK
SAMPLE_ROWS = DEC_BATCH * DEC_SEQ
BT = 32
RT = BT * DEC_SEQ
assert RT == WINDOW
GB = 4
GR = GB * DEC_SEQ
ATT_PIECE = 512
REST_PIECE = 256
N_REST_PIECES = (D_IN - C_XB) // REST_PIECE
Z_U = D_POOL
Z_VS = Z_U + D_SGU
Z_WIDTH = Z_VS + D_SGU
S1_TN = 1024
VMEM_LIMIT_PROMPT = 62 * 1024 * 1024
VMEM_LIMIT_SAMPLE = 56 * 1024 * 1024

F32 = jnp.float32
BF16 = jnp.bfloat16
NEG_INF = float("-inf")


def _rms(x, g):
    return x * lax.rsqrt(jnp.mean(x * x, axis=-1, keepdims=True) + EPS) * g


def _silu(x):
    return x * (1.0 / (1.0 + jnp.exp(-x)))


def _layernorm(x, g, b):
    mu = jnp.mean(x, axis=-1, keepdims=True)
    xc = x - mu
    var = jnp.mean(xc * xc, axis=-1, keepdims=True)
    return xc * lax.rsqrt(var + EPS) * g + b


def _swap_heads(x):
    return pltpu.roll(x, HEAD_DIM, 1)


def _rope(x, cos, sin):
    rows = x.shape[0]
    lane = lax.broadcasted_iota(jnp.int32, (rows, LANES), 1)
    first_half = (lane & HALF_DIM) == 0
    outs = []
    for c in range(x.shape[1] // LANES):
        xc = x[:, c * LANES:(c + 1) * LANES]
        partner = jnp.where(first_half,
                            pltpu.roll(xc, LANES - HALF_DIM, 1),
                            pltpu.roll(xc, HALF_DIM, 1))
        outs.append(xc * cos + partner * sin)
    return jnp.concatenate(outs, axis=1)


def _dot(a, b):
    return jnp.dot(a, b, preferred_element_type=F32)


def _dot_nt(a, b):
    return lax.dot_general(a, b, (((1,), (1,)), ((), ())), preferred_element_type=F32)


def _prompt_body(layer, n_alias, sinks_ref, x_ref, cos_ref, sin_ref, w_in_ref, w_out_ref,
                 npre_ref, npost_ref, pw_ref, ps_ref, lng_ref, lnb_ref, sw_ref, sb_ref, *rest):
    refs = rest[n_alias:]
    kbuf, vbuf, xbext = refs[5], refs[6], refs[7]
    j = pl.program_id(1)

    @pl.when(j == 0)
    def _start_of_sequence():
        kbuf[:, 0:WINDOW, :] = jnp.zeros((2 * N_KV, WINDOW, LANES), BF16)
        vbuf[:, 0:WINDOW, :] = jnp.zeros((N_KV, WINDOW, D_KV), BF16)
        xbext[0:HIST_PAD, :] = jnp.zeros((HIST_PAD, D_POOL), F32)

    @pl.when(j > 0)
    def _carry_from_previous_step():
        for i in range(2 * N_KV):
            kbuf[i, 0:WINDOW, :] = kbuf[i, TM:TM + WINDOW, :]
        for g in range(N_KV):
            vbuf[g, 0:WINDOW, :] = vbuf[g, TM:TM + WINDOW, :]
        xbext[0:HIST_PAD, :] = xbext[TM:TM + HIST_PAD, :]

    for sub in range(N_SUB):
        _prompt_sub_block(layer, sub, j, sinks_ref, x_ref, cos_ref, sin_ref, w_in_ref, w_out_ref,
                          npre_ref, npost_ref, pw_ref, ps_ref, lng_ref, lnb_ref, sw_ref, sb_ref, *refs)


def _prompt_sub_block(layer, sub, j, sinks_ref, x_ref, cos_ref, sin_ref, w_in_ref, w_out_ref,
                      npre_ref, npost_ref, pw_ref, ps_ref, lng_ref, lnb_ref, sw_ref, sb_ref,
                      y_ref, nk_ref, nv_ref, np_ref, qbuf, kbuf, vbuf, xbext, mixbuf, zrest, ybuf):
    base = sub * SUB
    sub_rows = slice(base, base + SUB)
    first_in_sequence = (j == 0) if sub == 0 else None
    h = _rms(x_ref[0, sub_rows], npre_ref[...]).astype(BF16)

    def proj(lo, width):
        return _dot(h, w_in_ref[:, lo:lo + width])

    cos = cos_ref[sub_rows, :]
    sin = sin_ref[sub_rows, :]

    k = _rope(proj(C_K, D_KV), cos, sin)
    v = proj(C_V, D_KV)
    cos_q = cos * (ATT_SCALE * LOG2_E)
    sin_q = sin * (ATT_SCALE * LOG2_E)
    for part in range(D_ATT // ATT_PIECE):
        cols = slice(part * ATT_PIECE, (part + 1) * ATT_PIECE)
        qbuf[:, cols] = _rope(proj(C_Q + part * ATT_PIECE, ATT_PIECE), cos_q, sin_q).astype(BF16)
    for part in range(D_ATT // ATT_PIECE):
        cols = slice(part * ATT_PIECE, (part + 1) * ATT_PIECE)
        mixbuf[:, cols] = _silu(proj(C_GA + part * ATT_PIECE, ATT_PIECE)).astype(BF16)
    if sub == N_SUB - 1:
        nk_ref[0] = k[SUB - WINDOW:SUB].T
        nv_ref[0] = v[SUB - WINDOW:SUB].T
    lane_k = lax.broadcasted_iota(jnp.int32, (SUB, LANES), 1)
    low = lane_k < HEAD_DIM
    lane_v = lax.broadcasted_iota(jnp.int32, (SUB, D_KV), 1)
    def store_by_chunk(buf, idx, val):
        val = val.astype(BF16)
        for c in range(N_CHUNKS):
            r0 = WINDOW + base + c * CHUNK
            buf[idx, r0:r0 + CHUNK, :] = val[c * CHUNK:(c + 1) * CHUNK]

    for pair in range(2):
        kp = k[:, pair * LANES:(pair + 1) * LANES]
        ks = _swap_heads(kp)
        store_by_chunk(kbuf, 4 * pair + 0, jnp.where(low, kp, 0.0))
        store_by_chunk(kbuf, 4 * pair + 1, jnp.where(low, 0.0, ks))
        store_by_chunk(kbuf, 4 * pair + 2, jnp.where(low, ks, 0.0))
        store_by_chunk(kbuf, 4 * pair + 3, jnp.where(low, 0.0, kp))
    for g in range(N_KV):
        keep_v = (lane_v >= g * HEAD_DIM) & (lane_v < (g + 1) * HEAD_DIM)
        store_by_chunk(vbuf, g, jnp.where(keep_v, v, 0.0))

    qi = lax.broadcasted_iota(jnp.int32, (CHUNK, 2 * WINDOW), 0)
    kj = lax.broadcasted_iota(jnp.int32, (CHUNK, 2 * WINDOW), 1)
    dist = qi + WINDOW - kj
    band_bias = jnp.where((dist >= 0) & (dist < WINDOW), 0.0, NEG_INF)
    if first_in_sequence is None:
        first_bias = band_bias
    else:
        no_prev = jnp.where(first_in_sequence, WINDOW, 0)
        first_bias = jnp.where(kj < no_prev, NEG_INF, band_bias)
    low_c = lax.broadcasted_iota(jnp.int32, (CHUNK, LANES), 1) < HEAD_DIM
    rest_piece = 0

    def project_rest_piece(piece):
        cols = slice(piece * REST_PIECE, (piece + 1) * REST_PIECE)
        lo = C_XB + piece * REST_PIECE
        val = proj(lo, REST_PIECE)
        if C_GB <= lo < C_U:
            mixbuf[:, D_ATT + lo - C_GB:D_ATT + lo - C_GB + REST_PIECE] = _silu(val).astype(BF16)
        elif lo >= C_GC:
            at = D_ATT + D_POOL + lo - C_GC
            mixbuf[:, at:at + REST_PIECE] = _silu(val).astype(BF16)
        else:
            at = lo - C_XB if lo < C_GB else (Z_U + lo - C_U if lo < C_VS else Z_VS + lo - C_VS)
            zrest[:, at:at + REST_PIECE] = val

    def scores(c, g):
        r0 = c * CHUNK
        qs = jnp.concatenate(
            [qbuf[r0:r0 + CHUNK, (2 * g + i) * LANES:(2 * g + i + 1) * LANES] for i in range(2)],
            axis=0)
        return [_dot_nt(qs, kbuf[2 * g + half, base + r0:base + r0 + 2 * WINDOW, :]) for half in range(2)]

    order = [(c, g) for c in range(N_CHUNKS) for g in range(N_KV)]
    while rest_piece < N_REST_PIECES - len(order):
        project_rest_piece(rest_piece)
        rest_piece += 1
    s_next = scores(*order[0])
    for step, (c, g) in enumerate(order):
        bias = first_bias if c == 0 else band_bias
        r0 = c * CHUNK
        s_half = s_next
        if step + 1 < len(order):
            s_next = scores(*order[step + 1])
        blocks = []
        for r in range(GROUP):
            sb = s_half[r % 2][(r // 2) * CHUNK:(r // 2 + 1) * CHUNK] + bias
            sink = sinks_ref[layer, GROUP * g + r] * LOG2_E
            m = jnp.maximum(jnp.max(sb, axis=1, keepdims=True), sink)
            pe = jnp.exp2(sb - m)
            den = jnp.sum(pe, axis=1, keepdims=True) + jnp.exp2(sink - m)
            blocks.append((pe * (1.0 / den)).astype(BF16))
        pg = jnp.concatenate(blocks, axis=0)
        og = _dot(pg, vbuf[g, base + r0:base + r0 + 2 * WINDOW, :])
        o = og if g == 0 else o + og
        if rest_piece < N_REST_PIECES:
            project_rest_piece(rest_piece)
            rest_piece += 1
        if g < N_KV - 1:
            continue
        for pair in range(2):
            for i in range(2):
                a = o[(2 * i) * CHUNK:(2 * i + 1) * CHUNK, pair * LANES:(pair + 1) * LANES]
                b = o[(2 * i + 1) * CHUNK:(2 * i + 2) * CHUNK, pair * LANES:(pair + 1) * LANES]
                for odd, blk in ((0, jnp.where(low_c, a, _swap_heads(b))),
                                 (1, jnp.where(low_c, _swap_heads(a), b))):
                    cols = slice((2 * (2 * pair + odd) + i) * LANES, (2 * (2 * pair + odd) + i + 1) * LANES)
                    mixbuf[r0:r0 + CHUNK, cols] = (blk * mixbuf[r0:r0 + CHUNK, cols]).astype(BF16)

    while rest_piece < N_REST_PIECES:
        project_rest_piece(rest_piece)
        rest_piece += 1

    xb0 = HIST_PAD + base
    xbext[xb0:xb0 + SUB, :] = zrest[:, 0:D_POOL]
    if sub == N_SUB - 1:
        np_ref[0] = xbext[HIST_PAD + TM - POOL_HIST:HIST_PAD + TM, :]
    pos1 = j * TM + base + lax.broadcasted_iota(jnp.int32, (SUB, LANES), 0) + 1
    for gi, w in enumerate(POOL_WINDOWS):
        cols = slice(gi * LANES, (gi + 1) * LANES)
        cur = xbext[xb0:xb0 + SUB, cols]
        acc = cur
        for i in range(1, w):
            acc = acc + xbext[xb0 - i:xb0 - i + SUB, cols]
        cnt = jnp.minimum(w, pos1).astype(F32)
        pooled = acc / cnt - cur
        mixed = _dot(pooled.astype(BF16), pw_ref[gi]) * ps_ref[:, cols]
        out_cols = slice(D_ATT + gi * LANES, D_ATT + (gi + 1) * LANES)
        mixbuf[:, out_cols] = (mixed * mixbuf[:, out_cols]).astype(BF16)

    vn = _layernorm(zrest[:, Z_VS:Z_VS + D_SGU], lng_ref[...], lnb_ref[...]).astype(BF16)
    ti = lax.broadcasted_iota(jnp.int32, (CHUNK, CHUNK), 0)
    si = lax.broadcasted_iota(jnp.int32, (CHUNK, CHUNK), 1)
    for hh in range(N_SGU_HEADS):
        cols = slice(hh * LANES, (hh + 1) * LANES)
        wm = jnp.where(ti >= si, sw_ref[hh], 0.0).astype(BF16)
        for c in range(N_CHUNKS):
            rows = slice(c * CHUNK, (c + 1) * CHUNK)
            mixed = _dot(wm, vn[rows, cols]) + sb_ref[hh]
            out_cols = slice(D_ATT + D_POOL + hh * LANES, D_ATT + D_POOL + (hh + 1) * LANES)
            mixbuf[rows, out_cols] = (
                zrest[rows, Z_U + hh * LANES:Z_U + (hh + 1) * LANES] * mixed * mixbuf[rows, out_cols]
            ).astype(BF16)

    slot = jnp.minimum(j, 0)
    y = _dot(mixbuf[...], w_out_ref[...])
    ybuf[slot] = y
    if sub < N_SUB - 1:
        y = ybuf[slot]
    y_ref[0, sub_rows] = x_ref[0, sub_rows] + _rms(y, npost_ref[...])


def _layer_spec(shape, layer, grid_rank):
    zeros = (0,) * len(shape)
    if grid_rank == 2:
        index_map = lambda b, j: (layer,) + zeros
    else:
        index_map = lambda i: (layer,) + zeros
    return pl.BlockSpec((None,) + tuple(shape), index_map, pipeline_mode=pl.Buffered(1))


def _prompt_layer(layer, x, cos, sin, params, prev_outs):
    (sinks, w_in_b, w_out_b, npre, npost, pw_b, ps, lng, lnb, sw, sb_full) = params
    grid = (BATCH, SEQ // TM)
    n_alias = len(prev_outs)
    in_specs = [
        pl.BlockSpec(memory_space=pltpu.SMEM),
        pl.BlockSpec((1, TM, D_MODEL), lambda b, j: (b, j, 0)),
        pl.BlockSpec((TM, LANES), lambda b, j: (j, 0)),
        pl.BlockSpec((TM, LANES), lambda b, j: (j, 0)),
        _layer_spec((D_MODEL, D_IN), layer, 2),
        _layer_spec((D_MODEL, D_MODEL), layer, 2),
        _layer_spec((1, D_MODEL), layer, 2),
        _layer_spec((1, D_MODEL), layer, 2),
        _layer_spec((len(POOL_WINDOWS), LANES, LANES), layer, 2),
        _layer_spec((1, D_POOL), layer, 2),
        _layer_spec((1, D_SGU), layer, 2),
        _layer_spec((1, D_SGU), layer, 2),
        _layer_spec((N_SGU_HEADS, CHUNK, CHUNK), layer, 2),
        _layer_spec((N_SGU_HEADS, CHUNK, LANES), layer, 2),
    ] + [pl.BlockSpec(memory_space=pl.ANY)] * n_alias
    out_specs = [
        pl.BlockSpec((1, TM, D_MODEL), lambda b, j: (b, j, 0)),
        pl.BlockSpec((None, 1, D_KV, WINDOW), lambda b, j: (layer, b, 0, 0)),
        pl.BlockSpec((None, 1, D_KV, WINDOW), lambda b, j: (layer, b, 0, 0)),
        pl.BlockSpec((None, 1, POOL_HIST, D_POOL), lambda b, j: (layer, b, 0, 0)),
    ]
    out_shape = [
        jax.ShapeDtypeStruct((BATCH, SEQ, D_MODEL), F32),
        jax.ShapeDtypeStruct((DEPTH, BATCH, D_KV, WINDOW), F32),
        jax.ShapeDtypeStruct((DEPTH, BATCH, D_KV, WINDOW), F32),
        jax.ShapeDtypeStruct((DEPTH, BATCH, POOL_HIST, D_POOL), F32),
    ]
    scratch = [
        pltpu.VMEM((SUB, D_ATT), BF16),
        pltpu.VMEM((2 * N_KV, WINDOW + TM, LANES), BF16),
        pltpu.VMEM((N_KV, WINDOW + TM, D_KV), BF16),
        pltpu.VMEM((HIST_PAD + TM, D_POOL), F32),
        pltpu.VMEM((SUB, D_MODEL), BF16),
        pltpu.VMEM((SUB, Z_WIDTH), F32),
        pltpu.VMEM((1, SUB, D_MODEL), F32),
    ]
    n_in = len(in_specs) - n_alias
    return pl.pallas_call(
        functools.partial(_prompt_body, layer, n_alias),
        grid=grid,
        in_specs=in_specs,
        out_specs=out_specs,
        out_shape=out_shape,
        scratch_shapes=scratch,
        input_output_aliases={n_in + i: 1 + i for i in range(n_alias)},
        compiler_params=pltpu.CompilerParams(
            dimension_semantics=("arbitrary", "arbitrary"),
            vmem_limit_bytes=VMEM_LIMIT_PROMPT),
        name="prompt_layer",
    )(sinks, x, cos, sin, w_in_b, w_out_b, npre, npost, pw_b, ps, lng, lnb, sw, sb_full, *prev_outs)


def _sample_proj_body(x_ref, npre_ref, w_ref, z_ref, hbuf):
    @pl.when(pl.program_id(0) == 0)
    def _norm_once():
        hbuf[...] = _rms(x_ref[...].reshape(SAMPLE_ROWS, D_MODEL), npre_ref[...]).astype(BF16)

    z_ref[...] = _dot(hbuf[...], w_ref[...])


def _sample_proj(layer, xs, npre, w_in_b):
    return pl.pallas_call(
        _sample_proj_body,
        grid=(D_IN // S1_TN,),
        in_specs=[
            pl.BlockSpec(xs.shape, lambda n: (0,) * xs.ndim),
            pl.BlockSpec((None, 1, D_MODEL), lambda n: (layer, 0, 0)),
            pl.BlockSpec((None, D_MODEL, S1_TN), lambda n: (layer, 0, n)),
        ],
        out_specs=pl.BlockSpec((SAMPLE_ROWS, S1_TN), lambda n: (0, n)),
        out_shape=jax.ShapeDtypeStruct((SAMPLE_ROWS, D_IN), F32),
        scratch_shapes=[pltpu.VMEM((SAMPLE_ROWS, D_MODEL), BF16)],
        compiler_params=pltpu.CompilerParams(
            dimension_semantics=("arbitrary",), vmem_limit_bytes=VMEM_LIMIT_SAMPLE),
        name="sample_proj",
    )(xs, npre, w_in_b)


SCORE_ROWS = N_HEADS * GR
N_REST_BLOCKS = (D_IN - C_XB) // LANES


def _sample_mix_body(n_alias, z_ref, cos_ref, sin_ref, sink_ref, ck_ref, cv_ref, st_ref,
                     pw_ref, ps_ref, lng_ref, lnb_ref, wexp_ref, bexp_ref, *rest):
    (mix_ref, nk_ref, nv_ref, npool_ref, vn_ref,
     qbig, knew_f, vnew_f, knew_t, vnew_t, zc, mo, vo) = rest[n_alias:]
    cos = cos_ref[...]
    sin = sin_ref[...]

    q = _rope(z_ref[:, C_Q:C_Q + D_ATT], cos * ATT_SCALE, sin * ATT_SCALE)
    k_new = _rope(z_ref[:, C_K:C_K + D_KV], cos, sin)
    v_new = z_ref[:, C_V:C_V + D_KV]
    knew_f[...] = k_new
    vnew_f[...] = v_new
    knew_t[...] = k_new.T
    vnew_t[...] = v_new.T
    low = lax.broadcasted_iota(jnp.int32, (RT, LANES), 1) < HEAD_DIM
    zero_blk = jnp.zeros((RT, LANES), F32)
    for h in range(N_HEADS):
        g = h // GROUP
        src = q[:, (h // 2) * LANES:(h // 2 + 1) * LANES]
        if h % 2 != g % 2:
            src = _swap_heads(src)
        piece = jnp.where(low, src, 0.0) if g % 2 == 0 else jnp.where(low, 0.0, src)
        full = [zero_blk, zero_blk]
        full[g // 2] = piece
        qbig[h] = jnp.concatenate(full, axis=1).astype(BF16)

    row_h = lax.broadcasted_iota(jnp.int32, (SCORE_ROWS, GB * WINDOW), 0)
    col_h = lax.broadcasted_iota(jnp.int32, (SCORE_ROWS, GB * WINDOW), 1)
    same_h = ((row_h >> 2) & (GB - 1)) == (col_h >> 7)
    bias_h = jnp.where(same_h, jnp.where((col_h & (WINDOW - 1)) > (row_h & (DEC_SEQ - 1)), 0.0, NEG_INF),
                       NEG_INF)
    row_n = lax.broadcasted_iota(jnp.int32, (SCORE_ROWS, GR), 0)
    col_n = lax.broadcasted_iota(jnp.int32, (SCORE_ROWS, GR), 1)
    same_n = ((row_n >> 2) & (GB - 1)) == (col_n >> 2)
    bias_n = jnp.where(same_n, jnp.where((col_n & (DEC_SEQ - 1)) <= (row_n & (DEC_SEQ - 1)), 0.0, NEG_INF),
                       NEG_INF)
    sink = sink_ref[:, 0:1]
    low_g = lax.broadcasted_iota(jnp.int32, (GR, LANES), 1) < HEAD_DIM
    key_lane = lax.broadcasted_iota(jnp.int32, (D_KV, WINDOW), 1)
    src_key = lax.broadcasted_iota(jnp.int32, (WINDOW, WINDOW), 0)
    dst_key = lax.broadcasted_iota(jnp.int32, (WINDOW, WINDOW), 1)
    shift_keys = jnp.where(src_key == dst_key + DEC_SEQ, 1.0, 0.0).astype(BF16)

    def roll_in(hist_t, new_t, bi):
        hi = hist_t.astype(BF16)
        rest1 = hist_t - hi.astype(F32)
        mid = rest1.astype(BF16)
        lo = (rest1 - mid.astype(F32)).astype(BF16)
        rolled = _dot(hi, shift_keys) + _dot(mid, shift_keys) + _dot(lo, shift_keys)
        placed = pltpu.roll(new_t, (WINDOW - DEC_SEQ - DEC_SEQ * bi) % RT, 1)
        return jnp.where(key_lane >= WINDOW - DEC_SEQ, placed, rolled)

    for grp in range(BT // GB):
        r0 = grp * GR
        lhs = qbig[:, r0:r0 + GR, :].reshape(SCORE_ROWS, D_KV)
        kh_t = jnp.concatenate([ck_ref[grp * GB + bb] for bb in range(GB)], axis=1)
        vh_t = jnp.concatenate([cv_ref[grp * GB + bb] for bb in range(GB)], axis=1)
        kn = knew_f[r0:r0 + GR, :]
        vn_new = vnew_f[r0:r0 + GR, :]
        s_h = _dot(lhs, kh_t.astype(BF16)) + bias_h
        s_n = _dot_nt(lhs, kn.astype(BF16)) + bias_n
        m = jnp.maximum(jnp.maximum(jnp.max(s_h, axis=1, keepdims=True),
                                    jnp.max(s_n, axis=1, keepdims=True)), sink)
        p_h = jnp.exp(s_h - m)
        p_n = jnp.exp(s_n - m)
        den = (jnp.sum(p_h, axis=1, keepdims=True) + jnp.sum(p_n, axis=1, keepdims=True)
               + jnp.exp(sink - m))
        inv = 1.0 / den
        o = (_dot_nt((p_h * inv).astype(BF16), vh_t.astype(BF16))
             + _dot((p_n * inv).astype(BF16), vn_new.astype(BF16)))
        for c in range(N_HEADS // 2):
            g = c // 2
            a = o[(2 * c) * GR:(2 * c + 1) * GR, (g // 2) * LANES:(g // 2 + 1) * LANES]
            b = o[(2 * c + 1) * GR:(2 * c + 2) * GR, (g // 2) * LANES:(g // 2 + 1) * LANES]
            blk = jnp.where(low_g, a, _swap_heads(b)) if g % 2 == 0 else jnp.where(low_g, _swap_heads(a), b)
            gate = _silu(z_ref[r0:r0 + GR, C_GA + c * LANES:C_GA + (c + 1) * LANES])
            mix_ref[r0:r0 + GR, c * LANES:(c + 1) * LANES] = blk * gate
        for bb in range(GB):
            bi = grp * GB + bb
            nk_ref[bi] = roll_in(ck_ref[bi], knew_t[...], bi)
            nv_ref[bi] = roll_in(cv_ref[bi], vnew_t[...], bi)

    for c in range(N_REST_BLOCKS):
        zc[c] = z_ref[:, C_XB + c * LANES:C_XB + (c + 1) * LANES]

    def tok(t, lo, width):
        blk0 = (lo - C_XB) // LANES
        parts = [zc[blk0 + c, pl.ds(t, BT, stride=DEC_SEQ), :] for c in range(width // LANES)]
        return parts[0] if len(parts) == 1 else jnp.concatenate(parts, axis=1)

    def ext(i, gi):
        if i < POOL_HIST:
            return st_ref[i, :, gi * LANES:(gi + 1) * LANES]
        return tok(i - POOL_HIST, C_XB + gi * LANES, LANES)

    for gi, w in enumerate(POOL_WINDOWS):
        for s in range(POOL_HIST):
            npool_ref[s, :, gi * LANES:(gi + 1) * LANES] = ext(s + DEC_SEQ, gi)
        pooled = []
        for t in range(DEC_SEQ):
            cur = ext(POOL_HIST + t, gi)
            acc = cur
            for i in range(1, w):
                acc = acc + ext(POOL_HIST + t - i, gi)
            cnt = float(min(w, PAST_LEN + t + 1))
            pooled.append(acc / cnt - cur)
        pooled = jnp.concatenate(pooled, axis=0).astype(BF16)
        mixed = _dot(pooled, pw_ref[gi]) * ps_ref[:, gi * LANES:(gi + 1) * LANES]
        for t in range(DEC_SEQ):
            gate = _silu(tok(t, C_GB + gi * LANES, LANES))
            mo[gi, pl.ds(t, BT, stride=DEC_SEQ), :] = mixed[t * BT:(t + 1) * BT] * gate

    vns = []
    for t in range(DEC_SEQ):
        vn_t = _layernorm(tok(t, C_VS, D_SGU), lng_ref[...], lnb_ref[...])
        for c in range(D_SGU // LANES):
            vo[c, pl.ds(t, BT, stride=DEC_SEQ), :] = vn_t[:, c * LANES:(c + 1) * LANES]
        vns.append(vn_t)
    for t in range(DEC_SEQ):
        mixed = bexp_ref[t:t + 1, :]
        for s in range(t + 1):
            mixed = mixed + wexp_ref[DEC_SEQ * t + s:DEC_SEQ * t + s + 1, :] * vns[s]
        out = tok(t, C_U, D_SGU) * mixed * _silu(tok(t, C_GC, D_SGU))
        for c in range(D_SGU // LANES):
            mo[D_POOL // LANES + c, pl.ds(t, BT, stride=DEC_SEQ), :] = out[:, c * LANES:(c + 1) * LANES]

    for c in range((D_POOL + D_SGU) // LANES):
        mix_ref[:, D_ATT + c * LANES:D_ATT + (c + 1) * LANES] = mo[c]
    for c in range(D_SGU // LANES):
        vn_ref[:, c * LANES:(c + 1) * LANES] = vo[c]


def _sample_mix(layer, z, cos_t, sin_t, params, ck, cv, state2, prev_outs):
    (sink_rows, pw_b, ps, lng, lnb, wexp, bexp) = params
    n_alias = len(prev_outs)
    const = lambda shape: pl.BlockSpec(shape, lambda i: (0,) * len(shape))
    in_specs = [
        pl.BlockSpec((RT, D_IN), lambda i: (i, 0)),
        const((RT, LANES)), const((RT, LANES)),
        _layer_spec((SCORE_ROWS, LANES), layer, 1),
        pl.BlockSpec((None, BT, D_KV, WINDOW), lambda i: (layer, i, 0, 0)),
        pl.BlockSpec((None, BT, D_KV, WINDOW), lambda i: (layer, i, 0, 0)),
        pl.BlockSpec((None, POOL_HIST, BT, D_POOL), lambda i: (layer, 0, i, 0)),
        _layer_spec((len(POOL_WINDOWS), LANES, LANES), layer, 1),
        _layer_spec((1, D_POOL), layer, 1),
        _layer_spec((1, D_SGU), layer, 1),
        _layer_spec((1, D_SGU), layer, 1),
        _layer_spec((DEC_SEQ * DEC_SEQ, D_SGU), layer, 1),
        _layer_spec((DEC_SEQ, D_SGU), layer, 1),
    ] + [pl.BlockSpec(memory_space=pl.ANY)] * n_alias
    out_specs = [
        pl.BlockSpec((RT, D_MODEL), lambda i: (i, 0)),
        pl.BlockSpec((None, BT, D_KV, WINDOW), lambda i: (layer, i, 0, 0)),
        pl.BlockSpec((None, BT, D_KV, WINDOW), lambda i: (layer, i, 0, 0)),
        pl.BlockSpec((None, POOL_HIST, BT, D_POOL), lambda i: (layer, 0, i, 0)),
        pl.BlockSpec((None, RT, D_SGU), lambda i: (layer, i, 0)),
    ]
    out_shape = [
        jax.ShapeDtypeStruct((SAMPLE_ROWS, D_MODEL), F32),
        jax.ShapeDtypeStruct((DEPTH, DEC_BATCH, D_KV, WINDOW), F32),
        jax.ShapeDtypeStruct((DEPTH, DEC_BATCH, D_KV, WINDOW), F32),
        jax.ShapeDtypeStruct((DEPTH, POOL_HIST, DEC_BATCH, D_POOL), F32),
        jax.ShapeDtypeStruct((DEPTH, SAMPLE_ROWS, D_SGU), F32),
    ]
    scratch = [
        pltpu.VMEM((N_HEADS, RT, D_KV), BF16),
        pltpu.VMEM((RT, D_KV), F32),
        pltpu.VMEM((RT, D_KV), F32),
        pltpu.VMEM((D_KV, RT), F32),
        pltpu.VMEM((D_KV, RT), F32),
        pltpu.VMEM((N_REST_BLOCKS, RT, LANES), F32),
        pltpu.VMEM(((D_POOL + D_SGU) // LANES, RT, LANES), F32),
        pltpu.VMEM((D_SGU // LANES, RT, LANES), F32),
    ]
    n_in = len(in_specs) - n_alias
    return pl.pallas_call(
        functools.partial(_sample_mix_body, n_alias),
        grid=(DEC_BATCH // BT,),
        in_specs=in_specs,
        out_specs=out_specs,
        out_shape=out_shape,
        scratch_shapes=scratch,
        input_output_aliases={n_in + i: 1 + i for i in range(n_alias)},
        compiler_params=pltpu.CompilerParams(
            dimension_semantics=("arbitrary",), vmem_limit_bytes=VMEM_LIMIT_SAMPLE),
        name="sample_mix",
    )(z, cos_t, sin_t, sink_rows, ck, cv, state2, pw_b, ps, lng, lnb, wexp, bexp, *prev_outs)


def _sample_out_body(x_ref, mix_ref, w_out_ref, npost_ref, y_ref):
    y = _dot(mix_ref[...].astype(BF16), w_out_ref[...])
    out = x_ref[...].reshape(SAMPLE_ROWS, D_MODEL) + _rms(y, npost_ref[...])
    y_ref[...] = out.reshape(y_ref.shape)


def _sample_out(layer, xs, mix, w_out_b, npost, out_shape):
    full = lambda shape: pl.BlockSpec(shape, lambda i: (0,) * len(shape))
    return pl.pallas_call(
        _sample_out_body,
        grid=(1,),
        in_specs=[
            full(xs.shape),
            full((SAMPLE_ROWS, D_MODEL)),
            pl.BlockSpec((None, D_MODEL, D_MODEL), lambda i: (layer, 0, 0)),
            pl.BlockSpec((None, 1, D_MODEL), lambda i: (layer, 0, 0)),
        ],
        out_specs=full(out_shape),
        out_shape=jax.ShapeDtypeStruct(out_shape, F32),
        compiler_params=pltpu.CompilerParams(
            dimension_semantics=("arbitrary",), vmem_limit_bytes=VMEM_LIMIT_SAMPLE),
        name="sample_out",
    )(xs, mix, w_out_b, npost)


def _rope_tables(pos):
    inv = np.float32(ROPE_THETA) ** (-np.arange(0, HEAD_DIM, 2, dtype=np.float32) / np.float32(HEAD_DIM))
    ang = pos.astype(np.float32)[:, None] * inv[None, :].astype(np.float32)
    c, s = np.cos(ang, dtype=np.float32), np.sin(ang, dtype=np.float32)
    return np.concatenate([c, c, c, c], axis=1), np.concatenate([-s, s, -s, s], axis=1)


def kernel(x_prompt, x_sample, cache_k, cache_v, state_pool, w_in, w_out, norm_pre, norm_post,
           attn_sinks, pool_w, pool_scale, sgu_ln_g, sgu_ln_b, sgu_w, sgu_b):
    cos_p, sin_p = (jnp.asarray(t) for t in _rope_tables(np.arange(SEQ)))
    cos_s, sin_s = _rope_tables(PAST_LEN + np.arange(DEC_SEQ))
    cos_t = jnp.asarray(np.tile(cos_s, (BT, 1)))
    sin_t = jnp.asarray(np.tile(sin_s, (BT, 1)))

    w_in_b = w_in.astype(BF16)
    w_out_b = w_out.astype(BF16)
    pw_b = pool_w.astype(BF16)
    npre = norm_pre[:, None, :]
    npost = norm_post[:, None, :]
    ps = pool_scale[:, None, :]
    lng = sgu_ln_g[:, None, :]
    lnb = sgu_ln_b[:, None, :]
    sb_full = jnp.broadcast_to(sgu_b[:, :, :, None], (DEPTH, N_SGU_HEADS, CHUNK, LANES))
    wexp = jnp.repeat(sgu_w[:, :, :DEC_SEQ, :DEC_SEQ].transpose(0, 2, 3, 1), LANES, axis=-1
                      ).reshape(DEPTH, DEC_SEQ * DEC_SEQ, D_SGU)
    bexp = jnp.repeat(sgu_b[:, :, :DEC_SEQ].transpose(0, 2, 1), LANES, axis=-1)
    sink_rows = jnp.broadcast_to(attn_sinks[:, :, None, None], (DEPTH, N_HEADS, GR, LANES)
                                 ).reshape(DEPTH, SCORE_ROWS, LANES)

    xp = x_prompt
    xs = x_sample
    to_key_minor = lambda c: c.transpose(0, 1, 3, 4, 2).reshape(DEPTH, DEC_BATCH, D_KV, WINDOW)
    from_key_minor = lambda c: c.reshape(DEPTH, DEC_BATCH, N_KV, HEAD_DIM, WINDOW).transpose(0, 1, 4, 2, 3)
    ck = to_key_minor(cache_k)
    cv = to_key_minor(cache_v)
    state2 = state_pool.transpose(0, 2, 1, 3)
    prompt_params = (attn_sinks, w_in_b, w_out_b, npre, npost, pw_b, ps, lng, lnb, sgu_w, sb_full)
    sample_params = (sink_rows, pw_b, ps, lng, lnb, wexp, bexp)
    p_outs, s_outs = [], []
    for layer in range(DEPTH):
        xp, *p_outs = _prompt_layer(layer, xp, cos_p, sin_p, prompt_params, p_outs)
        z = _sample_proj(layer, xs, npre, w_in_b)
        mix, *s_outs = _sample_mix(layer, z, cos_t, sin_t, sample_params, ck, cv, state2, s_outs)
        xs_shape = x_sample.shape if layer == DEPTH - 1 else (SAMPLE_ROWS, D_MODEL)
        xs = _sample_out(layer, xs, mix, w_out_b, npost, xs_shape)

    nk_p, nv_p, np_p = p_outs
    nk_s, nv_s, np_s, vn_s = s_outs
    prompt_kv = lambda c: c.reshape(DEPTH, BATCH, N_KV, HEAD_DIM, WINDOW).transpose(0, 1, 4, 2, 3)
    return (xp, xs,
            prompt_kv(nk_p), prompt_kv(nv_p), np_p,
            from_key_minor(nk_s), from_key_minor(nv_s),
            np_s.transpose(0, 2, 1, 3),
            vn_s.reshape(DEPTH, DEC_BATCH, DEC_SEQ, D_SGU))
```

```python
import functools

import jax
import jax.numpy as jnp
import numpy as np
from jax import lax
from jax.experimental import pallas as pl
from jax.experimental.pallas import tpu as pltpu

D_MODEL = 2048
SEQ = 2048
BATCH = 8
DEPTH = 2
DEC_BATCH = 128
DEC_SEQ = 4
PAST_LEN = 8192
HEAD_DIM = 64
HALF_DIM = HEAD_DIM // 2
N_HEADS = 16
N_KV = 4
GROUP = N_HEADS // N_KV
D_ATT = N_HEADS * HEAD_DIM
D_KV = N_KV * HEAD_DIM
WINDOW = 128
ROPE_THETA = 10000.0
D_POOL = 512
POOL_WINDOWS = (2, 4, 8, 16)
POOL_HIST = 15
D_SGU = 512
CHUNK = 128
N_SGU_HEADS = 4
D_IN = 5120
EPS = 1e-6
ATT_SCALE = HEAD_DIM ** -0.5
LOG2_E = 1.4426950408889634

C_Q = 0
C_K = C_Q + D_ATT
C_V = C_K + D_KV
C_GA = C_V + D_KV
C_XB = C_GA + D_ATT
C_GB = C_XB + D_POOL
C_U = C_GB + D_POOL
C_VS = C_U + D_SGU
C_GC = C_VS + D_SGU
assert C_GC + D_SGU == D_IN

LANES = 128
SUBLANES = 8
HIST_PAD = 16
SUB = 256
N_SUB = 2
TM = SUB * N_SUB
N_CHUNKS = SUB // CHUNK
SAMPLE_ROWS = DEC_BATCH * DEC_SEQ
BT = 32
RT = BT * DEC_SEQ
assert RT == WINDOW
GB = 4
GR = GB * DEC_SEQ
ATT_PIECE = 512
REST_PIECE = 256
N_REST_PIECES = (D_IN - C_XB) // REST_PIECE
Z_U = D_POOL
Z_VS = Z_U + D_SGU
Z_WIDTH = Z_VS + D_SGU
S1_TN = 640
VMEM_LIMIT_PROMPT = 62 * 1024 * 1024
VMEM_LIMIT_SAMPLE = 56 * 1024 * 1024

F32 = jnp.float32
BF16 = jnp.bfloat16
NEG_INF = float("-inf")


def _rms(x, g):
    return x * lax.rsqrt(jnp.mean(x * x, axis=-1, keepdims=True) + EPS) * g


def _silu(x):
    return x * (1.0 / (1.0 + jnp.exp(-x)))


def _layernorm(x, g, b):
    mu = jnp.mean(x, axis=-1, keepdims=True)
    xc = x - mu
    var = jnp.mean(xc * xc, axis=-1, keepdims=True)
    return xc * lax.rsqrt(var + EPS) * g + b


def _swap_heads(x):
    return pltpu.roll(x, HEAD_DIM, 1)


def _rope(x, cos, sin):
    rows = x.shape[0]
    lane = lax.broadcasted_iota(jnp.int32, (rows, LANES), 1)
    first_half = (lane & HALF_DIM) == 0
    outs = []
    for c in range(x.shape[1] // LANES):
        xc = x[:, c * LANES:(c + 1) * LANES]
        partner = jnp.where(first_half,
                            pltpu.roll(xc, LANES - HALF_DIM, 1),
                            pltpu.roll(xc, HALF_DIM, 1))
        outs.append(xc * cos + partner * sin)
    return jnp.concatenate(outs, axis=1)


def _dot(a, b):
    return jnp.dot(a, b, preferred_element_type=F32)


def _dot_nt(a, b):
    return lax.dot_general(a, b, (((1,), (1,)), ((), ())), preferred_element_type=F32)


def _prompt_body(layer, n_alias, sinks_ref, x_ref, cos_ref, sin_ref, w_in_ref, w_out_ref,
                 npre_ref, npost_ref, pw_ref, ps_ref, lng_ref, lnb_ref, sw_ref, sb_ref, *rest):
    refs = rest[n_alias:]
    kbuf, vbuf, xbext = refs[5], refs[6], refs[7]
    j = pl.program_id(1)

    @pl.when(j == 0)
    def _start_of_sequence():
        kbuf[:, 0:WINDOW, :] = jnp.zeros((2 * N_KV, WINDOW, LANES), BF16)
        vbuf[:, 0:WINDOW, :] = jnp.zeros((N_KV, WINDOW, D_KV), BF16)
        xbext[0:HIST_PAD, :] = jnp.zeros((HIST_PAD, D_POOL), F32)

    @pl.when(j > 0)
    def _carry_from_previous_step():
        for i in range(2 * N_KV):
            kbuf[i, 0:WINDOW, :] = kbuf[i, TM:TM + WINDOW, :]
        for g in range(N_KV):
            vbuf[g, 0:WINDOW, :] = vbuf[g, TM:TM + WINDOW, :]
        xbext[0:HIST_PAD, :] = xbext[TM:TM + HIST_PAD, :]

    for sub in range(N_SUB):
        _prompt_sub_block(layer, sub, j, sinks_ref, x_ref, cos_ref, sin_ref, w_in_ref, w_out_ref,
                          npre_ref, npost_ref, pw_ref, ps_ref, lng_ref, lnb_ref, sw_ref, sb_ref, *refs)


def _prompt_sub_block(layer, sub, j, sinks_ref, x_ref, cos_ref, sin_ref, w_in_ref, w_out_ref,
                      npre_ref, npost_ref, pw_ref, ps_ref, lng_ref, lnb_ref, sw_ref, sb_ref,
                      y_ref, nk_ref, nv_ref, np_ref, qbuf, kbuf, vbuf, xbext, mixbuf, zrest, ybuf):
    base = sub * SUB
    sub_rows = slice(base, base + SUB)
    first_in_sequence = (j == 0) if sub == 0 else None
    h = _rms(x_ref[0, sub_rows], npre_ref[...]).astype(BF16)

    def proj(lo, width):
        return _dot(h, w_in_ref[:, lo:lo + width])

    cos = cos_ref[sub_rows, :]
    sin = sin_ref[sub_rows, :]

    k = _rope(proj(C_K, D_KV), cos, sin)
    v = proj(C_V, D_KV)
    cos_q = cos * (ATT_SCALE * LOG2_E)
    sin_q = sin * (ATT_SCALE * LOG2_E)
    for part in range(D_ATT // ATT_PIECE):
        cols = slice(part * ATT_PIECE, (part + 1) * ATT_PIECE)
        qbuf[:, cols] = _rope(proj(C_Q + part * ATT_PIECE, ATT_PIECE), cos_q, sin_q).astype(BF16)
    for part in range(D_ATT // ATT_PIECE):
        cols = slice(part * ATT_PIECE, (part + 1) * ATT_PIECE)
        mixbuf[:, cols] = _silu(proj(C_GA + part * ATT_PIECE, ATT_PIECE)).astype(BF16)
    if sub == N_SUB - 1:
        nk_ref[0] = k[SUB - WINDOW:SUB].T
        nv_ref[0] = v[SUB - WINDOW:SUB].T
    lane_k = lax.broadcasted_iota(jnp.int32, (SUB, LANES), 1)
    low = lane_k < HEAD_DIM
    lane_v = lax.broadcasted_iota(jnp.int32, (SUB, D_KV), 1)
    def store_by_chunk(buf, idx, val):
        val = val.astype(BF16)
        for c in range(N_CHUNKS):
            r0 = WINDOW + base + c * CHUNK
            buf[idx, r0:r0 + CHUNK, :] = val[c * CHUNK:(c + 1) * CHUNK]

    for pair in range(2):
        kp = k[:, pair * LANES:(pair + 1) * LANES]
        ks = _swap_heads(kp)
        store_by_chunk(kbuf, 4 * pair + 0, jnp.where(low, kp, 0.0))
        store_by_chunk(kbuf, 4 * pair + 1, jnp.where(low, 0.0, ks))
        store_by_chunk(kbuf, 4 * pair + 2, jnp.where(low, ks, 0.0))
        store_by_chunk(kbuf, 4 * pair + 3, jnp.where(low, 0.0, kp))
    for g in range(N_KV):
        keep_v = (lane_v >= g * HEAD_DIM) & (lane_v < (g + 1) * HEAD_DIM)
        store_by_chunk(vbuf, g, jnp.where(keep_v, v, 0.0))

    qi = lax.broadcasted_iota(jnp.int32, (CHUNK, 2 * WINDOW), 0)
    kj = lax.broadcasted_iota(jnp.int32, (CHUNK, 2 * WINDOW), 1)
    dist = qi + WINDOW - kj
    band_bias = jnp.where((dist >= 0) & (dist < WINDOW), 0.0, NEG_INF)
    if first_in_sequence is None:
        first_bias = band_bias
    else:
        no_prev = jnp.where(first_in_sequence, WINDOW, 0)
        first_bias = jnp.where(kj < no_prev, NEG_INF, band_bias)
    low_c = lax.broadcasted_iota(jnp.int32, (CHUNK, LANES), 1) < HEAD_DIM
    rest_piece = 0

    def project_rest_piece(piece):
        cols = slice(piece * REST_PIECE, (piece + 1) * REST_PIECE)
        lo = C_XB + piece * REST_PIECE
        val = proj(lo, REST_PIECE)
        if C_GB <= lo < C_U:
            mixbuf[:, D_ATT + lo - C_GB:D_ATT + lo - C_GB + REST_PIECE] = _silu(val).astype(BF16)
        elif lo >= C_GC:
            at = D_ATT + D_POOL + lo - C_GC
            mixbuf[:, at:at + REST_PIECE] = _silu(val).astype(BF16)
        else:
            at = lo - C_XB if lo < C_GB else (Z_U + lo - C_U if lo < C_VS else Z_VS + lo - C_VS)
            zrest[:, at:at + REST_PIECE] = val

    def scores(c, g):
        r0 = c * CHUNK
        qs = jnp.concatenate(
            [qbuf[r0:r0 + CHUNK, (2 * g + i) * LANES:(2 * g + i + 1) * LANES] for i in range(2)],
            axis=0)
        return [_dot_nt(qs, kbuf[2 * g + half, base + r0:base + r0 + 2 * WINDOW, :]) for half in range(2)]

    order = [(c, g) for c in range(N_CHUNKS) for g in range(N_KV)]
    while rest_piece < N_REST_PIECES - len(order):
        project_rest_piece(rest_piece)
        rest_piece += 1
    s_next = scores(*order[0])
    for step, (c, g) in enumerate(order):
        bias = first_bias if c == 0 else band_bias
        r0 = c * CHUNK
        s_half = s_next
        if step + 1 < len(order):
            s_next = scores(*order[step + 1])
        blocks = []
        for r in range(GROUP):
            sb = s_half[r % 2][(r // 2) * CHUNK:(r // 2 + 1) * CHUNK] + bias
            sink = sinks_ref[layer, GROUP * g + r] * LOG2_E
            m = jnp.maximum(jnp.max(sb, axis=1, keepdims=True), sink)
            pe = jnp.exp2(sb - m)
            den = jnp.sum(pe, axis=1, keepdims=True) + jnp.exp2(sink - m)
            blocks.append((pe * (1.0 / den)).astype(BF16))
        pg = jnp.concatenate(blocks, axis=0)
        og = _dot(pg, vbuf[g, base + r0:base + r0 + 2 * WINDOW, :])
        o = og if g == 0 else o + og
        if rest_piece < N_REST_PIECES:
            project_rest_piece(rest_piece)
            rest_piece += 1
        if g < N_KV - 1:
            continue
        for pair in range(2):
            for i in range(2):
                a = o[(2 * i) * CHUNK:(2 * i + 1) * CHUNK, pair * LANES:(pair + 1) * LANES]
                b = o[(2 * i + 1) * CHUNK:(2 * i + 2) * CHUNK, pair * LANES:(pair + 1) * LANES]
                for odd, blk in ((0, jnp.where(low_c, a, _swap_heads(b))),
                                 (1, jnp.where(low_c, _swap_heads(a), b))):
                    cols = slice((2 * (2 * pair + odd) + i) * LANES, (2 * (2 * pair + odd) + i + 1) * LANES)
                    mixbuf[r0:r0 + CHUNK, cols] = (blk * mixbuf[r0:r0 + CHUNK, cols]).astype(BF16)

    while rest_piece < N_REST_PIECES:
        project_rest_piece(rest_piece)
        rest_piece += 1

    xb0 = HIST_PAD + base
    xbext[xb0:xb0 + SUB, :] = zrest[:, 0:D_POOL]
    if sub == N_SUB - 1:
        np_ref[0] = xbext[HIST_PAD + TM - POOL_HIST:HIST_PAD + TM, :]
    pos1 = j * TM + base + lax.broadcasted_iota(jnp.int32, (SUB, LANES), 0) + 1
    for gi, w in enumerate(POOL_WINDOWS):
        cols = slice(gi * LANES, (gi + 1) * LANES)
        cur = xbext[xb0:xb0 + SUB, cols]
        acc = cur
        for i in range(1, w):
            acc = acc + xbext[xb0 - i:xb0 - i + SUB, cols]
        cnt = jnp.minimum(w, pos1).astype(F32)
        pooled = acc / cnt - cur
        mixed = _dot(pooled.astype(BF16), pw_ref[gi]) * ps_ref[:, cols]
        out_cols = slice(D_ATT + gi * LANES, D_ATT + (gi + 1) * LANES)
        mixbuf[:, out_cols] = (mixed * mixbuf[:, out_cols]).astype(BF16)

    vn = _layernorm(zrest[:, Z_VS:Z_VS + D_SGU], lng_ref[...], lnb_ref[...]).astype(BF16)
    ti = lax.broadcasted_iota(jnp.int32, (CHUNK, CHUNK), 0)
    si = lax.broadcasted_iota(jnp.int32, (CHUNK, CHUNK), 1)
    for hh in range(N_SGU_HEADS):
        cols = slice(hh * LANES, (hh + 1) * LANES)
        wm = jnp.where(ti >= si, sw_ref[hh], 0.0).astype(BF16)
        for c in range(N_CHUNKS):
            rows = slice(c * CHUNK, (c + 1) * CHUNK)
            mixed = _dot(wm, vn[rows, cols]) + sb_ref[hh]
            out_cols = slice(D_ATT + D_POOL + hh * LANES, D_ATT + D_POOL + (hh + 1) * LANES)
            mixbuf[rows, out_cols] = (
                zrest[rows, Z_U + hh * LANES:Z_U + (hh + 1) * LANES] * mixed * mixbuf[rows, out_cols]
            ).astype(BF16)

    slot = jnp.minimum(j, 0)
    y = _dot(mixbuf[...], w_out_ref[...])
    ybuf[slot] = y
    if sub < N_SUB - 1:
        y = ybuf[slot]
    y_ref[0, sub_rows] = x_ref[0, sub_rows] + _rms(y, npost_ref[...])


def _layer_spec(shape, layer, grid_rank):
    zeros = (0,) * len(shape)
    if grid_rank == 2:
        index_map = lambda b, j: (layer,) + zeros
    else:
        index_map = lambda i: (layer,) + zeros
    return pl.BlockSpec((None,) + tuple(shape), index_map, pipeline_mode=pl.Buffered(1))


def _prompt_layer(layer, x, cos, sin, params, prev_outs):
    (sinks, w_in_b, w_out_b, npre, npost, pw_b, ps, lng, lnb, sw, sb_full) = params
    grid = (BATCH, SEQ // TM)
    n_alias = len(prev_outs)
    in_specs = [
        pl.BlockSpec(memory_space=pltpu.SMEM),
        pl.BlockSpec((1, TM, D_MODEL), lambda b, j: (b, j, 0)),
        pl.BlockSpec((TM, LANES), lambda b, j: (j, 0)),
        pl.BlockSpec((TM, LANES), lambda b, j: (j, 0)),
        _layer_spec((D_MODEL, D_IN), 0, 2),
        _layer_spec((D_MODEL, D_MODEL), 0, 2),
        _layer_spec((1, D_MODEL), layer, 2),
        _layer_spec((1, D_MODEL), layer, 2),
        _layer_spec((len(POOL_WINDOWS), LANES, LANES), layer, 2),
        _layer_spec((1, D_POOL), layer, 2),
        _layer_spec((1, D_SGU), layer, 2),
        _layer_spec((1, D_SGU), layer, 2),
        _layer_spec((N_SGU_HEADS, CHUNK, CHUNK), layer, 2),
        _layer_spec((N_SGU_HEADS, CHUNK, LANES), layer, 2),
    ] + [pl.BlockSpec(memory_space=pl.ANY)] * n_alias
    out_specs = [
        pl.BlockSpec((1, TM, D_MODEL), lambda b, j: (b, j, 0)),
        pl.BlockSpec((None, 1, D_KV, WINDOW), lambda b, j: (layer, b, 0, 0)),
        pl.BlockSpec((None, 1, D_KV, WINDOW), lambda b, j: (layer, b, 0, 0)),
        pl.BlockSpec((None, 1, POOL_HIST, D_POOL), lambda b, j: (layer, b, 0, 0)),
    ]
    out_shape = [
        jax.ShapeDtypeStruct((BATCH, SEQ, D_MODEL), F32),
        jax.ShapeDtypeStruct((DEPTH, BATCH, D_KV, WINDOW), F32),
        jax.ShapeDtypeStruct((DEPTH, BATCH, D_KV, WINDOW), F32),
        jax.ShapeDtypeStruct((DEPTH, BATCH, POOL_HIST, D_POOL), F32),
    ]
    scratch = [
        pltpu.VMEM((SUB, D_ATT), BF16),
        pltpu.VMEM((2 * N_KV, WINDOW + TM, LANES), BF16),
        pltpu.VMEM((N_KV, WINDOW + TM, D_KV), BF16),
        pltpu.VMEM((HIST_PAD + TM, D_POOL), F32),
        pltpu.VMEM((SUB, D_MODEL), BF16),
        pltpu.VMEM((SUB, Z_WIDTH), F32),
        pltpu.VMEM((1, SUB, D_MODEL), F32),
    ]
    n_in = len(in_specs) - n_alias
    return pl.pallas_call(
        functools.partial(_prompt_body, layer, n_alias),
        grid=grid,
        in_specs=in_specs,
        out_specs=out_specs,
        out_shape=out_shape,
        scratch_shapes=scratch,
        input_output_aliases={n_in + i: 1 + i for i in range(n_alias)},
        compiler_params=pltpu.CompilerParams(
            dimension_semantics=("arbitrary", "arbitrary"),
            vmem_limit_bytes=VMEM_LIMIT_PROMPT),
        name="prompt_layer",
    )(sinks, x, cos, sin, w_in_b, w_out_b, npre, npost, pw_b, ps, lng, lnb, sw, sb_full, *prev_outs)


def _sample_proj_body(x_ref, npre_ref, w_in_ref, w_out_ref, z_ref, w_in_b_ref, w_out_b_ref, hbuf):
    @pl.when(pl.program_id(0) == 0)
    def _norm_once():
        hbuf[...] = _rms(x_ref[...].reshape(SAMPLE_ROWS, D_MODEL), npre_ref[...]).astype(BF16)

    w_tile = w_in_ref[...].astype(BF16)
    w_in_b_ref[0] = w_tile
    w_out_b_ref[0] = w_out_ref[...].astype(BF16)
    z_ref[...] = _dot(hbuf[...], w_tile)


def _sample_proj(layer, xs, npre, w_in, w_out):
    steps = D_IN // S1_TN
    rows_out = D_MODEL // steps
    return pl.pallas_call(
        _sample_proj_body,
        grid=(steps,),
        in_specs=[
            pl.BlockSpec(xs.shape, lambda n: (0,) * xs.ndim),
            pl.BlockSpec((None, 1, D_MODEL), lambda n: (layer, 0, 0)),
            pl.BlockSpec((None, D_MODEL, S1_TN), lambda n: (layer, 0, n)),
            pl.BlockSpec((None, rows_out, D_MODEL), lambda n: (layer, n, 0)),
        ],
        out_specs=[
            pl.BlockSpec((SAMPLE_ROWS, S1_TN), lambda n: (0, n)),
            pl.BlockSpec((1, D_MODEL, S1_TN), lambda n: (0, 0, n)),
            pl.BlockSpec((1, rows_out, D_MODEL), lambda n: (0, n, 0)),
        ],
        out_shape=[
            jax.ShapeDtypeStruct((SAMPLE_ROWS, D_IN), F32),
            jax.ShapeDtypeStruct((1, D_MODEL, D_IN), BF16),
            jax.ShapeDtypeStruct((1, D_MODEL, D_MODEL), BF16),
        ],
        scratch_shapes=[pltpu.VMEM((SAMPLE_ROWS, D_MODEL), BF16)],
        compiler_params=pltpu.CompilerParams(
            dimension_semantics=("arbitrary",), vmem_limit_bytes=VMEM_LIMIT_SAMPLE),
        name="sample_proj",
    )(xs, npre, w_in, w_out)


SCORE_ROWS = N_HEADS * GR
N_REST_BLOCKS = (D_IN - C_XB) // LANES


def _sample_mix_body(n_alias, z_ref, cos_ref, sin_ref, sink_ref, ck_ref, cv_ref, st_ref,
                     pw_ref, ps_ref, lng_ref, lnb_ref, wexp_ref, bexp_ref, *rest):
    (mix_ref, nk_ref, nv_ref, npool_ref, vn_ref,
     qbig, knew_f, vnew_f, knew_t, vnew_t, zc, mo, vo) = rest[n_alias:]
    cos = cos_ref[...]
    sin = sin_ref[...]

    q = _rope(z_ref[:, C_Q:C_Q + D_ATT], cos * ATT_SCALE, sin * ATT_SCALE)
    k_new = _rope(z_ref[:, C_K:C_K + D_KV], cos, sin)
    v_new = z_ref[:, C_V:C_V + D_KV]
    knew_f[...] = k_new
    vnew_f[...] = v_new
    knew_t[...] = k_new.T
    vnew_t[...] = v_new.T
    low = lax.broadcasted_iota(jnp.int32, (RT, LANES), 1) < HEAD_DIM
    zero_blk = jnp.zeros((RT, LANES), F32)
    for h in range(N_HEADS):
        g = h // GROUP
        src = q[:, (h // 2) * LANES:(h // 2 + 1) * LANES]
        if h % 2 != g % 2:
            src = _swap_heads(src)
        piece = jnp.where(low, src, 0.0) if g % 2 == 0 else jnp.where(low, 0.0, src)
        full = [zero_blk, zero_blk]
        full[g // 2] = piece
        qbig[h] = jnp.concatenate(full, axis=1).astype(BF16)

    row_h = lax.broadcasted_iota(jnp.int32, (SCORE_ROWS, GB * WINDOW), 0)
    col_h = lax.broadcasted_iota(jnp.int32, (SCORE_ROWS, GB * WINDOW), 1)
    same_h = ((row_h >> 2) & (GB - 1)) == (col_h >> 7)
    bias_h = jnp.where(same_h, jnp.where((col_h & (WINDOW - 1)) > (row_h & (DEC_SEQ - 1)), 0.0, NEG_INF),
                       NEG_INF)
    row_n = lax.broadcasted_iota(jnp.int32, (SCORE_ROWS, GR), 0)
    col_n = lax.broadcasted_iota(jnp.int32, (SCORE_ROWS, GR), 1)
    same_n = ((row_n >> 2) & (GB - 1)) == (col_n >> 2)
    bias_n = jnp.where(same_n, jnp.where((col_n & (DEC_SEQ - 1)) <= (row_n & (DEC_SEQ - 1)), 0.0, NEG_INF),
                       NEG_INF)
    sink = sink_ref[:, 0:1]
    low_g = lax.broadcasted_iota(jnp.int32, (GR, LANES), 1) < HEAD_DIM
    key_lane = lax.broadcasted_iota(jnp.int32, (D_KV, WINDOW), 1)
    src_key = lax.broadcasted_iota(jnp.int32, (WINDOW, WINDOW), 0)
    dst_key = lax.broadcasted_iota(jnp.int32, (WINDOW, WINDOW), 1)
    shift_keys = jnp.where(src_key == dst_key + DEC_SEQ, 1.0, 0.0).astype(BF16)

    def roll_in(hist_t, new_t, bi):
        hi = hist_t.astype(BF16)
        rest1 = hist_t - hi.astype(F32)
        mid = rest1.astype(BF16)
        lo = (rest1 - mid.astype(F32)).astype(BF16)
        rolled = _dot(hi, shift_keys) + _dot(mid, shift_keys) + _dot(lo, shift_keys)
        placed = pltpu.roll(new_t, (WINDOW - DEC_SEQ - DEC_SEQ * bi) % RT, 1)
        return jnp.where(key_lane >= WINDOW - DEC_SEQ, placed, rolled)

    for grp in range(BT // GB):
        r0 = grp * GR
        lhs = qbig[:, r0:r0 + GR, :].reshape(SCORE_ROWS, D_KV)
        kh_t = jnp.concatenate([ck_ref[grp * GB + bb] for bb in range(GB)], axis=1)
        vh_t = jnp.concatenate([cv_ref[grp * GB + bb] for bb in range(GB)], axis=1)
        kn = knew_f[r0:r0 + GR, :]
        vn_new = vnew_f[r0:r0 + GR, :]
        s_h = _dot(lhs, kh_t.astype(BF16)) + bias_h
        s_n = _dot_nt(lhs, kn.astype(BF16)) + bias_n
        m = jnp.maximum(jnp.maximum(jnp.max(s_h, axis=1, keepdims=True),
                                    jnp.max(s_n, axis=1, keepdims=True)), sink)
        p_h = jnp.exp(s_h - m)
        p_n = jnp.exp(s_n - m)
        den = (jnp.sum(p_h, axis=1, keepdims=True) + jnp.sum(p_n, axis=1, keepdims=True)
               + jnp.exp(sink - m))
        inv = 1.0 / den
        o = (_dot_nt((p_h * inv).astype(BF16), vh_t.astype(BF16))
             + _dot((p_n * inv).astype(BF16), vn_new.astype(BF16)))
        for c in range(N_HEADS // 2):
            g = c // 2
            a = o[(2 * c) * GR:(2 * c + 1) * GR, (g // 2) * LANES:(g // 2 + 1) * LANES]
            b = o[(2 * c + 1) * GR:(2 * c + 2) * GR, (g // 2) * LANES:(g // 2 + 1) * LANES]
            blk = jnp.where(low_g, a, _swap_heads(b)) if g % 2 == 0 else jnp.where(low_g, _swap_heads(a), b)
            gate = _silu(z_ref[r0:r0 + GR, C_GA + c * LANES:C_GA + (c + 1) * LANES])
            mix_ref[r0:r0 + GR, c * LANES:(c + 1) * LANES] = blk * gate
        for bb in range(GB):
            bi = grp * GB + bb
            nk_ref[bi] = roll_in(ck_ref[bi], knew_t[...], bi)
            nv_ref[bi] = roll_in(cv_ref[bi], vnew_t[...], bi)

    for c in range(N_REST_BLOCKS):
        zc[c] = z_ref[:, C_XB + c * LANES:C_XB + (c + 1) * LANES]

    def tok(t, lo, width):
        blk0 = (lo - C_XB) // LANES
        parts = [zc[blk0 + c, pl.ds(t, BT, stride=DEC_SEQ), :] for c in range(width // LANES)]
        return parts[0] if len(parts) == 1 else jnp.concatenate(parts, axis=1)

    def ext(i, gi):
        if i < POOL_HIST:
            return st_ref[i, :, gi * LANES:(gi + 1) * LANES]
        return tok(i - POOL_HIST, C_XB + gi * LANES, LANES)

    for gi, w in enumerate(POOL_WINDOWS):
        for s in range(POOL_HIST):
            npool_ref[s, :, gi * LANES:(gi + 1) * LANES] = ext(s + DEC_SEQ, gi)
        pooled = []
        for t in range(DEC_SEQ):
            cur = ext(POOL_HIST + t, gi)
            acc = cur
            for i in range(1, w):
                acc = acc + ext(POOL_HIST + t - i, gi)
            cnt = float(min(w, PAST_LEN + t + 1))
            pooled.append(acc / cnt - cur)
        pooled = jnp.concatenate(pooled, axis=0).astype(BF16)
        mixed = _dot(pooled, pw_ref[gi]) * ps_ref[:, gi * LANES:(gi + 1) * LANES]
        for t in range(DEC_SEQ):
            gate = _silu(tok(t, C_GB + gi * LANES, LANES))
            mo[gi, pl.ds(t, BT, stride=DEC_SEQ), :] = mixed[t * BT:(t + 1) * BT] * gate

    vns = []
    for t in range(DEC_SEQ):
        vn_t = _layernorm(tok(t, C_VS, D_SGU), lng_ref[...], lnb_ref[...])
        for c in range(D_SGU // LANES):
            vo[c, pl.ds(t, BT, stride=DEC_SEQ), :] = vn_t[:, c * LANES:(c + 1) * LANES]
        vns.append(vn_t)
    for t in range(DEC_SEQ):
        mixed = bexp_ref[t:t + 1, :]
        for s in range(t + 1):
            mixed = mixed + wexp_ref[DEC_SEQ * t + s:DEC_SEQ * t + s + 1, :] * vns[s]
        out = tok(t, C_U, D_SGU) * mixed * _silu(tok(t, C_GC, D_SGU))
        for c in range(D_SGU // LANES):
            mo[D_POOL // LANES + c, pl.ds(t, BT, stride=DEC_SEQ), :] = out[:, c * LANES:(c + 1) * LANES]

    for c in range((D_POOL + D_SGU) // LANES):
        mix_ref[:, D_ATT + c * LANES:D_ATT + (c + 1) * LANES] = mo[c]
    for c in range(D_SGU // LANES):
        vn_ref[:, c * LANES:(c + 1) * LANES] = vo[c]


def _sample_mix(layer, z, cos_t, sin_t, params, ck, cv, state2, prev_outs):
    (sink_rows, pw_b, ps, lng, lnb, wexp, bexp) = params
    n_alias = len(prev_outs)
    const = lambda shape: pl.BlockSpec(shape, lambda i: (0,) * len(shape))
    in_specs = [
        pl.BlockSpec((RT, D_IN), lambda i: (i, 0)),
        const((RT, LANES)), const((RT, LANES)),
        _layer_spec((SCORE_ROWS, LANES), layer, 1),
        pl.BlockSpec((None, BT, D_KV, WINDOW), lambda i: (layer, i, 0, 0)),
        pl.BlockSpec((None, BT, D_KV, WINDOW), lambda i: (layer, i, 0, 0)),
        pl.BlockSpec((None, POOL_HIST, BT, D_POOL), lambda i: (layer, 0, i, 0)),
        _layer_spec((len(POOL_WINDOWS), LANES, LANES), layer, 1),
        _layer_spec((1, D_POOL), layer, 1),
        _layer_spec((1, D_SGU), layer, 1),
        _layer_spec((1, D_SGU), layer, 1),
        _layer_spec((DEC_SEQ * DEC_SEQ, D_SGU), layer, 1),
        _layer_spec((DEC_SEQ, D_SGU), layer, 1),
    ] + [pl.BlockSpec(memory_space=pl.ANY)] * n_alias
    out_specs = [
        pl.BlockSpec((RT, D_MODEL), lambda i: (i, 0)),
        pl.BlockSpec((None, BT, D_KV, WINDOW), lambda i: (layer, i, 0, 0)),
        pl.BlockSpec((None, BT, D_KV, WINDOW), lambda i: (layer, i, 0, 0)),
        pl.BlockSpec((None, POOL_HIST, BT, D_POOL), lambda i: (layer, 0, i, 0)),
        pl.BlockSpec((None, RT, D_SGU), lambda i: (layer, i, 0)),
    ]
    out_shape = [
        jax.ShapeDtypeStruct((SAMPLE_ROWS, D_MODEL), F32),
        jax.ShapeDtypeStruct((DEPTH, DEC_BATCH, D_KV, WINDOW), F32),
        jax.ShapeDtypeStruct((DEPTH, DEC_BATCH, D_KV, WINDOW), F32),
        jax.ShapeDtypeStruct((DEPTH, POOL_HIST, DEC_BATCH, D_POOL), F32),
        jax.ShapeDtypeStruct((DEPTH, SAMPLE_ROWS, D_SGU), F32),
    ]
    scratch = [
        pltpu.VMEM((N_HEADS, RT, D_KV), BF16),
        pltpu.VMEM((RT, D_KV), F32),
        pltpu.VMEM((RT, D_KV), F32),
        pltpu.VMEM((D_KV, RT), F32),
        pltpu.VMEM((D_KV, RT), F32),
        pltpu.VMEM((N_REST_BLOCKS, RT, LANES), F32),
        pltpu.VMEM(((D_POOL + D_SGU) // LANES, RT, LANES), F32),
        pltpu.VMEM((D_SGU // LANES, RT, LANES), F32),
    ]
    n_in = len(in_specs) - n_alias
    return pl.pallas_call(
        functools.partial(_sample_mix_body, n_alias),
        grid=(DEC_BATCH // BT,),
        in_specs=in_specs,
        out_specs=out_specs,
        out_shape=out_shape,
        scratch_shapes=scratch,
        input_output_aliases={n_in + i: 1 + i for i in range(n_alias)},
        compiler_params=pltpu.CompilerParams(
            dimension_semantics=("arbitrary",), vmem_limit_bytes=VMEM_LIMIT_SAMPLE),
        name="sample_mix",
    )(z, cos_t, sin_t, sink_rows, ck, cv, state2, pw_b, ps, lng, lnb, wexp, bexp, *prev_outs)


def _sample_out_body(x_ref, mix_ref, w_out_ref, npost_ref, y_ref):
    y = _dot(mix_ref[...].astype(BF16), w_out_ref[...])
    out = x_ref[...].reshape(SAMPLE_ROWS, D_MODEL) + _rms(y, npost_ref[...])
    y_ref[...] = out.reshape(y_ref.shape)


def _sample_out(layer, xs, mix, w_out_b, npost, out_shape):
    full = lambda shape: pl.BlockSpec(shape, lambda i: (0,) * len(shape))
    return pl.pallas_call(
        _sample_out_body,
        grid=(1,),
        in_specs=[
            full(xs.shape),
            full((SAMPLE_ROWS, D_MODEL)),
            pl.BlockSpec((None, D_MODEL, D_MODEL), lambda i: (0, 0, 0)),
            pl.BlockSpec((None, 1, D_MODEL), lambda i: (layer, 0, 0)),
        ],
        out_specs=full(out_shape),
        out_shape=jax.ShapeDtypeStruct(out_shape, F32),
        compiler_params=pltpu.CompilerParams(
            dimension_semantics=("arbitrary",), vmem_limit_bytes=VMEM_LIMIT_SAMPLE),
        name="sample_out",
    )(xs, mix, w_out_b, npost)


def _rope_tables(pos):
    inv = np.float32(ROPE_THETA) ** (-np.arange(0, HEAD_DIM, 2, dtype=np.float32) / np.float32(HEAD_DIM))
    ang = pos.astype(np.float32)[:, None] * inv[None, :].astype(np.float32)
    c, s = np.cos(ang, dtype=np.float32), np.sin(ang, dtype=np.float32)
    return np.concatenate([c, c, c, c], axis=1), np.concatenate([-s, s, -s, s], axis=1)


def kernel(x_prompt, x_sample, cache_k, cache_v, state_pool, w_in, w_out, norm_pre, norm_post,
           attn_sinks, pool_w, pool_scale, sgu_ln_g, sgu_ln_b, sgu_w, sgu_b):
    cos_p, sin_p = (jnp.asarray(t) for t in _rope_tables(np.arange(SEQ)))
    cos_s, sin_s = _rope_tables(PAST_LEN + np.arange(DEC_SEQ))
    cos_t = jnp.asarray(np.tile(cos_s, (BT, 1)))
    sin_t = jnp.asarray(np.tile(sin_s, (BT, 1)))

    pw_b = pool_w.astype(BF16)
    npre = norm_pre[:, None, :]
    npost = norm_post[:, None, :]
    ps = pool_scale[:, None, :]
    lng = sgu_ln_g[:, None, :]
    lnb = sgu_ln_b[:, None, :]
    sb_full = jnp.broadcast_to(sgu_b[:, :, :, None], (DEPTH, N_SGU_HEADS, CHUNK, LANES))
    wexp = jnp.repeat(sgu_w[:, :, :DEC_SEQ, :DEC_SEQ].transpose(0, 2, 3, 1), LANES, axis=-1
                      ).reshape(DEPTH, DEC_SEQ * DEC_SEQ, D_SGU)
    bexp = jnp.repeat(sgu_b[:, :, :DEC_SEQ].transpose(0, 2, 1), LANES, axis=-1)
    sink_rows = jnp.broadcast_to(attn_sinks[:, :, None, None], (DEPTH, N_HEADS, GR, LANES)
                                 ).reshape(DEPTH, SCORE_ROWS, LANES)

    xp = x_prompt
    xs = x_sample
    to_key_minor = lambda c: c.transpose(0, 1, 3, 4, 2).reshape(DEPTH, DEC_BATCH, D_KV, WINDOW)
    from_key_minor = lambda c: c.reshape(DEPTH, DEC_BATCH, N_KV, HEAD_DIM, WINDOW).transpose(0, 1, 4, 2, 3)
    ck = to_key_minor(cache_k)
    cv = to_key_minor(cache_v)
    state2 = state_pool.transpose(0, 2, 1, 3)
    sample_params = (sink_rows, pw_b, ps, lng, lnb, wexp, bexp)
    p_outs, s_outs = [], []
    for layer in range(DEPTH):
        z, w_in_b, w_out_b = _sample_proj(layer, xs, npre, w_in, w_out)
        prompt_params = (attn_sinks, w_in_b, w_out_b, npre, npost, pw_b, ps, lng, lnb, sgu_w, sb_full)
        xp, *p_outs = _prompt_layer(layer, xp, cos_p, sin_p, prompt_params, p_outs)
        mix, *s_outs = _sample_mix(layer, z, cos_t, sin_t, sample_params, ck, cv, state2, s_outs)
        xs_shape = x_sample.shape if layer == DEPTH - 1 else (SAMPLE_ROWS, D_MODEL)
        xs = _sample_out(layer, xs, mix, w_out_b, npost, xs_shape)

    nk_p, nv_p, np_p = p_outs
    nk_s, nv_s, np_s, vn_s = s_outs
    prompt_kv = lambda c: c.reshape(DEPTH, BATCH, N_KV, HEAD_DIM, WINDOW).transpose(0, 1, 4, 2, 3)
    return (xp, xs,
            prompt_kv(nk_p), prompt_kv(nv_p), np_p,
            from_key_minor(nk_s), from_key_minor(nv_s),
            np_s.transpose(0, 2, 1, 3),
            vn_s.reshape(DEPTH, DEC_BATCH, DEC_SEQ, D_SGU))
```

```python
import functools

import jax
import jax.numpy as jnp
import numpy as np
from jax import lax
from jax.experimental import pallas as pl
from jax.experimental.pallas import tpu as pltpu

D_MODEL = 2048
SEQ = 2048
BATCH = 8
DEPTH = 2
DEC_BATCH = 128
DEC_SEQ = 4
PAST_LEN = 8192
HEAD_DIM = 64
HALF_DIM = HEAD_DIM // 2
N_HEADS = 16
N_KV = 4
GROUP = N_HEADS // N_KV
D_ATT = N_HEADS * HEAD_DIM
D_KV = N_KV * HEAD_DIM
WINDOW = 128
ROPE_THETA = 10000.0
D_POOL = 512
POOL_WINDOWS = (2, 4, 8, 16)
POOL_HIST = 15
D_SGU = 512
CHUNK = 128
N_SGU_HEADS = 4
D_IN = 5120
EPS = 1e-6
ATT_SCALE = HEAD_DIM ** -0.5
LOG2_E = 1.4426950408889634

C_Q = 0
C_K = C_Q + D_ATT
C_V = C_K + D_KV
C_GA = C_V + D_KV
C_XB = C_GA + D_ATT
C_GB = C_XB + D_POOL
C_U = C_GB + D_POOL
C_VS = C_U + D_SGU
C_GC = C_VS + D_SGU
assert C_GC + D_SGU == D_IN

LANES = 128
SUBLANES = 8
HIST_PAD = 16
SUB = 256
N_SUB = 2
TM = SUB * N_SUB
N_CHUNKS = SUB // CHUNK
SAMPLE_ROWS = DEC_BATCH * DEC_SEQ
BT = 32
RT = BT * DEC_SEQ
assert RT == WINDOW
GB = 4
GR = GB * DEC_SEQ
ATT_PIECE = 512
REST_PIECE = 256
N_REST_PIECES = (D_IN - C_XB) // REST_PIECE
Z_U = D_POOL
Z_VS = Z_U + D_SGU
Z_WIDTH = Z_VS + D_SGU
S1_TN = 640
S3_STEPS = 4
VMEM_LIMIT_PROMPT = 62 * 1024 * 1024
VMEM_LIMIT_SAMPLE = 56 * 1024 * 1024

F32 = jnp.float32
BF16 = jnp.bfloat16
NEG_INF = float("-inf")


def _rms(x, g):
    return x * lax.rsqrt(jnp.mean(x * x, axis=-1, keepdims=True) + EPS) * g


def _silu(x):
    return x * (1.0 / (1.0 + jnp.exp(-x)))


def _layernorm(x, g, b):
    mu = jnp.mean(x, axis=-1, keepdims=True)
    xc = x - mu
    var = jnp.mean(xc * xc, axis=-1, keepdims=True)
    return xc * lax.rsqrt(var + EPS) * g + b


def _swap_heads(x):
    return pltpu.roll(x, HEAD_DIM, 1)


def _rope(x, cos, sin):
    rows = x.shape[0]
    lane = lax.broadcasted_iota(jnp.int32, (rows, LANES), 1)
    first_half = (lane & HALF_DIM) == 0
    outs = []
    for c in range(x.shape[1] // LANES):
        xc = x[:, c * LANES:(c + 1) * LANES]
        partner = jnp.where(first_half,
                            pltpu.roll(xc, LANES - HALF_DIM, 1),
                            pltpu.roll(xc, HALF_DIM, 1))
        outs.append(xc * cos + partner * sin)
    return jnp.concatenate(outs, axis=1)


def _dot(a, b):
    return jnp.dot(a, b, preferred_element_type=F32)


def _dot_nt(a, b):
    return lax.dot_general(a, b, (((1,), (1,)), ((), ())), preferred_element_type=F32)


def _prompt_body(layer, n_alias, sinks_ref, x_ref, cos_ref, sin_ref, w_in_ref, w_out_ref,
                 npre_ref, npost_ref, pw_ref, ps_ref, lng_ref, lnb_ref, sw_ref, sb_ref, *rest):
    refs = rest[n_alias:]
    kbuf, vbuf, xbext = refs[5], refs[6], refs[7]
    j = pl.program_id(1)

    @pl.when(j == 0)
    def _start_of_sequence():
        kbuf[:, 0:WINDOW, :] = jnp.zeros((2 * N_KV, WINDOW, LANES), BF16)
        vbuf[:, 0:WINDOW, :] = jnp.zeros((N_KV, WINDOW, D_KV), BF16)
        xbext[0:HIST_PAD, :] = jnp.zeros((HIST_PAD, D_POOL), F32)

    @pl.when(j > 0)
    def _carry_from_previous_step():
        for i in range(2 * N_KV):
            kbuf[i, 0:WINDOW, :] = kbuf[i, TM:TM + WINDOW, :]
        for g in range(N_KV):
            vbuf[g, 0:WINDOW, :] = vbuf[g, TM:TM + WINDOW, :]
        xbext[0:HIST_PAD, :] = xbext[TM:TM + HIST_PAD, :]

    for sub in range(N_SUB):
        _prompt_sub_block(layer, sub, j, sinks_ref, x_ref, cos_ref, sin_ref, w_in_ref, w_out_ref,
                          npre_ref, npost_ref, pw_ref, ps_ref, lng_ref, lnb_ref, sw_ref, sb_ref, *refs)


def _prompt_sub_block(layer, sub, j, sinks_ref, x_ref, cos_ref, sin_ref, w_in_ref, w_out_ref,
                      npre_ref, npost_ref, pw_ref, ps_ref, lng_ref, lnb_ref, sw_ref, sb_ref,
                      y_ref, nk_ref, nv_ref, np_ref, qbuf, kbuf, vbuf, xbext, mixbuf, zrest, ybuf):
    base = sub * SUB
    sub_rows = slice(base, base + SUB)
    first_in_sequence = (j == 0) if sub == 0 else None
    h = _rms(x_ref[0, sub_rows], npre_ref[...]).astype(BF16)

    def proj(lo, width):
        return _dot(h, w_in_ref[:, lo:lo + width])

    cos = cos_ref[sub_rows, :]
    sin = sin_ref[sub_rows, :]

    k = _rope(proj(C_K, D_KV), cos, sin)
    v = proj(C_V, D_KV)
    cos_q = cos * (ATT_SCALE * LOG2_E)
    sin_q = sin * (ATT_SCALE * LOG2_E)
    for part in range(D_ATT // ATT_PIECE):
        cols = slice(part * ATT_PIECE, (part + 1) * ATT_PIECE)
        qbuf[:, cols] = _rope(proj(C_Q + part * ATT_PIECE, ATT_PIECE), cos_q, sin_q).astype(BF16)
    for part in range(D_ATT // ATT_PIECE):
        cols = slice(part * ATT_PIECE, (part + 1) * ATT_PIECE)
        mixbuf[:, cols] = _silu(proj(C_GA + part * ATT_PIECE, ATT_PIECE)).astype(BF16)
    if sub == N_SUB - 1:
        nk_ref[0] = k[SUB - WINDOW:SUB].T
        nv_ref[0] = v[SUB - WINDOW:SUB].T
    lane_k = lax.broadcasted_iota(jnp.int32, (SUB, LANES), 1)
    low = lane_k < HEAD_DIM
    lane_v = lax.broadcasted_iota(jnp.int32, (SUB, D_KV), 1)
    def store_by_chunk(buf, idx, val):
        val = val.astype(BF16)
        for c in range(N_CHUNKS):
            r0 = WINDOW + base + c * CHUNK
            buf[idx, r0:r0 + CHUNK, :] = val[c * CHUNK:(c + 1) * CHUNK]

    for pair in range(2):
        kp = k[:, pair * LANES:(pair + 1) * LANES]
        ks = _swap_heads(kp)
        store_by_chunk(kbuf, 4 * pair + 0, jnp.where(low, kp, 0.0))
        store_by_chunk(kbuf, 4 * pair + 1, jnp.where(low, 0.0, ks))
        store_by_chunk(kbuf, 4 * pair + 2, jnp.where(low, ks, 0.0))
        store_by_chunk(kbuf, 4 * pair + 3, jnp.where(low, 0.0, kp))
    for g in range(N_KV):
        keep_v = (lane_v >= g * HEAD_DIM) & (lane_v < (g + 1) * HEAD_DIM)
        store_by_chunk(vbuf, g, jnp.where(keep_v, v, 0.0))

    qi = lax.broadcasted_iota(jnp.int32, (CHUNK, 2 * WINDOW), 0)
    kj = lax.broadcasted_iota(jnp.int32, (CHUNK, 2 * WINDOW), 1)
    dist = qi + WINDOW - kj
    band_bias = jnp.where((dist >= 0) & (dist < WINDOW), 0.0, NEG_INF)
    if first_in_sequence is None:
        first_bias = band_bias
    else:
        no_prev = jnp.where(first_in_sequence, WINDOW, 0)
        first_bias = jnp.where(kj < no_prev, NEG_INF, band_bias)
    low_c = lax.broadcasted_iota(jnp.int32, (CHUNK, LANES), 1) < HEAD_DIM
    rest_piece = 0

    def project_rest_piece(piece):
        cols = slice(piece * REST_PIECE, (piece + 1) * REST_PIECE)
        lo = C_XB + piece * REST_PIECE
        val = proj(lo, REST_PIECE)
        if C_GB <= lo < C_U:
            mixbuf[:, D_ATT + lo - C_GB:D_ATT + lo - C_GB + REST_PIECE] = _silu(val).astype(BF16)
        elif lo >= C_GC:
            at = D_ATT + D_POOL + lo - C_GC
            mixbuf[:, at:at + REST_PIECE] = _silu(val).astype(BF16)
        else:
            at = lo - C_XB if lo < C_GB else (Z_U + lo - C_U if lo < C_VS else Z_VS + lo - C_VS)
            zrest[:, at:at + REST_PIECE] = val

    def scores(c, g):
        r0 = c * CHUNK
        qs = jnp.concatenate(
            [qbuf[r0:r0 + CHUNK, (2 * g + i) * LANES:(2 * g + i + 1) * LANES] for i in range(2)],
            axis=0)
        return [_dot_nt(qs, kbuf[2 * g + half, base + r0:base + r0 + 2 * WINDOW, :]) for half in range(2)]

    order = [(c, g) for c in range(N_CHUNKS) for g in range(N_KV)]
    while rest_piece < N_REST_PIECES - len(order):
        project_rest_piece(rest_piece)
        rest_piece += 1
    s_next = scores(*order[0])
    for step, (c, g) in enumerate(order):
        bias = first_bias if c == 0 else band_bias
        r0 = c * CHUNK
        s_half = s_next
        if step + 1 < len(order):
            s_next = scores(*order[step + 1])
        blocks = []
        for r in range(GROUP):
            sb = s_half[r % 2][(r // 2) * CHUNK:(r // 2 + 1) * CHUNK] + bias
            sink = sinks_ref[layer, GROUP * g + r] * LOG2_E
            m = jnp.maximum(jnp.max(sb, axis=1, keepdims=True), sink)
            pe = jnp.exp2(sb - m)
            den = jnp.sum(pe, axis=1, keepdims=True) + jnp.exp2(sink - m)
            blocks.append((pe * (1.0 / den)).astype(BF16))
        pg = jnp.concatenate(blocks, axis=0)
        og = _dot(pg, vbuf[g, base + r0:base + r0 + 2 * WINDOW, :])
        o = og if g == 0 else o + og
        if rest_piece < N_REST_PIECES:
            project_rest_piece(rest_piece)
            rest_piece += 1
        if g < N_KV - 1:
            continue
        for pair in range(2):
            for i in range(2):
                a = o[(2 * i) * CHUNK:(2 * i + 1) * CHUNK, pair * LANES:(pair + 1) * LANES]
                b = o[(2 * i + 1) * CHUNK:(2 * i + 2) * CHUNK, pair * LANES:(pair + 1) * LANES]
                for odd, blk in ((0, jnp.where(low_c, a, _swap_heads(b))),
                                 (1, jnp.where(low_c, _swap_heads(a), b))):
                    cols = slice((2 * (2 * pair + odd) + i) * LANES, (2 * (2 * pair + odd) + i + 1) * LANES)
                    mixbuf[r0:r0 + CHUNK, cols] = (blk * mixbuf[r0:r0 + CHUNK, cols]).astype(BF16)

    while rest_piece < N_REST_PIECES:
        project_rest_piece(rest_piece)
        rest_piece += 1

    xb0 = HIST_PAD + base
    xbext[xb0:xb0 + SUB, :] = zrest[:, 0:D_POOL]
    if sub == N_SUB - 1:
        np_ref[0] = xbext[HIST_PAD + TM - POOL_HIST:HIST_PAD + TM, :]
    pos1 = j * TM + base + lax.broadcasted_iota(jnp.int32, (SUB, LANES), 0) + 1
    for gi, w in enumerate(POOL_WINDOWS):
        cols = slice(gi * LANES, (gi + 1) * LANES)
        cur = xbext[xb0:xb0 + SUB, cols]
        acc = cur
        for i in range(1, w):
            acc = acc + xbext[xb0 - i:xb0 - i + SUB, cols]
        cnt = jnp.minimum(w, pos1).astype(F32)
        pooled = acc / cnt - cur
        mixed = _dot(pooled.astype(BF16), pw_ref[gi]) * ps_ref[:, cols]
        out_cols = slice(D_ATT + gi * LANES, D_ATT + (gi + 1) * LANES)
        mixbuf[:, out_cols] = (mixed * mixbuf[:, out_cols]).astype(BF16)

    vn = _layernorm(zrest[:, Z_VS:Z_VS + D_SGU], lng_ref[...], lnb_ref[...]).astype(BF16)
    ti = lax.broadcasted_iota(jnp.int32, (CHUNK, CHUNK), 0)
    si = lax.broadcasted_iota(jnp.int32, (CHUNK, CHUNK), 1)
    for hh in range(N_SGU_HEADS):
        cols = slice(hh * LANES, (hh + 1) * LANES)
        wm = jnp.where(ti >= si, sw_ref[hh], 0.0).astype(BF16)
        for c in range(N_CHUNKS):
            rows = slice(c * CHUNK, (c + 1) * CHUNK)
            mixed = _dot(wm, vn[rows, cols]) + sb_ref[hh]
            out_cols = slice(D_ATT + D_POOL + hh * LANES, D_ATT + D_POOL + (hh + 1) * LANES)
            mixbuf[rows, out_cols] = (
                zrest[rows, Z_U + hh * LANES:Z_U + (hh + 1) * LANES] * mixed * mixbuf[rows, out_cols]
            ).astype(BF16)

    slot = jnp.minimum(j, 0)
    y = _dot(mixbuf[...], w_out_ref[...])
    ybuf[slot] = y
    if sub < N_SUB - 1:
        y = ybuf[slot]
    y_ref[0, sub_rows] = x_ref[0, sub_rows] + _rms(y, npost_ref[...])


def _layer_spec(shape, layer, grid_rank):
    zeros = (0,) * len(shape)
    if grid_rank == 2:
        index_map = lambda b, j: (layer,) + zeros
    else:
        index_map = lambda i: (layer,) + zeros
    return pl.BlockSpec((None,) + tuple(shape), index_map, pipeline_mode=pl.Buffered(1))


def _prompt_layer(layer, x, cos, sin, params, prev_outs):
    (sinks, w_in_b, w_out_b, npre, npost, pw_b, ps, lng, lnb, sw, sb_full) = params
    grid = (BATCH, SEQ // TM)
    n_alias = len(prev_outs)
    in_specs = [
        pl.BlockSpec(memory_space=pltpu.SMEM),
        pl.BlockSpec((1, TM, D_MODEL), lambda b, j: (b, j, 0)),
        pl.BlockSpec((TM, LANES), lambda b, j: (j, 0)),
        pl.BlockSpec((TM, LANES), lambda b, j: (j, 0)),
        _layer_spec((D_MODEL, D_IN), 0, 2),
        _layer_spec((D_MODEL, D_MODEL), 0, 2),
        _layer_spec((1, D_MODEL), layer, 2),
        _layer_spec((1, D_MODEL), layer, 2),
        _layer_spec((len(POOL_WINDOWS), LANES, LANES), layer, 2),
        _layer_spec((1, D_POOL), layer, 2),
        _layer_spec((1, D_SGU), layer, 2),
        _layer_spec((1, D_SGU), layer, 2),
        _layer_spec((N_SGU_HEADS, CHUNK, CHUNK), layer, 2),
        _layer_spec((N_SGU_HEADS, CHUNK, LANES), layer, 2),
    ] + [pl.BlockSpec(memory_space=pl.ANY)] * n_alias
    out_specs = [
        pl.BlockSpec((1, TM, D_MODEL), lambda b, j: (b, j, 0)),
        pl.BlockSpec((None, 1, D_KV, WINDOW), lambda b, j: (layer, b, 0, 0)),
        pl.BlockSpec((None, 1, D_KV, WINDOW), lambda b, j: (layer, b, 0, 0)),
        pl.BlockSpec((None, 1, POOL_HIST, D_POOL), lambda b, j: (layer, b, 0, 0)),
    ]
    out_shape = [
        jax.ShapeDtypeStruct((BATCH, SEQ, D_MODEL), F32),
        jax.ShapeDtypeStruct((DEPTH, BATCH, D_KV, WINDOW), F32),
        jax.ShapeDtypeStruct((DEPTH, BATCH, D_KV, WINDOW), F32),
        jax.ShapeDtypeStruct((DEPTH, BATCH, POOL_HIST, D_POOL), F32),
    ]
    scratch = [
        pltpu.VMEM((SUB, D_ATT), BF16),
        pltpu.VMEM((2 * N_KV, WINDOW + TM, LANES), BF16),
        pltpu.VMEM((N_KV, WINDOW + TM, D_KV), BF16),
        pltpu.VMEM((HIST_PAD + TM, D_POOL), F32),
        pltpu.VMEM((SUB, D_MODEL), BF16),
        pltpu.VMEM((SUB, Z_WIDTH), F32),
        pltpu.VMEM((1, SUB, D_MODEL), F32),
    ]
    n_in = len(in_specs) - n_alias
    return pl.pallas_call(
        functools.partial(_prompt_body, layer, n_alias),
        grid=grid,
        in_specs=in_specs,
        out_specs=out_specs,
        out_shape=out_shape,
        scratch_shapes=scratch,
        input_output_aliases={n_in + i: 1 + i for i in range(n_alias)},
        compiler_params=pltpu.CompilerParams(
            dimension_semantics=("arbitrary", "arbitrary"),
            vmem_limit_bytes=VMEM_LIMIT_PROMPT),
        name="prompt_layer",
    )(sinks, x, cos, sin, w_in_b, w_out_b, npre, npost, pw_b, ps, lng, lnb, sw, sb_full, *prev_outs)


def _sample_proj_body(x_ref, npre_ref, w_in_ref, w_out_ref, z_ref, w_in_b_ref, w_out_b_ref, hbuf):
    @pl.when(pl.program_id(0) == 0)
    def _norm_once():
        hbuf[...] = _rms(x_ref[...].reshape(SAMPLE_ROWS, D_MODEL), npre_ref[...]).astype(BF16)

    w_tile = w_in_ref[...].astype(BF16)
    w_in_b_ref[0] = w_tile
    w_out_b_ref[0] = w_out_ref[...].astype(BF16)
    z_ref[...] = _dot(hbuf[...], w_tile)


def _sample_proj(layer, xs, npre, w_in, w_out):
    steps = D_IN // S1_TN
    rows_out = D_MODEL // steps
    return pl.pallas_call(
        _sample_proj_body,
        grid=(steps,),
        in_specs=[
            pl.BlockSpec(xs.shape, lambda n: (0,) * xs.ndim),
            pl.BlockSpec((None, 1, D_MODEL), lambda n: (layer, 0, 0)),
            pl.BlockSpec((None, D_MODEL, S1_TN), lambda n: (layer, 0, n)),
            pl.BlockSpec((None, rows_out, D_MODEL), lambda n: (layer, n, 0)),
        ],
        out_specs=[
            pl.BlockSpec((SAMPLE_ROWS, S1_TN), lambda n: (0, n)),
            pl.BlockSpec((1, D_MODEL, S1_TN), lambda n: (0, 0, n)),
            pl.BlockSpec((1, rows_out, D_MODEL), lambda n: (0, n, 0)),
        ],
        out_shape=[
            jax.ShapeDtypeStruct((SAMPLE_ROWS, D_IN), F32),
            jax.ShapeDtypeStruct((1, D_MODEL, D_IN), BF16),
            jax.ShapeDtypeStruct((1, D_MODEL, D_MODEL), BF16),
        ],
        scratch_shapes=[pltpu.VMEM((SAMPLE_ROWS, D_MODEL), BF16)],
        compiler_params=pltpu.CompilerParams(
            dimension_semantics=("arbitrary",), vmem_limit_bytes=VMEM_LIMIT_SAMPLE),
        name="sample_proj",
    )(xs, npre, w_in, w_out)


SCORE_ROWS = N_HEADS * GR
N_REST_BLOCKS = (D_IN - C_XB) // LANES


def _sample_mix_body(n_alias, z_ref, cos_ref, sin_ref, sink_ref, ck_ref, cv_ref, st_ref,
                     pw_ref, ps_ref, lng_ref, lnb_ref, wexp_ref, bexp_ref, *rest):
    (mix_ref, nk_ref, nv_ref, npool_ref, vn_ref,
     qbig, knew_f, vnew_f, knew_t, vnew_t, zc, mo, vo) = rest[n_alias:]
    cos = cos_ref[...]
    sin = sin_ref[...]

    q = _rope(z_ref[:, C_Q:C_Q + D_ATT], cos * ATT_SCALE, sin * ATT_SCALE)
    k_new = _rope(z_ref[:, C_K:C_K + D_KV], cos, sin)
    v_new = z_ref[:, C_V:C_V + D_KV]
    knew_f[...] = k_new
    vnew_f[...] = v_new
    knew_t[...] = k_new.T
    vnew_t[...] = v_new.T
    low = lax.broadcasted_iota(jnp.int32, (RT, LANES), 1) < HEAD_DIM
    zero_blk = jnp.zeros((RT, LANES), F32)
    for h in range(N_HEADS):
        g = h // GROUP
        src = q[:, (h // 2) * LANES:(h // 2 + 1) * LANES]
        if h % 2 != g % 2:
            src = _swap_heads(src)
        piece = jnp.where(low, src, 0.0) if g % 2 == 0 else jnp.where(low, 0.0, src)
        full = [zero_blk, zero_blk]
        full[g // 2] = piece
        qbig[h] = jnp.concatenate(full, axis=1).astype(BF16)

    row_h = lax.broadcasted_iota(jnp.int32, (SCORE_ROWS, GB * WINDOW), 0)
    col_h = lax.broadcasted_iota(jnp.int32, (SCORE_ROWS, GB * WINDOW), 1)
    same_h = ((row_h >> 2) & (GB - 1)) == (col_h >> 7)
    bias_h = jnp.where(same_h, jnp.where((col_h & (WINDOW - 1)) > (row_h & (DEC_SEQ - 1)), 0.0, NEG_INF),
                       NEG_INF)
    row_n = lax.broadcasted_iota(jnp.int32, (SCORE_ROWS, GR), 0)
    col_n = lax.broadcasted_iota(jnp.int32, (SCORE_ROWS, GR), 1)
    same_n = ((row_n >> 2) & (GB - 1)) == (col_n >> 2)
    bias_n = jnp.where(same_n, jnp.where((col_n & (DEC_SEQ - 1)) <= (row_n & (DEC_SEQ - 1)), 0.0, NEG_INF),
                       NEG_INF)
    sink = sink_ref[:, 0:1]
    low_g = lax.broadcasted_iota(jnp.int32, (GR, LANES), 1) < HEAD_DIM
    key_lane = lax.broadcasted_iota(jnp.int32, (D_KV, WINDOW), 1)
    src_key = lax.broadcasted_iota(jnp.int32, (WINDOW, WINDOW), 0)
    dst_key = lax.broadcasted_iota(jnp.int32, (WINDOW, WINDOW), 1)
    shift_keys = jnp.where(src_key == dst_key + DEC_SEQ, 1.0, 0.0).astype(BF16)

    def roll_in(hist_t, new_t, bi):
        hi = hist_t.astype(BF16)
        rest1 = hist_t - hi.astype(F32)
        mid = rest1.astype(BF16)
        lo = (rest1 - mid.astype(F32)).astype(BF16)
        rolled = _dot(hi, shift_keys) + _dot(mid, shift_keys) + _dot(lo, shift_keys)
        placed = pltpu.roll(new_t, (WINDOW - DEC_SEQ - DEC_SEQ * bi) % RT, 1)
        return jnp.where(key_lane >= WINDOW - DEC_SEQ, placed, rolled)

    for grp in range(BT // GB):
        r0 = grp * GR
        lhs = qbig[:, r0:r0 + GR, :].reshape(SCORE_ROWS, D_KV)
        kh_t = jnp.concatenate([ck_ref[grp * GB + bb] for bb in range(GB)], axis=1)
        vh_t = jnp.concatenate([cv_ref[grp * GB + bb] for bb in range(GB)], axis=1)
        kn = knew_f[r0:r0 + GR, :]
        vn_new = vnew_f[r0:r0 + GR, :]
        s_h = _dot(lhs, kh_t.astype(BF16)) + bias_h
        s_n = _dot_nt(lhs, kn.astype(BF16)) + bias_n
        m = jnp.maximum(jnp.maximum(jnp.max(s_h, axis=1, keepdims=True),
                                    jnp.max(s_n, axis=1, keepdims=True)), sink)
        p_h = jnp.exp(s_h - m)
        p_n = jnp.exp(s_n - m)
        den = (jnp.sum(p_h, axis=1, keepdims=True) + jnp.sum(p_n, axis=1, keepdims=True)
               + jnp.exp(sink - m))
        inv = 1.0 / den
        o = (_dot_nt((p_h * inv).astype(BF16), vh_t.astype(BF16))
             + _dot((p_n * inv).astype(BF16), vn_new.astype(BF16)))
        for c in range(N_HEADS // 2):
            g = c // 2
            a = o[(2 * c) * GR:(2 * c + 1) * GR, (g // 2) * LANES:(g // 2 + 1) * LANES]
            b = o[(2 * c + 1) * GR:(2 * c + 2) * GR, (g // 2) * LANES:(g // 2 + 1) * LANES]
            blk = jnp.where(low_g, a, _swap_heads(b)) if g % 2 == 0 else jnp.where(low_g, _swap_heads(a), b)
            gate = _silu(z_ref[r0:r0 + GR, C_GA + c * LANES:C_GA + (c + 1) * LANES])
            mix_ref[r0:r0 + GR, c * LANES:(c + 1) * LANES] = blk * gate
        for bb in range(GB):
            bi = grp * GB + bb
            nk_ref[bi] = roll_in(ck_ref[bi], knew_t[...], bi)
            nv_ref[bi] = roll_in(cv_ref[bi], vnew_t[...], bi)

    for c in range(N_REST_BLOCKS):
        zc[c] = z_ref[:, C_XB + c * LANES:C_XB + (c + 1) * LANES]

    def tok(t, lo, width):
        blk0 = (lo - C_XB) // LANES
        parts = [zc[blk0 + c, pl.ds(t, BT, stride=DEC_SEQ), :] for c in range(width // LANES)]
        return parts[0] if len(parts) == 1 else jnp.concatenate(parts, axis=1)

    def ext(i, gi):
        if i < POOL_HIST:
            return st_ref[i, :, gi * LANES:(gi + 1) * LANES]
        return tok(i - POOL_HIST, C_XB + gi * LANES, LANES)

    for gi, w in enumerate(POOL_WINDOWS):
        for s in range(POOL_HIST):
            npool_ref[s, :, gi * LANES:(gi + 1) * LANES] = ext(s + DEC_SEQ, gi)
        pooled = []
        for t in range(DEC_SEQ):
            cur = ext(POOL_HIST + t, gi)
            acc = cur
            for i in range(1, w):
                acc = acc + ext(POOL_HIST + t - i, gi)
            cnt = float(min(w, PAST_LEN + t + 1))
            pooled.append(acc / cnt - cur)
        pooled = jnp.concatenate(pooled, axis=0).astype(BF16)
        mixed = _dot(pooled, pw_ref[gi]) * ps_ref[:, gi * LANES:(gi + 1) * LANES]
        for t in range(DEC_SEQ):
            gate = _silu(tok(t, C_GB + gi * LANES, LANES))
            mo[gi, pl.ds(t, BT, stride=DEC_SEQ), :] = mixed[t * BT:(t + 1) * BT] * gate

    vns = []
    for t in range(DEC_SEQ):
        vn_t = _layernorm(tok(t, C_VS, D_SGU), lng_ref[...], lnb_ref[...])
        for c in range(D_SGU // LANES):
            vo[c, pl.ds(t, BT, stride=DEC_SEQ), :] = vn_t[:, c * LANES:(c + 1) * LANES]
        vns.append(vn_t)
    for t in range(DEC_SEQ):
        mixed = bexp_ref[t:t + 1, :]
        for s in range(t + 1):
            mixed = mixed + wexp_ref[DEC_SEQ * t + s:DEC_SEQ * t + s + 1, :] * vns[s]
        out = tok(t, C_U, D_SGU) * mixed * _silu(tok(t, C_GC, D_SGU))
        for c in range(D_SGU // LANES):
            mo[D_POOL // LANES + c, pl.ds(t, BT, stride=DEC_SEQ), :] = out[:, c * LANES:(c + 1) * LANES]

    for c in range((D_POOL + D_SGU) // LANES):
        mix_ref[:, D_ATT + c * LANES:D_ATT + (c + 1) * LANES] = mo[c]
    for c in range(D_SGU // LANES):
        vn_ref[:, c * LANES:(c + 1) * LANES] = vo[c]


def _sample_mix(layer, z, cos_t, sin_t, params, ck, cv, state2, prev_outs):
    (sink_rows, pw_b, ps, lng, lnb, wexp, bexp) = params
    n_alias = len(prev_outs)
    const = lambda shape: pl.BlockSpec(shape, lambda i: (0,) * len(shape))
    in_specs = [
        pl.BlockSpec((RT, D_IN), lambda i: (i, 0)),
        const((RT, LANES)), const((RT, LANES)),
        _layer_spec((SCORE_ROWS, LANES), layer, 1),
        pl.BlockSpec((None, BT, D_KV, WINDOW), lambda i: (layer, i, 0, 0)),
        pl.BlockSpec((None, BT, D_KV, WINDOW), lambda i: (layer, i, 0, 0)),
        pl.BlockSpec((None, POOL_HIST, BT, D_POOL), lambda i: (layer, 0, i, 0)),
        _layer_spec((len(POOL_WINDOWS), LANES, LANES), layer, 1),
        _layer_spec((1, D_POOL), layer, 1),
        _layer_spec((1, D_SGU), layer, 1),
        _layer_spec((1, D_SGU), layer, 1),
        _layer_spec((DEC_SEQ * DEC_SEQ, D_SGU), layer, 1),
        _layer_spec((DEC_SEQ, D_SGU), layer, 1),
    ] + [pl.BlockSpec(memory_space=pl.ANY)] * n_alias
    out_specs = [
        pl.BlockSpec((RT, D_MODEL), lambda i: (i, 0)),
        pl.BlockSpec((None, BT, D_KV, WINDOW), lambda i: (layer, i, 0, 0)),
        pl.BlockSpec((None, BT, D_KV, WINDOW), lambda i: (layer, i, 0, 0)),
        pl.BlockSpec((None, POOL_HIST, BT, D_POOL), lambda i: (layer, 0, i, 0)),
        pl.BlockSpec((None, RT, D_SGU), lambda i: (layer, i, 0)),
    ]
    out_shape = [
        jax.ShapeDtypeStruct((SAMPLE_ROWS, D_MODEL), F32),
        jax.ShapeDtypeStruct((DEPTH, DEC_BATCH, D_KV, WINDOW), F32),
        jax.ShapeDtypeStruct((DEPTH, DEC_BATCH, D_KV, WINDOW), F32),
        jax.ShapeDtypeStruct((DEPTH, POOL_HIST, DEC_BATCH, D_POOL), F32),
        jax.ShapeDtypeStruct((DEPTH, SAMPLE_ROWS, D_SGU), F32),
    ]
    scratch = [
        pltpu.VMEM((N_HEADS, RT, D_KV), BF16),
        pltpu.VMEM((RT, D_KV), F32),
        pltpu.VMEM((RT, D_KV), F32),
        pltpu.VMEM((D_KV, RT), F32),
        pltpu.VMEM((D_KV, RT), F32),
        pltpu.VMEM((N_REST_BLOCKS, RT, LANES), F32),
        pltpu.VMEM(((D_POOL + D_SGU) // LANES, RT, LANES), F32),
        pltpu.VMEM((D_SGU // LANES, RT, LANES), F32),
    ]
    n_in = len(in_specs) - n_alias
    return pl.pallas_call(
        functools.partial(_sample_mix_body, n_alias),
        grid=(DEC_BATCH // BT,),
        in_specs=in_specs,
        out_specs=out_specs,
        out_shape=out_shape,
        scratch_shapes=scratch,
        input_output_aliases={n_in + i: 1 + i for i in range(n_alias)},
        compiler_params=pltpu.CompilerParams(
            dimension_semantics=("arbitrary",), vmem_limit_bytes=VMEM_LIMIT_SAMPLE),
        name="sample_mix",
    )(z, cos_t, sin_t, sink_rows, ck, cv, state2, pw_b, ps, lng, lnb, wexp, bexp, *prev_outs)


def _sample_out_body(x_ref, mix_ref, w_out_ref, npost_ref, y_ref, acc):
    k = pl.program_id(0)
    part = _dot(mix_ref[...].astype(BF16), w_out_ref[...])

    @pl.when(k == 0)
    def _first():
        acc[...] = part

    @pl.when(k > 0)
    def _accumulate():
        acc[...] = acc[...] + part

    @pl.when(k == S3_STEPS - 1)
    def _finish():
        out = x_ref[...].reshape(SAMPLE_ROWS, D_MODEL) + _rms(acc[...], npost_ref[...])
        y_ref[...] = out.reshape(y_ref.shape)


def _sample_out(layer, xs, mix, w_out_b, npost, out_shape):
    full = lambda shape: pl.BlockSpec(shape, lambda i: (0,) * len(shape))
    k_chunk = D_MODEL // S3_STEPS
    return pl.pallas_call(
        _sample_out_body,
        grid=(S3_STEPS,),
        in_specs=[
            full(xs.shape),
            pl.BlockSpec((SAMPLE_ROWS, k_chunk), lambda i: (0, i)),
            pl.BlockSpec((None, k_chunk, D_MODEL), lambda i: (0, i, 0)),
            pl.BlockSpec((None, 1, D_MODEL), lambda i: (layer, 0, 0)),
        ],
        out_specs=full(out_shape),
        out_shape=jax.ShapeDtypeStruct(out_shape, F32),
        scratch_shapes=[pltpu.VMEM((SAMPLE_ROWS, D_MODEL), F32)],
        compiler_params=pltpu.CompilerParams(
            dimension_semantics=("arbitrary",), vmem_limit_bytes=VMEM_LIMIT_SAMPLE),
        name="sample_out",
    )(xs, mix, w_out_b, npost)


def _rope_tables(pos):
    inv = np.float32(ROPE_THETA) ** (-np.arange(0, HEAD_DIM, 2, dtype=np.float32) / np.float32(HEAD_DIM))
    ang = pos.astype(np.float32)[:, None] * inv[None, :].astype(np.float32)
    c, s = np.cos(ang, dtype=np.float32), np.sin(ang, dtype=np.float32)
    return np.concatenate([c, c, c, c], axis=1), np.concatenate([-s, s, -s, s], axis=1)


def kernel(x_prompt, x_sample, cache_k, cache_v, state_pool, w_in, w_out, norm_pre, norm_post,
           attn_sinks, pool_w, pool_scale, sgu_ln_g, sgu_ln_b, sgu_w, sgu_b):
    cos_p, sin_p = (jnp.asarray(t) for t in _rope_tables(np.arange(SEQ)))
    cos_s, sin_s = _rope_tables(PAST_LEN + np.arange(DEC_SEQ))
    cos_t = jnp.asarray(np.tile(cos_s, (BT, 1)))
    sin_t = jnp.asarray(np.tile(sin_s, (BT, 1)))

    pw_b = pool_w.astype(BF16)
    npre = norm_pre[:, None, :]
    npost = norm_post[:, None, :]
    ps = pool_scale[:, None, :]
    lng = sgu_ln_g[:, None, :]
    lnb = sgu_ln_b[:, None, :]
    sb_full = jnp.broadcast_to(sgu_b[:, :, :, None], (DEPTH, N_SGU_HEADS, CHUNK, LANES))
    wexp = jnp.repeat(sgu_w[:, :, :DEC_SEQ, :DEC_SEQ].transpose(0, 2, 3, 1), LANES, axis=-1
                      ).reshape(DEPTH, DEC_SEQ * DEC_SEQ, D_SGU)
    bexp = jnp.repeat(sgu_b[:, :, :DEC_SEQ].transpose(0, 2, 1), LANES, axis=-1)
    sink_rows = jnp.broadcast_to(attn_sinks[:, :, None, None], (DEPTH, N_HEADS, GR, LANES)
                                 ).reshape(DEPTH, SCORE_ROWS, LANES)

    xp = x_prompt
    xs = x_sample
    to_key_minor = lambda c: c.transpose(0, 1, 3, 4, 2).reshape(DEPTH, DEC_BATCH, D_KV, WINDOW)
    from_key_minor = lambda c: c.reshape(DEPTH, DEC_BATCH, N_KV, HEAD_DIM, WINDOW).transpose(0, 1, 4, 2, 3)
    ck = to_key_minor(cache_k)
    cv = to_key_minor(cache_v)
    state2 = state_pool.transpose(0, 2, 1, 3)
    sample_params = (sink_rows, pw_b, ps, lng, lnb, wexp, bexp)
    p_outs, s_outs = [], []
    for layer in range(DEPTH):
        z, w_in_b, w_out_b = _sample_proj(layer, xs, npre, w_in, w_out)
        prompt_params = (attn_sinks, w_in_b, w_out_b, npre, npost, pw_b, ps, lng, lnb, sgu_w, sb_full)
        xp, *p_outs = _prompt_layer(layer, xp, cos_p, sin_p, prompt_params, p_outs)
        mix, *s_outs = _sample_mix(layer, z, cos_t, sin_t, sample_params, ck, cv, state2, s_outs)
        xs_shape = x_sample.shape if layer == DEPTH - 1 else (SAMPLE_ROWS, D_MODEL)
        xs = _sample_out(layer, xs, mix, w_out_b, npost, xs_shape)

    nk_p, nv_p, np_p = p_outs
    nk_s, nv_s, np_s, vn_s = s_outs
    prompt_kv = lambda c: c.reshape(DEPTH, BATCH, N_KV, HEAD_DIM, WINDOW).transpose(0, 1, 4, 2, 3)
    return (xp, xs,
            prompt_kv(nk_p), prompt_kv(nv_p), np_p,
            from_key_minor(nk_s), from_key_minor(nv_s),
            np_s.transpose(0, 2, 1, 3),
            vn_s.reshape(DEPTH, DEC_BATCH, DEC_SEQ, D_SGU))
```

```python
import functools

import jax
import jax.numpy as jnp
import numpy as np
from jax import lax
from jax.experimental import pallas as pl
from jax.experimental.pallas import tpu as pltpu

D_MODEL = 2048
SEQ = 2048
BATCH = 8
DEPTH = 2
DEC_BATCH = 128
DEC_SEQ = 4
PAST_LEN = 8192
HEAD_DIM = 64
HALF_DIM = HEAD_DIM // 2
N_HEADS = 16
N_KV = 4
GROUP = N_HEADS // N_KV
D_ATT = N_HEADS * HEAD_DIM
D_KV = N_KV * HEAD_DIM
WINDOW = 128
ROPE_THETA = 10000.0
D_POOL = 512
POOL_WINDOWS = (2, 4, 8, 16)
POOL_HIST = 15
D_SGU = 512
CHUNK = 128
N_SGU_HEADS = 4
D_IN = 5120
EPS = 1e-6
ATT_SCALE = HEAD_DIM ** -0.5
LOG2_E = 1.4426950408889634

C_Q = 0
C_K = C_Q + D_ATT
C_V = C_K + D_KV
C_GA = C_V + D_KV
C_XB = C_GA + D_ATT
C_GB = C_XB + D_POOL
C_U = C_GB + D_POOL
C_VS = C_U + D_SGU
C_GC = C_VS + D_SGU
assert C_GC + D_SGU == D_IN

LANES = 128
SUBLANES = 8
HIST_PAD = 16
SUB = 256
N_SUB = 2
TM = SUB * N_SUB
N_CHUNKS = SUB // CHUNK
SAMPLE_ROWS = DEC_BATCH * DEC_SEQ
BT = 32
RT = BT * DEC_SEQ
assert RT == WINDOW
GB = 4
GR = GB * DEC_SEQ
ATT_PIECE = 512
REST_PIECE = 512
N_REST_PIECES = (D_IN - C_XB) // REST_PIECE
Z_U = D_POOL
Z_VS = Z_U + D_SGU
Z_WIDTH = Z_VS + D_SGU
S1_TN = 640
VMEM_LIMIT_PROMPT = 62 * 1024 * 1024
VMEM_LIMIT_SAMPLE = 56 * 1024 * 1024

F32 = jnp.float32
BF16 = jnp.bfloat16
NEG_INF = float("-inf")


def _rms(x, g):
    return x * lax.rsqrt(jnp.mean(x * x, axis=-1, keepdims=True) + EPS) * g


def _silu(x):
    return x * (1.0 / (1.0 + jnp.exp(-x)))


def _layernorm(x, g, b):
    mu = jnp.mean(x, axis=-1, keepdims=True)
    xc = x - mu
    var = jnp.mean(xc * xc, axis=-1, keepdims=True)
    return xc * lax.rsqrt(var + EPS) * g + b


def _swap_heads(x):
    return pltpu.roll(x, HEAD_DIM, 1)


def _rope(x, cos, sin):
    rows = x.shape[0]
    lane = lax.broadcasted_iota(jnp.int32, (rows, LANES), 1)
    first_half = (lane & HALF_DIM) == 0
    outs = []
    for c in range(x.shape[1] // LANES):
        xc = x[:, c * LANES:(c + 1) * LANES]
        partner = jnp.where(first_half,
                            pltpu.roll(xc, LANES - HALF_DIM, 1),
                            pltpu.roll(xc, HALF_DIM, 1))
        outs.append(xc * cos + partner * sin)
    return jnp.concatenate(outs, axis=1)


def _dot(a, b):
    return jnp.dot(a, b, preferred_element_type=F32)


def _dot_nt(a, b):
    return lax.dot_general(a, b, (((1,), (1,)), ((), ())), preferred_element_type=F32)


def _prompt_body(layer, n_alias, sinks_ref, x_ref, cos_ref, sin_ref, w_in_ref, w_out_ref,
                 npre_ref, npost_ref, pw_ref, ps_ref, lng_ref, lnb_ref, sw_ref, sb_ref, *rest):
    refs = rest[n_alias:]
    kbuf, vbuf, xbext = refs[5], refs[6], refs[7]
    j = pl.program_id(1)

    @pl.when(j == 0)
    def _start_of_sequence():
        kbuf[:, 0:WINDOW, :] = jnp.zeros((2 * N_KV, WINDOW, LANES), BF16)
        vbuf[:, 0:WINDOW, :] = jnp.zeros((N_KV, WINDOW, D_KV), BF16)
        xbext[0:HIST_PAD, :] = jnp.zeros((HIST_PAD, D_POOL), F32)

    @pl.when(j > 0)
    def _carry_from_previous_step():
        for i in range(2 * N_KV):
            kbuf[i, 0:WINDOW, :] = kbuf[i, TM:TM + WINDOW, :]
        for g in range(N_KV):
            vbuf[g, 0:WINDOW, :] = vbuf[g, TM:TM + WINDOW, :]
        xbext[0:HIST_PAD, :] = xbext[TM:TM + HIST_PAD, :]

    for sub in range(N_SUB):
        _prompt_sub_block(layer, sub, j, sinks_ref, x_ref, cos_ref, sin_ref, w_in_ref, w_out_ref,
                          npre_ref, npost_ref, pw_ref, ps_ref, lng_ref, lnb_ref, sw_ref, sb_ref, *refs)


def _prompt_sub_block(layer, sub, j, sinks_ref, x_ref, cos_ref, sin_ref, w_in_ref, w_out_ref,
                      npre_ref, npost_ref, pw_ref, ps_ref, lng_ref, lnb_ref, sw_ref, sb_ref,
                      y_ref, nk_ref, nv_ref, np_ref, qbuf, kbuf, vbuf, xbext, mixbuf, zrest, ybuf):
    base = sub * SUB
    sub_rows = slice(base, base + SUB)
    first_in_sequence = (j == 0) if sub == 0 else None
    h = _rms(x_ref[0, sub_rows], npre_ref[...]).astype(BF16)

    def proj(lo, width):
        return _dot(h, w_in_ref[:, lo:lo + width])

    cos = cos_ref[sub_rows, :]
    sin = sin_ref[sub_rows, :]

    k = _rope(proj(C_K, D_KV), cos, sin)
    v = proj(C_V, D_KV)
    cos_q = cos * (ATT_SCALE * LOG2_E)
    sin_q = sin * (ATT_SCALE * LOG2_E)
    for part in range(D_ATT // ATT_PIECE):
        cols = slice(part * ATT_PIECE, (part + 1) * ATT_PIECE)
        qbuf[:, cols] = _rope(proj(C_Q + part * ATT_PIECE, ATT_PIECE), cos_q, sin_q).astype(BF16)
    for part in range(D_ATT // ATT_PIECE):
        cols = slice(part * ATT_PIECE, (part + 1) * ATT_PIECE)
        mixbuf[:, cols] = _silu(proj(C_GA + part * ATT_PIECE, ATT_PIECE)).astype(BF16)
    if sub == N_SUB - 1:
        nk_ref[0] = k[SUB - WINDOW:SUB].T
        nv_ref[0] = v[SUB - WINDOW:SUB].T
    lane_k = lax.broadcasted_iota(jnp.int32, (SUB, LANES), 1)
    low = lane_k < HEAD_DIM
    lane_v = lax.broadcasted_iota(jnp.int32, (SUB, D_KV), 1)
    def store_by_chunk(buf, idx, val):
        val = val.astype(BF16)
        for c in range(N_CHUNKS):
            r0 = WINDOW + base + c * CHUNK
            buf[idx, r0:r0 + CHUNK, :] = val[c * CHUNK:(c + 1) * CHUNK]

    for pair in range(2):
        kp = k[:, pair * LANES:(pair + 1) * LANES]
        ks = _swap_heads(kp)
        store_by_chunk(kbuf, 4 * pair + 0, jnp.where(low, kp, 0.0))
        store_by_chunk(kbuf, 4 * pair + 1, jnp.where(low, 0.0, ks))
        store_by_chunk(kbuf, 4 * pair + 2, jnp.where(low, ks, 0.0))
        store_by_chunk(kbuf, 4 * pair + 3, jnp.where(low, 0.0, kp))
    for g in range(N_KV):
        keep_v = (lane_v >= g * HEAD_DIM) & (lane_v < (g + 1) * HEAD_DIM)
        store_by_chunk(vbuf, g, jnp.where(keep_v, v, 0.0))

    qi = lax.broadcasted_iota(jnp.int32, (CHUNK, 2 * WINDOW), 0)
    kj = lax.broadcasted_iota(jnp.int32, (CHUNK, 2 * WINDOW), 1)
    dist = qi + WINDOW - kj
    band_bias = jnp.where((dist >= 0) & (dist < WINDOW), 0.0, NEG_INF)
    if first_in_sequence is None:
        first_bias = band_bias
    else:
        no_prev = jnp.where(first_in_sequence, WINDOW, 0)
        first_bias = jnp.where(kj < no_prev, NEG_INF, band_bias)
    low_c = lax.broadcasted_iota(jnp.int32, (CHUNK, LANES), 1) < HEAD_DIM
    rest_piece = 0

    def project_rest_piece(piece):
        cols = slice(piece * REST_PIECE, (piece + 1) * REST_PIECE)
        lo = C_XB + piece * REST_PIECE
        val = proj(lo, REST_PIECE)
        if C_GB <= lo < C_U:
            mixbuf[:, D_ATT + lo - C_GB:D_ATT + lo - C_GB + REST_PIECE] = _silu(val).astype(BF16)
        elif lo >= C_GC:
            at = D_ATT + D_POOL + lo - C_GC
            mixbuf[:, at:at + REST_PIECE] = _silu(val).astype(BF16)
        else:
            at = lo - C_XB if lo < C_GB else (Z_U + lo - C_U if lo < C_VS else Z_VS + lo - C_VS)
            zrest[:, at:at + REST_PIECE] = val

    def scores(c, g):
        r0 = c * CHUNK
        qs = jnp.concatenate(
            [qbuf[r0:r0 + CHUNK, (2 * g + i) * LANES:(2 * g + i + 1) * LANES] for i in range(2)],
            axis=0)
        return [_dot_nt(qs, kbuf[2 * g + half, base + r0:base + r0 + 2 * WINDOW, :]) for half in range(2)]

    order = [(c, g) for c in range(N_CHUNKS) for g in range(N_KV)]
    while rest_piece < N_REST_PIECES - len(order):
        project_rest_piece(rest_piece)
        rest_piece += 1
    s_next = scores(*order[0])
    for step, (c, g) in enumerate(order):
        bias = first_bias if c == 0 else band_bias
        r0 = c * CHUNK
        s_half = s_next
        if step + 1 < len(order):
            s_next = scores(*order[step + 1])
        blocks = []
        for r in range(GROUP):
            sb = s_half[r % 2][(r // 2) * CHUNK:(r // 2 + 1) * CHUNK] + bias
            sink = sinks_ref[layer, GROUP * g + r] * LOG2_E
            m = jnp.maximum(jnp.max(sb, axis=1, keepdims=True), sink)
            pe = jnp.exp2(sb - m)
            den = jnp.sum(pe, axis=1, keepdims=True) + jnp.exp2(sink - m)
            blocks.append((pe * (1.0 / den)).astype(BF16))
        pg = jnp.concatenate(blocks, axis=0)
        og = _dot(pg, vbuf[g, base + r0:base + r0 + 2 * WINDOW, :])
        o = og if g == 0 else o + og
        if rest_piece < N_REST_PIECES:
            project_rest_piece(rest_piece)
            rest_piece += 1
        if g < N_KV - 1:
            continue
        for pair in range(2):
            for i in range(2):
                a = o[(2 * i) * CHUNK:(2 * i + 1) * CHUNK, pair * LANES:(pair + 1) * LANES]
                b = o[(2 * i + 1) * CHUNK:(2 * i + 2) * CHUNK, pair * LANES:(pair + 1) * LANES]
                for odd, blk in ((0, jnp.where(low_c, a, _swap_heads(b))),
                                 (1, jnp.where(low_c, _swap_heads(a), b))):
                    cols = slice((2 * (2 * pair + odd) + i) * LANES, (2 * (2 * pair + odd) + i + 1) * LANES)
                    mixbuf[r0:r0 + CHUNK, cols] = (blk * mixbuf[r0:r0 + CHUNK, cols]).astype(BF16)

    while rest_piece < N_REST_PIECES:
        project_rest_piece(rest_piece)
        rest_piece += 1

    xb0 = HIST_PAD + base
    xbext[xb0:xb0 + SUB, :] = zrest[:, 0:D_POOL]
    if sub == N_SUB - 1:
        np_ref[0] = xbext[HIST_PAD + TM - POOL_HIST:HIST_PAD + TM, :]
    pos1 = j * TM + base + lax.broadcasted_iota(jnp.int32, (SUB, LANES), 0) + 1
    for gi, w in enumerate(POOL_WINDOWS):
        cols = slice(gi * LANES, (gi + 1) * LANES)
        cur = xbext[xb0:xb0 + SUB, cols]
        acc = cur
        for i in range(1, w):
            acc = acc + xbext[xb0 - i:xb0 - i + SUB, cols]
        cnt = jnp.minimum(w, pos1).astype(F32)
        pooled = acc / cnt - cur
        mixed = _dot(pooled.astype(BF16), pw_ref[gi]) * ps_ref[:, cols]
        out_cols = slice(D_ATT + gi * LANES, D_ATT + (gi + 1) * LANES)
        mixbuf[:, out_cols] = (mixed * mixbuf[:, out_cols]).astype(BF16)

    vn = _layernorm(zrest[:, Z_VS:Z_VS + D_SGU], lng_ref[...], lnb_ref[...]).astype(BF16)
    ti = lax.broadcasted_iota(jnp.int32, (CHUNK, CHUNK), 0)
    si = lax.broadcasted_iota(jnp.int32, (CHUNK, CHUNK), 1)
    for hh in range(N_SGU_HEADS):
        cols = slice(hh * LANES, (hh + 1) * LANES)
        wm = jnp.where(ti >= si, sw_ref[hh], 0.0).astype(BF16)
        for c in range(N_CHUNKS):
            rows = slice(c * CHUNK, (c + 1) * CHUNK)
            mixed = _dot(wm, vn[rows, cols]) + sb_ref[hh]
            out_cols = slice(D_ATT + D_POOL + hh * LANES, D_ATT + D_POOL + (hh + 1) * LANES)
            mixbuf[rows, out_cols] = (
                zrest[rows, Z_U + hh * LANES:Z_U + (hh + 1) * LANES] * mixed * mixbuf[rows, out_cols]
            ).astype(BF16)

    slot = jnp.minimum(j, 0)
    y = _dot(mixbuf[...], w_out_ref[...])
    ybuf[slot] = y
    if sub < N_SUB - 1:
        y = ybuf[slot]
    y_ref[0, sub_rows] = x_ref[0, sub_rows] + _rms(y, npost_ref[...])


def _layer_spec(shape, layer, grid_rank):
    zeros = (0,) * len(shape)
    if grid_rank == 2:
        index_map = lambda b, j: (layer,) + zeros
    else:
        index_map = lambda i: (layer,) + zeros
    return pl.BlockSpec((None,) + tuple(shape), index_map, pipeline_mode=pl.Buffered(1))


def _prompt_layer(layer, x, cos, sin, params, prev_outs):
    (sinks, w_in_b, w_out_b, npre, npost, pw_b, ps, lng, lnb, sw, sb_full) = params
    grid = (BATCH, SEQ // TM)
    n_alias = len(prev_outs)
    in_specs = [
        pl.BlockSpec(memory_space=pltpu.SMEM),
        pl.BlockSpec((1, TM, D_MODEL), lambda b, j: (b, j, 0)),
        pl.BlockSpec((TM, LANES), lambda b, j: (j, 0)),
        pl.BlockSpec((TM, LANES), lambda b, j: (j, 0)),
        _layer_spec((D_MODEL, D_IN), 0, 2),
        _layer_spec((D_MODEL, D_MODEL), 0, 2),
        _layer_spec((1, D_MODEL), layer, 2),
        _layer_spec((1, D_MODEL), layer, 2),
        _layer_spec((len(POOL_WINDOWS), LANES, LANES), layer, 2),
        _layer_spec((1, D_POOL), layer, 2),
        _layer_spec((1, D_SGU), layer, 2),
        _layer_spec((1, D_SGU), layer, 2),
        _layer_spec((N_SGU_HEADS, CHUNK, CHUNK), layer, 2),
        _layer_spec((N_SGU_HEADS, CHUNK, LANES), layer, 2),
    ] + [pl.BlockSpec(memory_space=pl.ANY)] * n_alias
    out_specs = [
        pl.BlockSpec((1, TM, D_MODEL), lambda b, j: (b, j, 0)),
        pl.BlockSpec((None, 1, D_KV, WINDOW), lambda b, j: (layer, b, 0, 0)),
        pl.BlockSpec((None, 1, D_KV, WINDOW), lambda b, j: (layer, b, 0, 0)),
        pl.BlockSpec((None, 1, POOL_HIST, D_POOL), lambda b, j: (layer, b, 0, 0)),
    ]
    out_shape = [
        jax.ShapeDtypeStruct((BATCH, SEQ, D_MODEL), F32),
        jax.ShapeDtypeStruct((DEPTH, BATCH, D_KV, WINDOW), F32),
        jax.ShapeDtypeStruct((DEPTH, BATCH, D_KV, WINDOW), F32),
        jax.ShapeDtypeStruct((DEPTH, BATCH, POOL_HIST, D_POOL), F32),
    ]
    scratch = [
        pltpu.VMEM((SUB, D_ATT), BF16),
        pltpu.VMEM((2 * N_KV, WINDOW + TM, LANES), BF16),
        pltpu.VMEM((N_KV, WINDOW + TM, D_KV), BF16),
        pltpu.VMEM((HIST_PAD + TM, D_POOL), F32),
        pltpu.VMEM((SUB, D_MODEL), BF16),
        pltpu.VMEM((SUB, Z_WIDTH), F32),
        pltpu.VMEM((1, SUB, D_MODEL), F32),
    ]
    n_in = len(in_specs) - n_alias
    return pl.pallas_call(
        functools.partial(_prompt_body, layer, n_alias),
        grid=grid,
        in_specs=in_specs,
        out_specs=out_specs,
        out_shape=out_shape,
        scratch_shapes=scratch,
        input_output_aliases={n_in + i: 1 + i for i in range(n_alias)},
        compiler_params=pltpu.CompilerParams(
            dimension_semantics=("arbitrary", "arbitrary"),
            vmem_limit_bytes=VMEM_LIMIT_PROMPT),
        name="prompt_layer",
    )(sinks, x, cos, sin, w_in_b, w_out_b, npre, npost, pw_b, ps, lng, lnb, sw, sb_full, *prev_outs)


def _sample_proj_body(x_ref, npre_ref, w_in_ref, w_out_ref, z_ref, w_in_b_ref, w_out_b_ref, hbuf):
    @pl.when(pl.program_id(0) == 0)
    def _norm_once():
        hbuf[...] = _rms(x_ref[...].reshape(SAMPLE_ROWS, D_MODEL), npre_ref[...]).astype(BF16)

    w_tile = w_in_ref[...].astype(BF16)
    w_in_b_ref[0] = w_tile
    w_out_b_ref[0] = w_out_ref[...].astype(BF16)
    z_ref[...] = _dot(hbuf[...], w_tile)


def _sample_proj(layer, xs, npre, w_in, w_out):
    steps = D_IN // S1_TN
    rows_out = D_MODEL // steps
    return pl.pallas_call(
        _sample_proj_body,
        grid=(steps,),
        in_specs=[
            pl.BlockSpec(xs.shape, lambda n: (0,) * xs.ndim),
            pl.BlockSpec((None, 1, D_MODEL), lambda n: (layer, 0, 0)),
            pl.BlockSpec((None, D_MODEL, S1_TN), lambda n: (layer, 0, n)),
            pl.BlockSpec((None, rows_out, D_MODEL), lambda n: (layer, n, 0)),
        ],
        out_specs=[
            pl.BlockSpec((SAMPLE_ROWS, S1_TN), lambda n: (0, n)),
            pl.BlockSpec((1, D_MODEL, S1_TN), lambda n: (0, 0, n)),
            pl.BlockSpec((1, rows_out, D_MODEL), lambda n: (0, n, 0)),
        ],
        out_shape=[
            jax.ShapeDtypeStruct((SAMPLE_ROWS, D_IN), F32),
            jax.ShapeDtypeStruct((1, D_MODEL, D_IN), BF16),
            jax.ShapeDtypeStruct((1, D_MODEL, D_MODEL), BF16),
        ],
        scratch_shapes=[pltpu.VMEM((SAMPLE_ROWS, D_MODEL), BF16)],
        compiler_params=pltpu.CompilerParams(
            dimension_semantics=("arbitrary",), vmem_limit_bytes=VMEM_LIMIT_SAMPLE),
        name="sample_proj",
    )(xs, npre, w_in, w_out)


SCORE_ROWS = N_HEADS * GR
N_REST_BLOCKS = (D_IN - C_XB) // LANES


def _sample_mix_body(n_alias, z_ref, cos_ref, sin_ref, sink_ref, ck_ref, cv_ref, st_ref,
                     pw_ref, ps_ref, lng_ref, lnb_ref, wexp_ref, bexp_ref, *rest):
    (mix_ref, nk_ref, nv_ref, npool_ref, vn_ref,
     qbig, knew_f, vnew_f, knew_t, vnew_t, zc, mo, vo) = rest[n_alias:]
    cos = cos_ref[...]
    sin = sin_ref[...]

    q = _rope(z_ref[:, C_Q:C_Q + D_ATT], cos * ATT_SCALE, sin * ATT_SCALE)
    k_new = _rope(z_ref[:, C_K:C_K + D_KV], cos, sin)
    v_new = z_ref[:, C_V:C_V + D_KV]
    knew_f[...] = k_new
    vnew_f[...] = v_new
    knew_t[...] = k_new.T
    vnew_t[...] = v_new.T
    low = lax.broadcasted_iota(jnp.int32, (RT, LANES), 1) < HEAD_DIM
    zero_blk = jnp.zeros((RT, LANES), F32)
    for h in range(N_HEADS):
        g = h // GROUP
        src = q[:, (h // 2) * LANES:(h // 2 + 1) * LANES]
        if h % 2 != g % 2:
            src = _swap_heads(src)
        piece = jnp.where(low, src, 0.0) if g % 2 == 0 else jnp.where(low, 0.0, src)
        full = [zero_blk, zero_blk]
        full[g // 2] = piece
        qbig[h] = jnp.concatenate(full, axis=1).astype(BF16)

    row_h = lax.broadcasted_iota(jnp.int32, (SCORE_ROWS, GB * WINDOW), 0)
    col_h = lax.broadcasted_iota(jnp.int32, (SCORE_ROWS, GB * WINDOW), 1)
    same_h = ((row_h >> 2) & (GB - 1)) == (col_h >> 7)
    bias_h = jnp.where(same_h, jnp.where((col_h & (WINDOW - 1)) > (row_h & (DEC_SEQ - 1)), 0.0, NEG_INF),
                       NEG_INF)
    row_n = lax.broadcasted_iota(jnp.int32, (SCORE_ROWS, GR), 0)
    col_n = lax.broadcasted_iota(jnp.int32, (SCORE_ROWS, GR), 1)
    same_n = ((row_n >> 2) & (GB - 1)) == (col_n >> 2)
    bias_n = jnp.where(same_n, jnp.where((col_n & (DEC_SEQ - 1)) <= (row_n & (DEC_SEQ - 1)), 0.0, NEG_INF),
                       NEG_INF)
    sink = sink_ref[:, 0:1]
    low_g = lax.broadcasted_iota(jnp.int32, (GR, LANES), 1) < HEAD_DIM
    key_lane = lax.broadcasted_iota(jnp.int32, (D_KV, WINDOW), 1)
    src_key = lax.broadcasted_iota(jnp.int32, (WINDOW, WINDOW), 0)
    dst_key = lax.broadcasted_iota(jnp.int32, (WINDOW, WINDOW), 1)
    shift_keys = jnp.where(src_key == dst_key + DEC_SEQ, 1.0, 0.0).astype(BF16)

    def roll_in(hist_t, new_t, bi):
        hi = hist_t.astype(BF16)
        rest1 = hist_t - hi.astype(F32)
        mid = rest1.astype(BF16)
        lo = (rest1 - mid.astype(F32)).astype(BF16)
        rolled = _dot(hi, shift_keys) + _dot(mid, shift_keys) + _dot(lo, shift_keys)
        placed = pltpu.roll(new_t, (WINDOW - DEC_SEQ - DEC_SEQ * bi) % RT, 1)
        return jnp.where(key_lane >= WINDOW - DEC_SEQ, placed, rolled)

    for grp in range(BT // GB):
        r0 = grp * GR
        lhs = qbig[:, r0:r0 + GR, :].reshape(SCORE_ROWS, D_KV)
        kh_t = jnp.concatenate([ck_ref[grp * GB + bb] for bb in range(GB)], axis=1)
        vh_t = jnp.concatenate([cv_ref[grp * GB + bb] for bb in range(GB)], axis=1)
        kn = knew_f[r0:r0 + GR, :]
        vn_new = vnew_f[r0:r0 + GR, :]
        s_h = _dot(lhs, kh_t.astype(BF16)) + bias_h
        s_n = _dot_nt(lhs, kn.astype(BF16)) + bias_n
        m = jnp.maximum(jnp.maximum(jnp.max(s_h, axis=1, keepdims=True),
                                    jnp.max(s_n, axis=1, keepdims=True)), sink)
        p_h = jnp.exp(s_h - m)
        p_n = jnp.exp(s_n - m)
        den = (jnp.sum(p_h, axis=1, keepdims=True) + jnp.sum(p_n, axis=1, keepdims=True)
               + jnp.exp(sink - m))
        inv = 1.0 / den
        o = (_dot_nt((p_h * inv).astype(BF16), vh_t.astype(BF16))
             + _dot((p_n * inv).astype(BF16), vn_new.astype(BF16)))
        for c in range(N_HEADS // 2):
            g = c // 2
            a = o[(2 * c) * GR:(2 * c + 1) * GR, (g // 2) * LANES:(g // 2 + 1) * LANES]
            b = o[(2 * c + 1) * GR:(2 * c + 2) * GR, (g // 2) * LANES:(g // 2 + 1) * LANES]
            blk = jnp.where(low_g, a, _swap_heads(b)) if g % 2 == 0 else jnp.where(low_g, _swap_heads(a), b)
            gate = _silu(z_ref[r0:r0 + GR, C_GA + c * LANES:C_GA + (c + 1) * LANES])
            mix_ref[r0:r0 + GR, c * LANES:(c + 1) * LANES] = blk * gate
        for bb in range(GB):
            bi = grp * GB + bb
            nk_ref[bi] = roll_in(ck_ref[bi], knew_t[...], bi)
            nv_ref[bi] = roll_in(cv_ref[bi], vnew_t[...], bi)

    for c in range(N_REST_BLOCKS):
        zc[c] = z_ref[:, C_XB + c * LANES:C_XB + (c + 1) * LANES]

    def tok(t, lo, width):
        blk0 = (lo - C_XB) // LANES
        parts = [zc[blk0 + c, pl.ds(t, BT, stride=DEC_SEQ), :] for c in range(width // LANES)]
        return parts[0] if len(parts) == 1 else jnp.concatenate(parts, axis=1)

    def ext(i, gi):
        if i < POOL_HIST:
            return st_ref[i, :, gi * LANES:(gi + 1) * LANES]
        return tok(i - POOL_HIST, C_XB + gi * LANES, LANES)

    for gi, w in enumerate(POOL_WINDOWS):
        for s in range(POOL_HIST):
            npool_ref[s, :, gi * LANES:(gi + 1) * LANES] = ext(s + DEC_SEQ, gi)
        pooled = []
        for t in range(DEC_SEQ):
            cur = ext(POOL_HIST + t, gi)
            acc = cur
            for i in range(1, w):
                acc = acc + ext(POOL_HIST + t - i, gi)
            cnt = float(min(w, PAST_LEN + t + 1))
            pooled.append(acc / cnt - cur)
        pooled = jnp.concatenate(pooled, axis=0).astype(BF16)
        mixed = _dot(pooled, pw_ref[gi]) * ps_ref[:, gi * LANES:(gi + 1) * LANES]
        for t in range(DEC_SEQ):
            gate = _silu(tok(t, C_GB + gi * LANES, LANES))
            mo[gi, pl.ds(t, BT, stride=DEC_SEQ), :] = mixed[t * BT:(t + 1) * BT] * gate

    vns = []
    for t in range(DEC_SEQ):
        vn_t = _layernorm(tok(t, C_VS, D_SGU), lng_ref[...], lnb_ref[...])
        for c in range(D_SGU // LANES):
            vo[c, pl.ds(t, BT, stride=DEC_SEQ), :] = vn_t[:, c * LANES:(c + 1) * LANES]
        vns.append(vn_t)
    for t in range(DEC_SEQ):
        mixed = bexp_ref[t:t + 1, :]
        for s in range(t + 1):
            mixed = mixed + wexp_ref[DEC_SEQ * t + s:DEC_SEQ * t + s + 1, :] * vns[s]
        out = tok(t, C_U, D_SGU) * mixed * _silu(tok(t, C_GC, D_SGU))
        for c in range(D_SGU // LANES):
            mo[D_POOL // LANES + c, pl.ds(t, BT, stride=DEC_SEQ), :] = out[:, c * LANES:(c + 1) * LANES]

    for c in range((D_POOL + D_SGU) // LANES):
        mix_ref[:, D_ATT + c * LANES:D_ATT + (c + 1) * LANES] = mo[c]
    for c in range(D_SGU // LANES):
        vn_ref[:, c * LANES:(c + 1) * LANES] = vo[c]


def _sample_mix(layer, z, cos_t, sin_t, params, ck, cv, state2, prev_outs):
    (sink_rows, pw_b, ps, lng, lnb, wexp, bexp) = params
    n_alias = len(prev_outs)
    const = lambda shape: pl.BlockSpec(shape, lambda i: (0,) * len(shape))
    in_specs = [
        pl.BlockSpec((RT, D_IN), lambda i: (i, 0)),
        const((RT, LANES)), const((RT, LANES)),
        _layer_spec((SCORE_ROWS, LANES), layer, 1),
        pl.BlockSpec((None, BT, D_KV, WINDOW), lambda i: (layer, i, 0, 0)),
        pl.BlockSpec((None, BT, D_KV, WINDOW), lambda i: (layer, i, 0, 0)),
        pl.BlockSpec((None, POOL_HIST, BT, D_POOL), lambda i: (layer, 0, i, 0)),
        _layer_spec((len(POOL_WINDOWS), LANES, LANES), layer, 1),
        _layer_spec((1, D_POOL), layer, 1),
        _layer_spec((1, D_SGU), layer, 1),
        _layer_spec((1, D_SGU), layer, 1),
        _layer_spec((DEC_SEQ * DEC_SEQ, D_SGU), layer, 1),
        _layer_spec((DEC_SEQ, D_SGU), layer, 1),
    ] + [pl.BlockSpec(memory_space=pl.ANY)] * n_alias
    out_specs = [
        pl.BlockSpec((RT, D_MODEL), lambda i: (i, 0)),
        pl.BlockSpec((None, BT, D_KV, WINDOW), lambda i: (layer, i, 0, 0)),
        pl.BlockSpec((None, BT, D_KV, WINDOW), lambda i: (layer, i, 0, 0)),
        pl.BlockSpec((None, POOL_HIST, BT, D_POOL), lambda i: (layer, 0, i, 0)),
        pl.BlockSpec((None, RT, D_SGU), lambda i: (layer, i, 0)),
    ]
    out_shape = [
        jax.ShapeDtypeStruct((SAMPLE_ROWS, D_MODEL), F32),
        jax.ShapeDtypeStruct((DEPTH, DEC_BATCH, D_KV, WINDOW), F32),
        jax.ShapeDtypeStruct((DEPTH, DEC_BATCH, D_KV, WINDOW), F32),
        jax.ShapeDtypeStruct((DEPTH, POOL_HIST, DEC_BATCH, D_POOL), F32),
        jax.ShapeDtypeStruct((DEPTH, SAMPLE_ROWS, D_SGU), F32),
    ]
    scratch = [
        pltpu.VMEM((N_HEADS, RT, D_KV), BF16),
        pltpu.VMEM((RT, D_KV), F32),
        pltpu.VMEM((RT, D_KV), F32),
        pltpu.VMEM((D_KV, RT), F32),
        pltpu.VMEM((D_KV, RT), F32),
        pltpu.VMEM((N_REST_BLOCKS, RT, LANES), F32),
        pltpu.VMEM(((D_POOL + D_SGU) // LANES, RT, LANES), F32),
        pltpu.VMEM((D_SGU // LANES, RT, LANES), F32),
    ]
    n_in = len(in_specs) - n_alias
    return pl.pallas_call(
        functools.partial(_sample_mix_body, n_alias),
        grid=(DEC_BATCH // BT,),
        in_specs=in_specs,
        out_specs=out_specs,
        out_shape=out_shape,
        scratch_shapes=scratch,
        input_output_aliases={n_in + i: 1 + i for i in range(n_alias)},
        compiler_params=pltpu.CompilerParams(
            dimension_semantics=("arbitrary",), vmem_limit_bytes=VMEM_LIMIT_SAMPLE),
        name="sample_mix",
    )(z, cos_t, sin_t, sink_rows, ck, cv, state2, pw_b, ps, lng, lnb, wexp, bexp, *prev_outs)


def _sample_out_body(x_ref, mix_ref, w_out_ref, npost_ref, y_ref):
    y = _dot(mix_ref[...].astype(BF16), w_out_ref[...])
    out = x_ref[...].reshape(SAMPLE_ROWS, D_MODEL) + _rms(y, npost_ref[...])
    y_ref[...] = out.reshape(y_ref.shape)


def _sample_out(layer, xs, mix, w_out_b, npost, out_shape):
    full = lambda shape: pl.BlockSpec(shape, lambda i: (0,) * len(shape))
    return pl.pallas_call(
        _sample_out_body,
        grid=(1,),
        in_specs=[
            full(xs.shape),
            full((SAMPLE_ROWS, D_MODEL)),
            pl.BlockSpec((None, D_MODEL, D_MODEL), lambda i: (0, 0, 0)),
            pl.BlockSpec((None, 1, D_MODEL), lambda i: (layer, 0, 0)),
        ],
        out_specs=full(out_shape),
        out_shape=jax.ShapeDtypeStruct(out_shape, F32),
        compiler_params=pltpu.CompilerParams(
            dimension_semantics=("arbitrary",), vmem_limit_bytes=VMEM_LIMIT_SAMPLE),
        name="sample_out",
    )(xs, mix, w_out_b, npost)


def _rope_tables(pos):
    inv = np.float32(ROPE_THETA) ** (-np.arange(0, HEAD_DIM, 2, dtype=np.float32) / np.float32(HEAD_DIM))
    ang = pos.astype(np.float32)[:, None] * inv[None, :].astype(np.float32)
    c, s = np.cos(ang, dtype=np.float32), np.sin(ang, dtype=np.float32)
    return np.concatenate([c, c, c, c], axis=1), np.concatenate([-s, s, -s, s], axis=1)


def kernel(x_prompt, x_sample, cache_k, cache_v, state_pool, w_in, w_out, norm_pre, norm_post,
           attn_sinks, pool_w, pool_scale, sgu_ln_g, sgu_ln_b, sgu_w, sgu_b):
    cos_p, sin_p = (jnp.asarray(t) for t in _rope_tables(np.arange(SEQ)))
    cos_s, sin_s = _rope_tables(PAST_LEN + np.arange(DEC_SEQ))
    cos_t = jnp.asarray(np.tile(cos_s, (BT, 1)))
    sin_t = jnp.asarray(np.tile(sin_s, (BT, 1)))

    pw_b = pool_w.astype(BF16)
    npre = norm_pre[:, None, :]
    npost = norm_post[:, None, :]
    ps = pool_scale[:, None, :]
    lng = sgu_ln_g[:, None, :]
    lnb = sgu_ln_b[:, None, :]
    sb_full = jnp.broadcast_to(sgu_b[:, :, :, None], (DEPTH, N_SGU_HEADS, CHUNK, LANES))
    wexp = jnp.repeat(sgu_w[:, :, :DEC_SEQ, :DEC_SEQ].transpose(0, 2, 3, 1), LANES, axis=-1
                      ).reshape(DEPTH, DEC_SEQ * DEC_SEQ, D_SGU)
    bexp = jnp.repeat(sgu_b[:, :, :DEC_SEQ].transpose(0, 2, 1), LANES, axis=-1)
    sink_rows = jnp.broadcast_to(attn_sinks[:, :, None, None], (DEPTH, N_HEADS, GR, LANES)
                                 ).reshape(DEPTH, SCORE_ROWS, LANES)

    xp = x_prompt
    xs = x_sample
    to_key_minor = lambda c: c.transpose(0, 1, 3, 4, 2).reshape(DEPTH, DEC_BATCH, D_KV, WINDOW)
    from_key_minor = lambda c: c.reshape(DEPTH, DEC_BATCH, N_KV, HEAD_DIM, WINDOW).transpose(0, 1, 4, 2, 3)
    ck = to_key_minor(cache_k)
    cv = to_key_minor(cache_v)
    state2 = state_pool.transpose(0, 2, 1, 3)
    sample_params = (sink_rows, pw_b, ps, lng, lnb, wexp, bexp)
    p_outs, s_outs = [], []
    for layer in range(DEPTH):
        z, w_in_b, w_out_b = _sample_proj(layer, xs, npre, w_in, w_out)
        prompt_params = (attn_sinks, w_in_b, w_out_b, npre, npost, pw_b, ps, lng, lnb, sgu_w, sb_full)
        xp, *p_outs = _prompt_layer(layer, xp, cos_p, sin_p, prompt_params, p_outs)
        mix, *s_outs = _sample_mix(layer, z, cos_t, sin_t, sample_params, ck, cv, state2, s_outs)
        xs_shape = x_sample.shape if layer == DEPTH - 1 else (SAMPLE_ROWS, D_MODEL)
        xs = _sample_out(layer, xs, mix, w_out_b, npost, xs_shape)

    nk_p, nv_p, np_p = p_outs
    nk_s, nv_s, np_s, vn_s = s_outs
    prompt_kv = lambda c: c.reshape(DEPTH, BATCH, N_KV, HEAD_DIM, WINDOW).transpose(0, 1, 4, 2, 3)
    return (xp, xs,
            prompt_kv(nk_p), prompt_kv(nv_p), np_p,
            from_key_minor(nk_s), from_key_minor(nv_s),
            np_s.transpose(0, 2, 1, 3),
            vn_s.reshape(DEPTH, DEC_BATCH, DEC_SEQ, D_SGU))
```

```python
import functools

import jax
import jax.numpy as jnp
import numpy as np
from jax import lax
from jax.experimental import pallas as pl
from jax.experimental.pallas import tpu as pltpu

D_MODEL = 2048
SEQ = 2048
BATCH = 8
DEPTH = 2
DEC_BATCH = 128
DEC_SEQ = 4
PAST_LEN = 8192
HEAD_DIM = 64
HALF_DIM = HEAD_DIM // 2
N_HEADS = 16
N_KV = 4
GROUP = N_HEADS // N_KV
D_ATT = N_HEADS * HEAD_DIM
D_KV = N_KV * HEAD_DIM
WINDOW = 128
ROPE_THETA = 10000.0
D_POOL = 512
POOL_WINDOWS = (2, 4, 8, 16)
POOL_HIST = 15
D_SGU = 512
CHUNK = 128
N_SGU_HEADS = 4
D_IN = 5120
EPS = 1e-6
ATT_SCALE = HEAD_DIM ** -0.5
LOG2_E = 1.4426950408889634

C_Q = 0
C_K = C_Q + D_ATT
C_V = C_K + D_KV
C_GA = C_V + D_KV
C_XB = C_GA + D_ATT
C_GB = C_XB + D_POOL
C_U = C_GB + D_POOL
C_VS = C_U + D_SGU
C_GC = C_VS + D_SGU
assert C_GC + D_SGU == D_IN

LANES = 128
SUBLANES = 8
HIST_PAD = 16
SUB = 256
N_SUB = 2
TM = SUB * N_SUB
N_CHUNKS = SUB // CHUNK
SAMPLE_ROWS = DEC_BATCH * DEC_SEQ
BT = 32
RT = BT * DEC_SEQ
assert RT == WINDOW
GB = 4
GR = GB * DEC_SEQ
ATT_PIECE = 512
REST_PIECE = 256
N_REST_PIECES = (D_IN - C_XB) // REST_PIECE
Z_U = D_POOL
Z_VS = Z_U + D_SGU
Z_WIDTH = Z_VS + D_SGU
S1_TN = 640
VMEM_LIMIT_PROMPT = 62 * 1024 * 1024
VMEM_LIMIT_SAMPLE = 56 * 1024 * 1024

F32 = jnp.float32
BF16 = jnp.bfloat16
NEG_INF = float("-inf")


def _rms(x, g):
    return x * lax.rsqrt(jnp.mean(x * x, axis=-1, keepdims=True) + EPS) * g


def _silu(x):
    return x * (1.0 / (1.0 + jnp.exp(-x)))


def _layernorm(x, g, b):
    mu = jnp.mean(x, axis=-1, keepdims=True)
    xc = x - mu
    var = jnp.mean(xc * xc, axis=-1, keepdims=True)
    return xc * lax.rsqrt(var + EPS) * g + b


def _swap_heads(x):
    return pltpu.roll(x, HEAD_DIM, 1)


def _rope(x, cos, sin):
    rows = x.shape[0]
    lane = lax.broadcasted_iota(jnp.int32, (rows, LANES), 1)
    first_half = (lane & HALF_DIM) == 0
    outs = []
    for c in range(x.shape[1] // LANES):
        xc = x[:, c * LANES:(c + 1) * LANES]
        partner = jnp.where(first_half,
                            pltpu.roll(xc, LANES - HALF_DIM, 1),
                            pltpu.roll(xc, HALF_DIM, 1))
        outs.append(xc * cos + partner * sin)
    return jnp.concatenate(outs, axis=1)


def _dot(a, b):
    return jnp.dot(a, b, preferred_element_type=F32)


def _dot_nt(a, b):
    return lax.dot_general(a, b, (((1,), (1,)), ((), ())), preferred_element_type=F32)


def _prompt_body(layer, n_alias, sinks_ref, x_ref, cos_ref, sin_ref, w_in_ref, w_out_ref,
                 npre_ref, npost_ref, pw_ref, ps_ref, lng_ref, lnb_ref, sw_ref, sb_ref, *rest):
    refs = rest[n_alias:]
    kbuf, vbuf, xbext = refs[5], refs[6], refs[7]
    j = pl.program_id(1)

    @pl.when(j == 0)
    def _start_of_sequence():
        kbuf[:, 0:WINDOW, :] = jnp.zeros((2 * N_KV, WINDOW, LANES), BF16)
        vbuf[:, 0:WINDOW, :] = jnp.zeros((N_KV, WINDOW, D_KV), BF16)
        xbext[0:HIST_PAD, :] = jnp.zeros((HIST_PAD, D_POOL), F32)

    @pl.when(j > 0)
    def _carry_from_previous_step():
        for i in range(2 * N_KV):
            kbuf[i, 0:WINDOW, :] = kbuf[i, TM:TM + WINDOW, :]
        for g in range(N_KV):
            vbuf[g, 0:WINDOW, :] = vbuf[g, TM:TM + WINDOW, :]
        xbext[0:HIST_PAD, :] = xbext[TM:TM + HIST_PAD, :]

    for sub in range(N_SUB):
        _prompt_sub_block(layer, sub, j, sinks_ref, x_ref, cos_ref, sin_ref, w_in_ref, w_out_ref,
                          npre_ref, npost_ref, pw_ref, ps_ref, lng_ref, lnb_ref, sw_ref, sb_ref, *refs)


def _prompt_sub_block(layer, sub, j, sinks_ref, x_ref, cos_ref, sin_ref, w_in_ref, w_out_ref,
                      npre_ref, npost_ref, pw_ref, ps_ref, lng_ref, lnb_ref, sw_ref, sb_ref,
                      y_ref, nk_ref, nv_ref, np_ref, qbuf, kbuf, vbuf, xbext, mixbuf, zrest, ybuf):
    base = sub * SUB
    sub_rows = slice(base, base + SUB)
    first_in_sequence = (j == 0) if sub == 0 else None
    h = _rms(x_ref[0, sub_rows], npre_ref[...]).astype(BF16)

    def proj(lo, width):
        return _dot(h, w_in_ref[:, lo:lo + width])

    cos = cos_ref[sub_rows, :]
    sin = sin_ref[sub_rows, :]

    k = _rope(proj(C_K, D_KV), cos, sin)
    v = proj(C_V, D_KV)
    cos_q = cos * (ATT_SCALE * LOG2_E)
    sin_q = sin * (ATT_SCALE * LOG2_E)
    for part in range(D_ATT // ATT_PIECE):
        cols = slice(part * ATT_PIECE, (part + 1) * ATT_PIECE)
        qbuf[:, cols] = _rope(proj(C_Q + part * ATT_PIECE, ATT_PIECE), cos_q, sin_q).astype(BF16)
    for part in range(D_ATT // ATT_PIECE):
        cols = slice(part * ATT_PIECE, (part + 1) * ATT_PIECE)
        mixbuf[:, cols] = _silu(proj(C_GA + part * ATT_PIECE, ATT_PIECE)).astype(BF16)
    if sub == N_SUB - 1:
        nk_ref[0] = k[SUB - WINDOW:SUB].T
        nv_ref[0] = v[SUB - WINDOW:SUB].T
    lane_k = lax.broadcasted_iota(jnp.int32, (SUB, LANES), 1)
    low = lane_k < HEAD_DIM
    lane_v = lax.broadcasted_iota(jnp.int32, (SUB, D_KV), 1)
    def store_by_chunk(buf, idx, val):
        val = val.astype(BF16)
        for c in range(N_CHUNKS):
            r0 = WINDOW + base + c * CHUNK
            buf[idx, r0:r0 + CHUNK, :] = val[c * CHUNK:(c + 1) * CHUNK]

    for pair in range(2):
        kp = k[:, pair * LANES:(pair + 1) * LANES]
        ks = _swap_heads(kp)
        store_by_chunk(kbuf, 4 * pair + 0, jnp.where(low, kp, 0.0))
        store_by_chunk(kbuf, 4 * pair + 1, jnp.where(low, 0.0, ks))
        store_by_chunk(kbuf, 4 * pair + 2, jnp.where(low, ks, 0.0))
        store_by_chunk(kbuf, 4 * pair + 3, jnp.where(low, 0.0, kp))
    for g in range(N_KV):
        keep_v = (lane_v >= g * HEAD_DIM) & (lane_v < (g + 1) * HEAD_DIM)
        store_by_chunk(vbuf, g, jnp.where(keep_v, v, 0.0))

    qi = lax.broadcasted_iota(jnp.int32, (CHUNK, 2 * WINDOW), 0)
    kj = lax.broadcasted_iota(jnp.int32, (CHUNK, 2 * WINDOW), 1)
    dist = qi + WINDOW - kj
    band_bias = jnp.where((dist >= 0) & (dist < WINDOW), 0.0, NEG_INF)
    if first_in_sequence is None:
        first_bias = band_bias
    else:
        no_prev = jnp.where(first_in_sequence, WINDOW, 0)
        first_bias = jnp.where(kj < no_prev, NEG_INF, band_bias)
    low_c = lax.broadcasted_iota(jnp.int32, (CHUNK, LANES), 1) < HEAD_DIM
    rest_piece = 0

    def project_rest_piece(piece):
        cols = slice(piece * REST_PIECE, (piece + 1) * REST_PIECE)
        lo = C_XB + piece * REST_PIECE
        val = proj(lo, REST_PIECE)
        if C_GB <= lo < C_U:
            mixbuf[:, D_ATT + lo - C_GB:D_ATT + lo - C_GB + REST_PIECE] = _silu(val).astype(BF16)
        elif lo >= C_GC:
            at = D_ATT + D_POOL + lo - C_GC
            mixbuf[:, at:at + REST_PIECE] = _silu(val).astype(BF16)
        else:
            at = lo - C_XB if lo < C_GB else (Z_U + lo - C_U if lo < C_VS else Z_VS + lo - C_VS)
            zrest[:, at:at + REST_PIECE] = val

    def scores(c, g):
        r0 = c * CHUNK
        qs = jnp.concatenate(
            [qbuf[r0:r0 + CHUNK, (2 * g + i) * LANES:(2 * g + i + 1) * LANES] for i in range(2)],
            axis=0)
        return [_dot_nt(qs, kbuf[2 * g + half, base + r0:base + r0 + 2 * WINDOW, :]) for half in range(2)]

    order = [(c, g) for c in range(N_CHUNKS) for g in range(N_KV)]
    while rest_piece < N_REST_PIECES - len(order):
        project_rest_piece(rest_piece)
        rest_piece += 1
    s_next = scores(*order[0])
    for step, (c, g) in enumerate(order):
        bias = first_bias if c == 0 else band_bias
        r0 = c * CHUNK
        s_half = s_next
        if step + 1 < len(order):
            s_next = scores(*order[step + 1])
        blocks = []
        for r in range(GROUP):
            sb = s_half[r % 2][(r // 2) * CHUNK:(r // 2 + 1) * CHUNK] + bias
            sink = sinks_ref[layer, GROUP * g + r] * LOG2_E
            m = jnp.maximum(jnp.max(sb, axis=1, keepdims=True), sink)
            pe = jnp.exp2(sb - m)
            den = jnp.sum(pe, axis=1, keepdims=True) + jnp.exp2(sink - m)
            blocks.append((pe * (1.0 / den)).astype(BF16))
        pg = jnp.concatenate(blocks, axis=0)
        og = _dot(pg, vbuf[g, base + r0:base + r0 + 2 * WINDOW, :])
        o = og if g == 0 else o + og
        if rest_piece < N_REST_PIECES:
            project_rest_piece(rest_piece)
            rest_piece += 1
        if g < N_KV - 1:
            continue
        for pair in range(2):
            for i in range(2):
                a = o[(2 * i) * CHUNK:(2 * i + 1) * CHUNK, pair * LANES:(pair + 1) * LANES]
                b = o[(2 * i + 1) * CHUNK:(2 * i + 2) * CHUNK, pair * LANES:(pair + 1) * LANES]
                for odd, blk in ((0, jnp.where(low_c, a, _swap_heads(b))),
                                 (1, jnp.where(low_c, _swap_heads(a), b))):
                    cols = slice((2 * (2 * pair + odd) + i) * LANES, (2 * (2 * pair + odd) + i + 1) * LANES)
                    mixbuf[r0:r0 + CHUNK, cols] = (blk * mixbuf[r0:r0 + CHUNK, cols]).astype(BF16)

    while rest_piece < N_REST_PIECES:
        project_rest_piece(rest_piece)
        rest_piece += 1

    xb0 = HIST_PAD + base
    xbext[xb0:xb0 + SUB, :] = zrest[:, 0:D_POOL]
    if sub == N_SUB - 1:
        np_ref[0] = xbext[HIST_PAD + TM - POOL_HIST:HIST_PAD + TM, :]
    pos1 = j * TM + base + lax.broadcasted_iota(jnp.int32, (SUB, LANES), 0) + 1
    for gi, w in enumerate(POOL_WINDOWS):
        cols = slice(gi * LANES, (gi + 1) * LANES)
        cur = xbext[xb0:xb0 + SUB, cols]
        acc = cur
        for i in range(1, w):
            acc = acc + xbext[xb0 - i:xb0 - i + SUB, cols]
        cnt = jnp.minimum(w, pos1).astype(F32)
        pooled = acc / cnt - cur
        mixed = _dot(pooled.astype(BF16), pw_ref[gi]) * ps_ref[:, cols]
        out_cols = slice(D_ATT + gi * LANES, D_ATT + (gi + 1) * LANES)
        mixbuf[:, out_cols] = (mixed * mixbuf[:, out_cols]).astype(BF16)

    vn = _layernorm(zrest[:, Z_VS:Z_VS + D_SGU], lng_ref[...], lnb_ref[...]).astype(BF16)
    ti = lax.broadcasted_iota(jnp.int32, (CHUNK, CHUNK), 0)
    si = lax.broadcasted_iota(jnp.int32, (CHUNK, CHUNK), 1)
    for hh in range(N_SGU_HEADS):
        cols = slice(hh * LANES, (hh + 1) * LANES)
        wm = jnp.where(ti >= si, sw_ref[hh], 0.0).astype(BF16)
        for c in range(N_CHUNKS):
            rows = slice(c * CHUNK, (c + 1) * CHUNK)
            mixed = _dot(wm, vn[rows, cols]) + sb_ref[hh]
            out_cols = slice(D_ATT + D_POOL + hh * LANES, D_ATT + D_POOL + (hh + 1) * LANES)
            mixbuf[rows, out_cols] = (
                zrest[rows, Z_U + hh * LANES:Z_U + (hh + 1) * LANES] * mixed * mixbuf[rows, out_cols]
            ).astype(BF16)

    slot = jnp.minimum(j, 0)
    y = _dot(mixbuf[...], w_out_ref[...])
    ybuf[slot] = y
    if sub < N_SUB - 1:
        y = ybuf[slot]
    y_ref[0, sub_rows] = x_ref[0, sub_rows] + _rms(y, npost_ref[...])


def _layer_spec(shape, layer, grid_rank):
    zeros = (0,) * len(shape)
    if grid_rank == 2:
        index_map = lambda b, j: (layer,) + zeros
    else:
        index_map = lambda i: (layer,) + zeros
    return pl.BlockSpec((None,) + tuple(shape), index_map, pipeline_mode=pl.Buffered(1))


def _prompt_layer(layer, x, cos, sin, params, prev_outs):
    (sinks, w_in_b, w_out_b, npre, npost, pw_b, ps, lng, lnb, sw, sb_full) = params
    grid = (BATCH, SEQ // TM)
    n_alias = len(prev_outs)
    in_specs = [
        pl.BlockSpec(memory_space=pltpu.SMEM),
        pl.BlockSpec((1, TM, D_MODEL), lambda b, j: (b, j, 0)),
        pl.BlockSpec((TM, LANES), lambda b, j: (j, 0)),
        pl.BlockSpec((TM, LANES), lambda b, j: (j, 0)),
        _layer_spec((D_MODEL, D_IN), 0, 2),
        _layer_spec((D_MODEL, D_MODEL), 0, 2),
        _layer_spec((1, D_MODEL), layer, 2),
        _layer_spec((1, D_MODEL), layer, 2),
        _layer_spec((len(POOL_WINDOWS), LANES, LANES), layer, 2),
        _layer_spec((1, D_POOL), layer, 2),
        _layer_spec((1, D_SGU), layer, 2),
        _layer_spec((1, D_SGU), layer, 2),
        _layer_spec((N_SGU_HEADS, CHUNK, CHUNK), layer, 2),
        _layer_spec((N_SGU_HEADS, CHUNK, LANES), layer, 2),
    ] + [pl.BlockSpec(memory_space=pl.ANY)] * n_alias
    out_specs = [
        pl.BlockSpec((1, TM, D_MODEL), lambda b, j: (b, j, 0)),
        pl.BlockSpec((None, 1, D_KV, WINDOW), lambda b, j: (layer, b, 0, 0)),
        pl.BlockSpec((None, 1, D_KV, WINDOW), lambda b, j: (layer, b, 0, 0)),
        pl.BlockSpec((None, 1, POOL_HIST, D_POOL), lambda b, j: (layer, b, 0, 0)),
    ]
    out_shape = [
        jax.ShapeDtypeStruct((BATCH, SEQ, D_MODEL), F32),
        jax.ShapeDtypeStruct((DEPTH, BATCH, D_KV, WINDOW), F32),
        jax.ShapeDtypeStruct((DEPTH, BATCH, D_KV, WINDOW), F32),
        jax.ShapeDtypeStruct((DEPTH, BATCH, POOL_HIST, D_POOL), F32),
    ]
    scratch = [
        pltpu.VMEM((SUB, D_ATT), BF16),
        pltpu.VMEM((2 * N_KV, WINDOW + TM, LANES), BF16),
        pltpu.VMEM((N_KV, WINDOW + TM, D_KV), BF16),
        pltpu.VMEM((HIST_PAD + TM, D_POOL), F32),
        pltpu.VMEM((SUB, D_MODEL), BF16),
        pltpu.VMEM((SUB, Z_WIDTH), F32),
        pltpu.VMEM((1, SUB, D_MODEL), F32),
    ]
    n_in = len(in_specs) - n_alias
    return pl.pallas_call(
        functools.partial(_prompt_body, layer, n_alias),
        grid=grid,
        in_specs=in_specs,
        out_specs=out_specs,
        out_shape=out_shape,
        scratch_shapes=scratch,
        input_output_aliases={n_in + i: 1 + i for i in range(n_alias)},
        compiler_params=pltpu.CompilerParams(
            dimension_semantics=("arbitrary", "arbitrary"),
            vmem_limit_bytes=VMEM_LIMIT_PROMPT),
        name="prompt_layer",
    )(sinks, x, cos, sin, w_in_b, w_out_b, npre, npost, pw_b, ps, lng, lnb, sw, sb_full, *prev_outs)


def _sample_proj_body(x_ref, npre_ref, w_in_ref, w_out_ref, z_ref, w_in_b_ref, w_out_b_ref, hbuf):
    @pl.when(pl.program_id(0) == 0)
    def _norm_once():
        hbuf[...] = _rms(x_ref[...].reshape(SAMPLE_ROWS, D_MODEL), npre_ref[...]).astype(BF16)

    w_tile = w_in_ref[...].astype(BF16)
    w_in_b_ref[0] = w_tile
    w_out_b_ref[0] = w_out_ref[...].astype(BF16)
    z_ref[...] = _dot(hbuf[...], w_tile)


def _sample_proj(layer, xs, npre, w_in, w_out):
    steps = D_IN // S1_TN
    rows_out = D_MODEL // steps
    return pl.pallas_call(
        _sample_proj_body,
        grid=(steps,),
        in_specs=[
            pl.BlockSpec(xs.shape, lambda n: (0,) * xs.ndim),
            pl.BlockSpec((None, 1, D_MODEL), lambda n: (layer, 0, 0)),
            pl.BlockSpec((None, D_MODEL, S1_TN), lambda n: (layer, 0, n)),
            pl.BlockSpec((None, rows_out, D_MODEL), lambda n: (layer, n, 0)),
        ],
        out_specs=[
            pl.BlockSpec((SAMPLE_ROWS, S1_TN), lambda n: (0, n)),
            pl.BlockSpec((1, D_MODEL, S1_TN), lambda n: (0, 0, n)),
            pl.BlockSpec((1, rows_out, D_MODEL), lambda n: (0, n, 0)),
        ],
        out_shape=[
            jax.ShapeDtypeStruct((SAMPLE_ROWS, D_IN), F32),
            jax.ShapeDtypeStruct((1, D_MODEL, D_IN), BF16),
            jax.ShapeDtypeStruct((1, D_MODEL, D_MODEL), BF16),
        ],
        scratch_shapes=[pltpu.VMEM((SAMPLE_ROWS, D_MODEL), BF16)],
        compiler_params=pltpu.CompilerParams(
            dimension_semantics=("arbitrary",), vmem_limit_bytes=VMEM_LIMIT_SAMPLE),
        name="sample_proj",
    )(xs, npre, w_in, w_out)


SCORE_ROWS = N_HEADS * GR
N_REST_BLOCKS = (D_IN - C_XB) // LANES


def _sample_mix_body(n_alias, z_ref, cos_ref, sin_ref, sink_ref, ck_ref, cv_ref, st_ref,
                     pw_ref, ps_ref, lng_ref, lnb_ref, wexp_ref, bexp_ref, *rest):
    (mix_ref, nk_ref, nv_ref, npool_ref, vn_ref,
     qbig, knew_f, vnew_f, knew_t, vnew_t, zc, mo, vo) = rest[n_alias:]
    cos = cos_ref[...]
    sin = sin_ref[...]

    q = _rope(z_ref[:, C_Q:C_Q + D_ATT], cos * ATT_SCALE, sin * ATT_SCALE)
    k_new = _rope(z_ref[:, C_K:C_K + D_KV], cos, sin)
    v_new = z_ref[:, C_V:C_V + D_KV]
    knew_f[...] = k_new
    vnew_f[...] = v_new
    knew_t[...] = k_new.T
    vnew_t[...] = v_new.T
    low = lax.broadcasted_iota(jnp.int32, (RT, LANES), 1) < HEAD_DIM
    zero_blk = jnp.zeros((RT, LANES), F32)
    for h in range(N_HEADS):
        g = h // GROUP
        src = q[:, (h // 2) * LANES:(h // 2 + 1) * LANES]
        if h % 2 != g % 2:
            src = _swap_heads(src)
        piece = jnp.where(low, src, 0.0) if g % 2 == 0 else jnp.where(low, 0.0, src)
        full = [zero_blk, zero_blk]
        full[g // 2] = piece
        qbig[h] = jnp.concatenate(full, axis=1).astype(BF16)

    row_h = lax.broadcasted_iota(jnp.int32, (SCORE_ROWS, GB * WINDOW), 0)
    col_h = lax.broadcasted_iota(jnp.int32, (SCORE_ROWS, GB * WINDOW), 1)
    same_h = ((row_h >> 2) & (GB - 1)) == (col_h >> 7)
    bias_h = jnp.where(same_h, jnp.where((col_h & (WINDOW - 1)) > (row_h & (DEC_SEQ - 1)), 0.0, NEG_INF),
                       NEG_INF)
    row_n = lax.broadcasted_iota(jnp.int32, (SCORE_ROWS, GR), 0)
    col_n = lax.broadcasted_iota(jnp.int32, (SCORE_ROWS, GR), 1)
    same_n = ((row_n >> 2) & (GB - 1)) == (col_n >> 2)
    bias_n = jnp.where(same_n, jnp.where((col_n & (DEC_SEQ - 1)) <= (row_n & (DEC_SEQ - 1)), 0.0, NEG_INF),
                       NEG_INF)
    sink = sink_ref[:, 0:1]
    low_g = lax.broadcasted_iota(jnp.int32, (GR, LANES), 1) < HEAD_DIM
    key_lane = lax.broadcasted_iota(jnp.int32, (D_KV, WINDOW), 1)
    src = lax.broadcasted_iota(jnp.int32, (2 * WINDOW, WINDOW), 0)
    dst = lax.broadcasted_iota(jnp.int32, (2 * WINDOW, WINDOW), 1)
    from_hist = (src < WINDOW) & (src == dst + DEC_SEQ)

    def split3(x):
        hi = x.astype(BF16)
        rest1 = x - hi.astype(F32)
        mid = rest1.astype(BF16)
        return hi, mid, (rest1 - mid.astype(F32)).astype(BF16)

    new_pieces = {id(knew_t): split3(knew_t[...]), id(vnew_t): split3(vnew_t[...])}

    def roll_in(hist_t, new_ref, bi):
        from_new = (src >= WINDOW) & (dst >= WINDOW - DEC_SEQ) & (
            src - WINDOW == dst - (WINDOW - DEC_SEQ) + DEC_SEQ * bi)
        route = jnp.where(from_hist | from_new, 1.0, 0.0).astype(BF16)
        out = None
        for h_piece, n_piece in zip(split3(hist_t), new_pieces[id(new_ref)]):
            part = _dot(jnp.concatenate([h_piece, n_piece], axis=1), route)
            out = part if out is None else out + part
        return out

    for grp in range(BT // GB):
        r0 = grp * GR
        lhs = qbig[:, r0:r0 + GR, :].reshape(SCORE_ROWS, D_KV)
        kh_t = jnp.concatenate([ck_ref[grp * GB + bb] for bb in range(GB)], axis=1)
        vh_t = jnp.concatenate([cv_ref[grp * GB + bb] for bb in range(GB)], axis=1)
        kn = knew_f[r0:r0 + GR, :]
        vn_new = vnew_f[r0:r0 + GR, :]
        s_h = _dot(lhs, kh_t.astype(BF16)) + bias_h
        s_n = _dot_nt(lhs, kn.astype(BF16)) + bias_n
        m = jnp.maximum(jnp.maximum(jnp.max(s_h, axis=1, keepdims=True),
                                    jnp.max(s_n, axis=1, keepdims=True)), sink)
        p_h = jnp.exp(s_h - m)
        p_n = jnp.exp(s_n - m)
        den = (jnp.sum(p_h, axis=1, keepdims=True) + jnp.sum(p_n, axis=1, keepdims=True)
               + jnp.exp(sink - m))
        inv = 1.0 / den
        o = (_dot_nt((p_h * inv).astype(BF16), vh_t.astype(BF16))
             + _dot((p_n * inv).astype(BF16), vn_new.astype(BF16)))
        for c in range(N_HEADS // 2):
            g = c // 2
            a = o[(2 * c) * GR:(2 * c + 1) * GR, (g // 2) * LANES:(g // 2 + 1) * LANES]
            b = o[(2 * c + 1) * GR:(2 * c + 2) * GR, (g // 2) * LANES:(g // 2 + 1) * LANES]
            blk = jnp.where(low_g, a, _swap_heads(b)) if g % 2 == 0 else jnp.where(low_g, _swap_heads(a), b)
            gate = _silu(z_ref[r0:r0 + GR, C_GA + c * LANES:C_GA + (c + 1) * LANES])
            mix_ref[r0:r0 + GR, c * LANES:(c + 1) * LANES] = blk * gate
        for bb in range(GB):
            bi = grp * GB + bb
            nk_ref[bi] = roll_in(ck_ref[bi], knew_t, bi)
            nv_ref[bi] = roll_in(cv_ref[bi], vnew_t, bi)

    for c in range(N_REST_BLOCKS):
        zc[c] = z_ref[:, C_XB + c * LANES:C_XB + (c + 1) * LANES]

    def tok(t, lo, width):
        blk0 = (lo - C_XB) // LANES
        parts = [zc[blk0 + c, pl.ds(t, BT, stride=DEC_SEQ), :] for c in range(width // LANES)]
        return parts[0] if len(parts) == 1 else jnp.concatenate(parts, axis=1)

    def ext(i, gi):
        if i < POOL_HIST:
            return st_ref[i, :, gi * LANES:(gi + 1) * LANES]
        return tok(i - POOL_HIST, C_XB + gi * LANES, LANES)

    for gi, w in enumerate(POOL_WINDOWS):
        for s in range(POOL_HIST):
            npool_ref[s, :, gi * LANES:(gi + 1) * LANES] = ext(s + DEC_SEQ, gi)
        pooled = []
        for t in range(DEC_SEQ):
            cur = ext(POOL_HIST + t, gi)
            acc = cur
            for i in range(1, w):
                acc = acc + ext(POOL_HIST + t - i, gi)
            cnt = float(min(w, PAST_LEN + t + 1))
            pooled.append(acc / cnt - cur)
        pooled = jnp.concatenate(pooled, axis=0).astype(BF16)
        mixed = _dot(pooled, pw_ref[gi]) * ps_ref[:, gi * LANES:(gi + 1) * LANES]
        for t in range(DEC_SEQ):
            gate = _silu(tok(t, C_GB + gi * LANES, LANES))
            mo[gi, pl.ds(t, BT, stride=DEC_SEQ), :] = mixed[t * BT:(t + 1) * BT] * gate

    vns = []
    for t in range(DEC_SEQ):
        vn_t = _layernorm(tok(t, C_VS, D_SGU), lng_ref[...], lnb_ref[...])
        for c in range(D_SGU // LANES):
            vo[c, pl.ds(t, BT, stride=DEC_SEQ), :] = vn_t[:, c * LANES:(c + 1) * LANES]
        vns.append(vn_t)
    for t in range(DEC_SEQ):
        mixed = bexp_ref[t:t + 1, :]
        for s in range(t + 1):
            mixed = mixed + wexp_ref[DEC_SEQ * t + s:DEC_SEQ * t + s + 1, :] * vns[s]
        out = tok(t, C_U, D_SGU) * mixed * _silu(tok(t, C_GC, D_SGU))
        for c in range(D_SGU // LANES):
            mo[D_POOL // LANES + c, pl.ds(t, BT, stride=DEC_SEQ), :] = out[:, c * LANES:(c + 1) * LANES]

    for c in range((D_POOL + D_SGU) // LANES):
        mix_ref[:, D_ATT + c * LANES:D_ATT + (c + 1) * LANES] = mo[c]
    for c in range(D_SGU // LANES):
        vn_ref[:, c * LANES:(c + 1) * LANES] = vo[c]


def _sample_mix(layer, z, cos_t, sin_t, params, ck, cv, state2, prev_outs):
    (sink_rows, pw_b, ps, lng, lnb, wexp, bexp) = params
    n_alias = len(prev_outs)
    const = lambda shape: pl.BlockSpec(shape, lambda i: (0,) * len(shape))
    in_specs = [
        pl.BlockSpec((RT, D_IN), lambda i: (i, 0)),
        const((RT, LANES)), const((RT, LANES)),
        _layer_spec((SCORE_ROWS, LANES), layer, 1),
        pl.BlockSpec((None, BT, D_KV, WINDOW), lambda i: (layer, i, 0, 0)),
        pl.BlockSpec((None, BT, D_KV, WINDOW), lambda i: (layer, i, 0, 0)),
        pl.BlockSpec((None, POOL_HIST, BT, D_POOL), lambda i: (layer, 0, i, 0)),
        _layer_spec((len(POOL_WINDOWS), LANES, LANES), layer, 1),
        _layer_spec((1, D_POOL), layer, 1),
        _layer_spec((1, D_SGU), layer, 1),
        _layer_spec((1, D_SGU), layer, 1),
        _layer_spec((DEC_SEQ * DEC_SEQ, D_SGU), layer, 1),
        _layer_spec((DEC_SEQ, D_SGU), layer, 1),
    ] + [pl.BlockSpec(memory_space=pl.ANY)] * n_alias
    out_specs = [
        pl.BlockSpec((RT, D_MODEL), lambda i: (i, 0)),
        pl.BlockSpec((None, BT, D_KV, WINDOW), lambda i: (layer, i, 0, 0)),
        pl.BlockSpec((None, BT, D_KV, WINDOW), lambda i: (layer, i, 0, 0)),
        pl.BlockSpec((None, POOL_HIST, BT, D_POOL), lambda i: (layer, 0, i, 0)),
        pl.BlockSpec((None, RT, D_SGU), lambda i: (layer, i, 0)),
    ]
    out_shape = [
        jax.ShapeDtypeStruct((SAMPLE_ROWS, D_MODEL), F32),
        jax.ShapeDtypeStruct((DEPTH, DEC_BATCH, D_KV, WINDOW), F32),
        jax.ShapeDtypeStruct((DEPTH, DEC_BATCH, D_KV, WINDOW), F32),
        jax.ShapeDtypeStruct((DEPTH, POOL_HIST, DEC_BATCH, D_POOL), F32),
        jax.ShapeDtypeStruct((DEPTH, SAMPLE_ROWS, D_SGU), F32),
    ]
    scratch = [
        pltpu.VMEM((N_HEADS, RT, D_KV), BF16),
        pltpu.VMEM((RT, D_KV), F32),
        pltpu.VMEM((RT, D_KV), F32),
        pltpu.VMEM((D_KV, RT), F32),
        pltpu.VMEM((D_KV, RT), F32),
        pltpu.VMEM((N_REST_BLOCKS, RT, LANES), F32),
        pltpu.VMEM(((D_POOL + D_SGU) // LANES, RT, LANES), F32),
        pltpu.VMEM((D_SGU // LANES, RT, LANES), F32),
    ]
    n_in = len(in_specs) - n_alias
    return pl.pallas_call(
        functools.partial(_sample_mix_body, n_alias),
        grid=(DEC_BATCH // BT,),
        in_specs=in_specs,
        out_specs=out_specs,
        out_shape=out_shape,
        scratch_shapes=scratch,
        input_output_aliases={n_in + i: 1 + i for i in range(n_alias)},
        compiler_params=pltpu.CompilerParams(
            dimension_semantics=("arbitrary",), vmem_limit_bytes=VMEM_LIMIT_SAMPLE),
        name="sample_mix",
    )(z, cos_t, sin_t, sink_rows, ck, cv, state2, pw_b, ps, lng, lnb, wexp, bexp, *prev_outs)


def _sample_out_body(x_ref, mix_ref, w_out_ref, npost_ref, y_ref):
    y = _dot(mix_ref[...].astype(BF16), w_out_ref[...])
    out = x_ref[...].reshape(SAMPLE_ROWS, D_MODEL) + _rms(y, npost_ref[...])
    y_ref[...] = out.reshape(y_ref.shape)


def _sample_out(layer, xs, mix, w_out_b, npost, out_shape):
    full = lambda shape: pl.BlockSpec(shape, lambda i: (0,) * len(shape))
    return pl.pallas_call(
        _sample_out_body,
        grid=(1,),
        in_specs=[
            full(xs.shape),
            full((SAMPLE_ROWS, D_MODEL)),
            pl.BlockSpec((None, D_MODEL, D_MODEL), lambda i: (0, 0, 0)),
            pl.BlockSpec((None, 1, D_MODEL), lambda i: (layer, 0, 0)),
        ],
        out_specs=full(out_shape),
        out_shape=jax.ShapeDtypeStruct(out_shape, F32),
        compiler_params=pltpu.CompilerParams(
            dimension_semantics=("arbitrary",), vmem_limit_bytes=VMEM_LIMIT_SAMPLE),
        name="sample_out",
    )(xs, mix, w_out_b, npost)


def _rope_tables(pos):
    inv = np.float32(ROPE_THETA) ** (-np.arange(0, HEAD_DIM, 2, dtype=np.float32) / np.float32(HEAD_DIM))
    ang = pos.astype(np.float32)[:, None] * inv[None, :].astype(np.float32)
    c, s = np.cos(ang, dtype=np.float32), np.sin(ang, dtype=np.float32)
    return np.concatenate([c, c, c, c], axis=1), np.concatenate([-s, s, -s, s], axis=1)


def kernel(x_prompt, x_sample, cache_k, cache_v, state_pool, w_in, w_out, norm_pre, norm_post,
           attn_sinks, pool_w, pool_scale, sgu_ln_g, sgu_ln_b, sgu_w, sgu_b):
    cos_p, sin_p = (jnp.asarray(t) for t in _rope_tables(np.arange(SEQ)))
    cos_s, sin_s = _rope_tables(PAST_LEN + np.arange(DEC_SEQ))
    cos_t = jnp.asarray(np.tile(cos_s, (BT, 1)))
    sin_t = jnp.asarray(np.tile(sin_s, (BT, 1)))

    pw_b = pool_w.astype(BF16)
    npre = norm_pre[:, None, :]
    npost = norm_post[:, None, :]
    ps = pool_scale[:, None, :]
    lng = sgu_ln_g[:, None, :]
    lnb = sgu_ln_b[:, None, :]
    sb_full = jnp.broadcast_to(sgu_b[:, :, :, None], (DEPTH, N_SGU_HEADS, CHUNK, LANES))
    wexp = jnp.repeat(sgu_w[:, :, :DEC_SEQ, :DEC_SEQ].transpose(0, 2, 3, 1), LANES, axis=-1
                      ).reshape(DEPTH, DEC_SEQ * DEC_SEQ, D_SGU)
    bexp = jnp.repeat(sgu_b[:, :, :DEC_SEQ].transpose(0, 2, 1), LANES, axis=-1)
    sink_rows = jnp.broadcast_to(attn_sinks[:, :, None, None], (DEPTH, N_HEADS, GR, LANES)
                                 ).reshape(DEPTH, SCORE_ROWS, LANES)

    xp = x_prompt
    xs = x_sample
    to_key_minor = lambda c: c.transpose(0, 1, 3, 4, 2).reshape(DEPTH, DEC_BATCH, D_KV, WINDOW)
    from_key_minor = lambda c: c.reshape(DEPTH, DEC_BATCH, N_KV, HEAD_DIM, WINDOW).transpose(0, 1, 4, 2, 3)
    ck = to_key_minor(cache_k)
    cv = to_key_minor(cache_v)
    state2 = state_pool.transpose(0, 2, 1, 3)
    sample_params = (sink_rows, pw_b, ps, lng, lnb, wexp, bexp)
    p_outs, s_outs = [], []
    for layer in range(DEPTH):
        z, w_in_b, w_out_b = _sample_proj(layer, xs, npre, w_in, w_out)
        prompt_params = (attn_sinks, w_in_b, w_out_b, npre, npost, pw_b, ps, lng, lnb, sgu_w, sb_full)
        xp, *p_outs = _prompt_layer(layer, xp, cos_p, sin_p, prompt_params, p_outs)
        mix, *s_outs = _sample_mix(layer, z, cos_t, sin_t, sample_params, ck, cv, state2, s_outs)
        xs_shape = x_sample.shape if layer == DEPTH - 1 else (SAMPLE_ROWS, D_MODEL)
        xs = _sample_out(layer, xs, mix, w_out_b, npost, xs_shape)

    nk_p, nv_p, np_p = p_outs
    nk_s, nv_s, np_s, vn_s = s_outs
    prompt_kv = lambda c: c.reshape(DEPTH, BATCH, N_KV, HEAD_DIM, WINDOW).transpose(0, 1, 4, 2, 3)
    return (xp, xs,
            prompt_kv(nk_p), prompt_kv(nv_p), np_p,
            from_key_minor(nk_s), from_key_minor(nv_s),
            np_s.transpose(0, 2, 1, 3),
            vn_s.reshape(DEPTH, DEC_BATCH, DEC_SEQ, D_SGU))
```
